```python
import math
import jax, jax.numpy as jnp
from jax import lax
import numpy as np

D_MODEL = 1024
BATCH = 8
SEQ = 8192
DEPTH = 1
DEC_BATCH = 1
DEC_SEQ = 16384
PAST_LEN = 128

N_META = 16
ATT_HEADS = 4
ATT_QK_DIM = 64
ATT_V_DIM = 2 * ATT_QK_DIM
ATT_WIDTH = ATT_HEADS * ATT_V_DIM
ML_HEADS = 4
ML_DIM = 128
ML_WIDTH = ML_HEADS * ML_DIM
MIX_WIDTH = ATT_WIDTH + ML_WIDTH
N_GATES = 4 * ML_HEADS
N_IN = 3 * ATT_WIDTH + 4 * ML_WIDTH + N_GATES
ROT_DIM = ATT_QK_DIM // 4
ROPE_THETA = 500000.0
Q_BLOCK = 128
CHUNK = 128
CONV_W = 3
PEER_HEADS = 8
PEER_NKEYS = 128
PEER_EXPERTS = PEER_NKEYS * PEER_NKEYS
PEER_QDIM = 256
PEER_TOPK = 16
PEER_BLOCK = 256
EPS = 1e-6

kernel_name = 'hymba_diffattn_mlstm_peer_encoder'


def rmsnorm(x, g):
    xf = x.astype(jnp.float32)
    y = xf * lax.rsqrt(jnp.mean(xf * xf, axis=-1, keepdims=True) + EPS)
    return (y * g.astype(jnp.float32)).astype(x.dtype)


def rope_partial(x, pos):
    half = ROT_DIM // 2
    inv = ROPE_THETA ** (-jnp.arange(half, dtype=jnp.float32) * 2.0 / ROT_DIM)
    ang = pos[:, None] * inv[None, :]
    shape = (1, pos.shape[0]) + (1,) * (x.ndim - 3) + (half,)
    cos = jnp.cos(ang).reshape(shape)
    sin = jnp.sin(ang).reshape(shape)
    xr = x[..., :ROT_DIM].astype(jnp.float32)
    x1, x2 = xr[..., :half], xr[..., half:]
    rot = jnp.concatenate([x1 * cos - x2 * sin, x2 * cos + x1 * sin], axis=-1).astype(x.dtype)
    return jnp.concatenate([rot, x[..., ROT_DIM:]], axis=-1)


def diff_attention(q, k, v, lam, lam_init, g_sub):
    B, H, _, L, dk = q.shape
    dv = v.shape[-1]
    pad = (-L) % Q_BLOCK
    qp = jnp.pad(q, ((0, 0), (0, 0), (0, 0), (pad, 0), (0, 0)))
    nb = (L + pad) // Q_BLOCK
    qb = jnp.moveaxis(qp.reshape(B, H, 2, nb, Q_BLOCK, dk), 3, 0)
    scale = dk ** -0.5

    def block(qblk):
        s = jnp.einsum('bhcqd,bhckd->bhcqk', qblk, k).astype(jnp.float32) * scale
        p = jax.nn.softmax(s, axis=-1)
        a = p[:, :, 0] - lam * p[:, :, 1]
        return jnp.einsum('bhqk,bhkd->bhqd', a.astype(v.dtype), v)

    o = lax.map(block, qb)
    o = jnp.moveaxis(o, 0, 2).reshape(B, H, nb * Q_BLOCK, dv)[:, :, pad:]
    o = rmsnorm(o, g_sub) * (1.0 - lam_init)
    return o.transpose(0, 2, 1, 3).reshape(B, L, H * dv)


def centred_conv(x, w, b):
    y = lax.conv_general_dilated(x, w.astype(x.dtype), window_strides=(1,),
                                 padding=[(CONV_W // 2, CONV_W // 2)],
                                 dimension_numbers=('NWC', 'WIO', 'NWC'),
                                 feature_group_count=x.shape[-1])
    return y + b.astype(x.dtype)


def mlstm_scan(q, k, v, li, lf):
    B, H, L, d = q.shape
    pad = (-L) % CHUNK
    pw = ((0, 0), (0, 0), (pad, 0), (0, 0))
    q = jnp.pad(q, pw)
    k = jnp.pad(k, pw)
    v = jnp.pad(v, pw)
    li = jnp.pad(li, ((0, 0), (0, 0), (pad, 0)), constant_values=-jnp.inf)
    lf = jnp.pad(lf, ((0, 0), (0, 0), (pad, 0)))
    nc = (L + pad) // CHUNK

    def chunks(a):
        return jnp.moveaxis(a.reshape((B, H, nc, CHUNK) + a.shape[3:]), 2, 0)

    causal = jnp.tril(jnp.ones((CHUNK, CHUNK), dtype=bool))

    def step(carry, inp):
        C, n, m = carry
        qc, kc, vc, lic, lfc = inp
        b = jnp.cumsum(lfc, axis=-1)
        logD = jnp.where(causal, b[..., :, None] - b[..., None, :] + lic[..., None, :], -jnp.inf)
        m_inter = b + m[..., None]
        mj = jnp.maximum(m_inter, jnp.max(logD, axis=-1))
        W = jnp.einsum('bhid,bhjd->bhij', qc, kc) * jnp.exp(logD - mj[..., None])
        sc = jnp.exp(m_inter - mj)
        num = sc[..., None] * jnp.einsum('bhvk,bhik->bhiv', C, qc) + jnp.einsum('bhij,bhjv->bhiv', W, vc)
        den = sc * jnp.einsum('bhk,bhik->bhi', n, qc) + jnp.sum(W, axis=-1)
        h = num / jnp.maximum(jnp.abs(den), jnp.exp(-mj))[..., None]
        bL = b[..., -1]
        logw = bL[..., None] - b + lic
        m_new = jnp.maximum(bL + m, jnp.max(logw, axis=-1))
        w = jnp.exp(logw - m_new[..., None])
        decay = jnp.exp(bL + m - m_new)
        C = decay[..., None, None] * C + jnp.einsum('bhj,bhjv,bhjk->bhvk', w, vc, kc)
        n = decay[..., None] * n + jnp.einsum('bhj,bhjk->bhk', w, kc)
        return (C, n, m_new), h

    init = (jnp.zeros((B, H, d, d), jnp.float32), jnp.zeros((B, H, d), jnp.float32),
            jnp.zeros((B, H), jnp.float32))
    _, hs = lax.scan(step, init, (chunks(q), chunks(k), chunks(v), chunks(li), chunks(lf)))
    return jnp.moveaxis(hs, 0, 2).reshape(B, H, nc * CHUNK, d)[:, :, pad:]


def peer(xn, w_pq, sub_k1, sub_k2, u, vv):
    B, L, D = xn.shape
    T = B * L
    pad = (-T) % PEER_BLOCK
    xt = jnp.pad(xn.reshape(T, D), ((0, pad), (0, 0))).reshape(-1, PEER_BLOCK, D)
    half = PEER_QDIM // 2

    def block(xb):
        q = (xb @ w_pq).reshape(PEER_BLOCK, PEER_HEADS, PEER_QDIM)
        s1 = jnp.einsum('thd,hnd->thn', q[..., :half], sub_k1).astype(jnp.float32)
        s2 = jnp.einsum('thd,hnd->thn', q[..., half:], sub_k2).astype(jnp.float32)
        v1, i1 = lax.top_k(s1, PEER_TOPK)
        v2, i2 = lax.top_k(s2, PEER_TOPK)
        cand = (v1[..., :, None] + v2[..., None, :]).reshape(PEER_BLOCK, PEER_HEADS, PEER_TOPK * PEER_TOPK)
        cidx = (i1[..., :, None] * PEER_NKEYS + i2[..., None, :]).reshape(PEER_BLOCK, PEER_HEADS, PEER_TOPK * PEER_TOPK)
        top_s, sel = lax.top_k(cand, PEER_TOPK)
        eidx = jnp.take_along_axis(cidx, sel, axis=-1)
        g = jax.nn.softmax(top_s, axis=-1)
        act = jax.nn.gelu(jnp.einsum('thkd,td->thk', u[eidx], xb).astype(jnp.float32), approximate=False)
        return jnp.einsum('thk,thkd->td', (g * act).astype(vv.dtype), vv[eidx])

    out = lax.map(block, xt).reshape(-1, D)[:T]
    return out.reshape(B, L, D).astype(xn.dtype)


def trunk(x, meta, g_mix, w_in, b_gates, conv_w, conv_b, g_qn, g_kn, lam_q1, lam_k1, lam_q2, lam_k2,
          g_sub, g_mh, w_out, g_ffn, w_pq, sub_k1, sub_k2, peer_u, peer_v):
    B, S, D = x.shape
    L = N_META + S
    h = jnp.concatenate([jnp.broadcast_to(meta.astype(x.dtype)[None], (B, N_META, D)), x], axis=1)
    pos = jnp.arange(L, dtype=jnp.float32)
    sizes = [ATT_WIDTH, ATT_WIDTH, ATT_WIDTH, ML_WIDTH, ML_WIDTH, ML_WIDTH, ML_WIDTH, N_GATES]
    splits = np.cumsum(sizes)[:-1].tolist()
    for l in range(DEPTH):
        lam_init = 0.8 - 0.6 * math.exp(-0.3 * l)
        xn = rmsnorm(h, g_mix[l])
        z = xn @ w_in[l]
        qa, ka, va, qm, km, vm, om, gates = jnp.split(z, splits, axis=-1)
        qa = rope_partial(rmsnorm(qa.reshape(B, L, ATT_HEADS, 2, ATT_QK_DIM), g_qn[l]), pos)
        ka = rope_partial(rmsnorm(ka.reshape(B, L, ATT_HEADS, 2, ATT_QK_DIM), g_kn[l]), pos)
        lam = (jnp.exp(jnp.sum(lam_q1[l].astype(jnp.float32) * lam_k1[l].astype(jnp.float32)))
               - jnp.exp(jnp.sum(lam_q2[l].astype(jnp.float32) * lam_k2[l].astype(jnp.float32))) + lam_init)
        att = diff_attention(qa.transpose(0, 2, 3, 1, 4), ka.transpose(0, 2, 3, 1, 4),
                             va.reshape(B, L, ATT_HEADS, ATT_V_DIM).transpose(0, 2, 1, 3),
                             lam, lam_init, g_sub[l])
        qk_m = jax.nn.silu(centred_conv(jnp.concatenate([qm, km], axis=-1), conv_w[l], conv_b[l]))
        qm, km = jnp.split(qk_m, 2, axis=-1)

        def heads(a):
            return a.reshape(B, L, ML_HEADS, ML_DIM).transpose(0, 2, 1, 3).astype(jnp.float32)

        qh, kh, vh = heads(qm), heads(km) * (ML_DIM ** -0.5), heads(vm)
        gt = (gates + b_gates[l].astype(gates.dtype)).astype(jnp.float32)
        gt = gt.reshape(B, L, 4, ML_HEADS).transpose(2, 0, 3, 1)
        i_f, f_f, i_b, f_b = gt[0], gt[1], gt[2], gt[3]
        fwd = mlstm_scan(qh, kh, vh, i_f, jax.nn.log_sigmoid(f_f))
        bwd = jnp.flip(mlstm_scan(jnp.flip(qh, 2), jnp.flip(kh, 2), jnp.flip(vh, 2),
                                  jnp.flip(i_b, -1), jnp.flip(jax.nn.log_sigmoid(f_b), -1)), 2)
        hm = rmsnorm(fwd + bwd, g_mh[l]).transpose(0, 2, 1, 3).reshape(B, L, ML_WIDTH)
        ml = (hm * jax.nn.sigmoid(om.astype(jnp.float32))).astype(h.dtype)
        h = h + jnp.concatenate([att.astype(h.dtype), ml], axis=-1) @ w_out[l]
        h = h + peer(rmsnorm(h, g_ffn[l]), w_pq[l], sub_k1[l], sub_k2[l], peer_u[l], peer_v[l])
    return h[:, N_META:]


def setup_inputs(seed: int = 0) -> dict:
    key = jax.random.key(seed)
    ks = jax.random.split(key, 24)
    f32 = jnp.float32

    def nrm(k, shape, scale):
        return jax.random.normal(k, shape, f32) * scale

    i_bias = nrm(ks[3], (DEPTH, ML_HEADS), 0.1)
    f_bias = jnp.linspace(3.0, 6.0, ML_HEADS, dtype=f32)[None] + nrm(ks[4], (DEPTH, ML_HEADS), 0.1)
    i_bias_b = nrm(ks[5], (DEPTH, ML_HEADS), 0.1)
    f_bias_b = jnp.linspace(3.0, 6.0, ML_HEADS, dtype=f32)[None] + nrm(ks[6], (DEPTH, ML_HEADS), 0.1)
    return {
        'x_prompt': nrm(ks[0], (BATCH, SEQ, D_MODEL), 1.0),
        'x_sample': nrm(ks[1], (DEC_BATCH, DEC_SEQ, D_MODEL), 1.0),
        'meta': nrm(ks[2], (N_META, D_MODEL), 1.0),
        'g_mix': 1.0 + nrm(ks[7], (DEPTH, D_MODEL), 0.01),
        'w_in': nrm(ks[8], (DEPTH, D_MODEL, N_IN), D_MODEL ** -0.5),
        'b_gates': jnp.concatenate([i_bias, f_bias, i_bias_b, f_bias_b], axis=-1),
        'conv_w': nrm(ks[9], (DEPTH, CONV_W, 1, 2 * ML_WIDTH), CONV_W ** -0.5),
        'conv_b': nrm(ks[10], (DEPTH, 2 * ML_WIDTH), 0.01),
        'g_qn': 1.0 + nrm(ks[11], (DEPTH, ATT_QK_DIM), 0.01),
        'g_kn': 1.0 + nrm(ks[12], (DEPTH, ATT_QK_DIM), 0.01),
        'lam_q1': nrm(ks[13], (DEPTH, ATT_QK_DIM), 0.1),
        'lam_k1': nrm(ks[14], (DEPTH, ATT_QK_DIM), 0.1),
        'lam_q2': nrm(ks[15], (DEPTH, ATT_QK_DIM), 0.1),
        'lam_k2': nrm(ks[16], (DEPTH, ATT_QK_DIM), 0.1),
        'g_sub': 1.0 + nrm(ks[17], (DEPTH, ATT_V_DIM), 0.01),
        'g_mh': 1.0 + nrm(ks[18], (DEPTH, ML_DIM), 0.01),
        'w_out': nrm(ks[19], (DEPTH, MIX_WIDTH, D_MODEL), MIX_WIDTH ** -0.5),
        'g_ffn': 1.0 + nrm(ks[20], (DEPTH, D_MODEL), 0.01),
        'w_pq': nrm(ks[21], (DEPTH, D_MODEL, PEER_HEADS * PEER_QDIM), D_MODEL ** -0.5),
        'sub_k1': nrm(ks[22], (DEPTH, PEER_HEADS, PEER_NKEYS, PEER_QDIM // 2), (PEER_QDIM // 2) ** -0.5),
        'sub_k2': nrm(ks[23], (DEPTH, PEER_HEADS, PEER_NKEYS, PEER_QDIM // 2), (PEER_QDIM // 2) ** -0.5),
        'peer_u': nrm(jax.random.fold_in(key, 101), (DEPTH, PEER_EXPERTS, D_MODEL), D_MODEL ** -0.5),
        'peer_v': nrm(jax.random.fold_in(key, 102), (DEPTH, PEER_EXPERTS, D_MODEL), PEER_HEADS ** -0.5),
    }


def reference(x_prompt, x_sample, meta, g_mix, w_in, b_gates, conv_w, conv_b, g_qn, g_kn,
              lam_q1, lam_k1, lam_q2, lam_k2, g_sub, g_mh, w_out, g_ffn, w_pq, sub_k1, sub_k2,
              peer_u, peer_v):
    y_prompt = trunk(x_prompt, meta, g_mix, w_in, b_gates, conv_w, conv_b, g_qn, g_kn,
                     lam_q1, lam_k1, lam_q2, lam_k2, g_sub, g_mh, w_out, g_ffn, w_pq,
                     sub_k1, sub_k2, peer_u, peer_v)
    y_sample = trunk(x_sample, meta, g_mix, w_in, b_gates, conv_w, conv_b, g_qn, g_kn,
                     lam_q1, lam_k1, lam_q2, lam_k2, g_sub, g_mh, w_out, g_ffn, w_pq,
                     sub_k1, sub_k2, peer_u, peer_v)
    return (y_prompt, y_sample)
```

```python
import functools
import math

import jax
import jax.numpy as jnp
from jax import lax
from jax.experimental import pallas as pl
from jax.experimental.pallas import tpu as pltpu

F32 = jnp.float32
BF16 = jnp.bfloat16

D_MODEL = 1024
N_META = 16
ATT_HEADS = 4
ATT_QK_DIM = 64
ATT_V_DIM = 128
ATT_WIDTH = ATT_HEADS * ATT_V_DIM
ML_HEADS = 4
ML_DIM = 128
ML_WIDTH = ML_HEADS * ML_DIM
N_GATES = 4 * ML_HEADS
N_MAIN = 3 * ATT_WIDTH + 4 * ML_WIDTH
ROT_DIM = ATT_QK_DIM // 4
ROPE_THETA = 500000.0
CHUNK = 128
PEER_HEADS = 8
PEER_NKEYS = 128
PEER_QDIM = 256
PEER_TOPK = 16
EPS = 1e-6
LAM_INIT = 0.8 - 0.6 * math.exp(-0.3 * 0)

LANES = 128
SUBLANES = 8
HALO = SUBLANES
VMEM_LIMIT = 56 * 1024 * 1024

PROJ_TILE = 512
ATT_TQ = 256
ATT_TK = 512
MIX_TILE = 512
PEER_TILE = 512
PEER_GROUP = 1024
NEG_INF = float("-inf")


def _cparams(sem):
    return pltpu.CompilerParams(dimension_semantics=sem, vmem_limit_bytes=VMEM_LIMIT)


def _const_spec(shape):
    nd = len(shape)
    return pl.BlockSpec(shape, lambda *_: (0,) * nd)


def _log_sigmoid(x):
    return -(jnp.maximum(-x, 0.0) + jnp.log1p(jnp.exp(-jnp.abs(x))))


def _proj_kernel(pos_base, tt,
                 x_ref, prev_ref, next_ref, gmix_ref, wmain_ref, wg_ref, wgt_ref, bg_ref, bgt_ref,
                 gqk_ref, inv_ref, convw_ref, convb_ref,
                 q_ref, k_ref, v_ref, qm_ref, km_ref, vm_ref, om_ref, gtm_ref, gfm_ref,
                 xe_ref, zc_ref):
    i = pl.program_id(1)
    xe_ref[0:HALO, :] = prev_ref[0, 0]
    xe_ref[HALO:HALO + tt, :] = x_ref[0]
    xe_ref[HALO + tt:2 * HALO + tt, :] = next_ref[0, 0]
    xe = xe_ref[...]
    ms = jnp.mean(xe * xe, axis=-1, keepdims=True)
    xn_ext = (xe * lax.rsqrt(ms + EPS) * gmix_ref[...]).astype(BF16)
    xn = xn_ext[HALO:HALO + tt]

    def mm(a, lo, hi):
        return jnp.dot(a, wmain_ref[:, lo:hi], preferred_element_type=F32)

    zqk = mm(xn, 0, 2 * ATT_WIDTH)
    sq = (zqk * zqk).astype(BF16)
    r = lax.broadcasted_iota(jnp.int32, (2 * LANES, 2 * LANES), 0) // ATT_QK_DIM
    c = lax.broadcasted_iota(jnp.int32, (2 * LANES, 2 * LANES), 1) // ATT_QK_DIM
    seg_ones = (r == c).astype(BF16)
    lane = lax.broadcasted_iota(jnp.int32, (tt, LANES), 1)
    l64 = lane % ATT_QK_DIM
    row = lax.broadcasted_iota(jnp.int32, (tt, LANES), 0)
    pos = (pos_base + i * tt + row).astype(F32)
    ang = pos * inv_ref[...]
    cos_a = jnp.cos(ang)
    sin_a = jnp.sin(ang)
    half = ROT_DIM // 2
    cos_m = jnp.where(l64 < ROT_DIM, cos_a, 1.0)
    sin_lo = jnp.where(l64 < half, -sin_a, 0.0)
    sin_hi = jnp.where((l64 >= half) & (l64 < ROT_DIM), sin_a, 0.0)
    for cb in range(4):
        lo = cb * 2 * LANES
        ss = jnp.dot(sq[:, lo:lo + 2 * LANES], seg_ones, preferred_element_type=F32)
        rn = lax.rsqrt(ss * (1.0 / ATT_QK_DIM) + EPS)
        qn = zqk[:, lo:lo + 2 * LANES] * rn * gqk_ref[:, lo:lo + 2 * LANES]
        for sb in range(2):
            xs = qn[:, sb * LANES:(sb + 1) * LANES]
            rot = (xs * cos_m + pltpu.roll(xs, LANES - half, 1) * sin_lo
                   + pltpu.roll(xs, half, 1) * sin_hi)
            col = lo + sb * LANES
            if col < ATT_WIDTH:
                q_ref[0, :, col:col + LANES] = (rot * (ATT_QK_DIM ** -0.5)).astype(BF16)
            else:
                k_ref[0, :, col - ATT_WIDTH:col - ATT_WIDTH + LANES] = rot.astype(BF16)

    v_ref[0] = mm(xn, 2 * ATT_WIDTH, 3 * ATT_WIDTH).astype(BF16)

    base = 3 * ATT_WIDTH
    zc_ref[...] = mm(xn_ext, base, base + 2 * ML_WIDTH)
    y = (convw_ref[0:1, :] * zc_ref[pl.ds(HALO - 1, tt), :]
         + convw_ref[1:2, :] * zc_ref[pl.ds(HALO, tt), :]
         + convw_ref[2:3, :] * zc_ref[pl.ds(HALO + 1, tt), :]
         + convb_ref[...])
    act = y * (1.0 / (1.0 + jnp.exp(-y)))
    qm_ref[0] = act[:, :ML_WIDTH].astype(BF16)
    km_ref[0] = (act[:, ML_WIDTH:] * (ML_DIM ** -0.5)).astype(BF16)
    vm_ref[0] = mm(xn, base + 2 * ML_WIDTH, base + 3 * ML_WIDTH).astype(BF16)
    om_ref[0] = mm(xn, base + 3 * ML_WIDTH, base + 4 * ML_WIDTH).astype(BF16)

    g_tm = jnp.dot(xn, wg_ref[...], preferred_element_type=F32) + bg_ref[...]
    g_fm = lax.dot_general(wgt_ref[...], xn, (((1,), (1,)), ((), ())),
                           preferred_element_type=F32) + bgt_ref[...]
    col_tm = lax.broadcasted_iota(jnp.int32, (tt, N_GATES), 1)
    row_fm = lax.broadcasted_iota(jnp.int32, (N_GATES, tt), 0)
    gtm_ref[0] = jnp.where((col_tm // ML_HEADS) % 2 == 1, _log_sigmoid(g_tm), g_tm)
    gfm_ref[0] = jnp.where((row_fm // ML_HEADS) % 2 == 1, _log_sigmoid(g_fm), g_fm)


def _proj(x, prev, nxt, pos_base, tt, w):
    b, s, d = x.shape
    nt = s // tt
    tok = lambda width, dt: jax.ShapeDtypeStruct((b, s, width), dt)
    out_shape = ([tok(ATT_WIDTH, BF16)] * 3 + [tok(ML_WIDTH, BF16)] * 4
                 + [tok(N_GATES, F32), jax.ShapeDtypeStruct((b, N_GATES, s), F32)])
    tspec = lambda width: pl.BlockSpec((1, tt, width), lambda bi, ti: (bi, ti, 0))
    hspec = pl.BlockSpec((1, 1, HALO, d), lambda bi, ti: (bi, ti, 0, 0))
    return pl.pallas_call(
        functools.partial(_proj_kernel, pos_base, tt),
        grid=(b, nt),
        in_specs=[tspec(d), hspec, hspec, _const_spec((1, d)), _const_spec((d, N_MAIN)),
                  _const_spec((d, N_GATES)), _const_spec((N_GATES, d)), _const_spec((1, N_GATES)),
                  _const_spec((N_GATES, 1)), _const_spec((1, 2 * ATT_WIDTH)), _const_spec((1, LANES)),
                  _const_spec((3, 2 * ML_WIDTH)), _const_spec((1, 2 * ML_WIDTH))],
        out_specs=[tspec(ATT_WIDTH)] * 3 + [tspec(ML_WIDTH)] * 4
        + [tspec(N_GATES), pl.BlockSpec((1, N_GATES, tt), lambda bi, ti: (bi, 0, ti))],
        out_shape=out_shape,
        scratch_shapes=[pltpu.VMEM((tt + 2 * HALO, d), F32), pltpu.VMEM((tt + 2 * HALO, 2 * ML_WIDTH), F32)],
        compiler_params=_cparams(("parallel", "parallel")),
        name=f"proj_t{tt}",
    )(x, prev, nxt, w["g_mix"], w["w_main"], w["w_g"], w["w_gt"], w["b_g"], w["b_gt"],
      w["g_qk"], w["inv_lane"], w["conv_w"], w["conv_b"])


def _attn_kernel(tq, tk, nk,
                 q_ref, k_ref, v_ref, kt_ref, vt_ref, lam_ref, gsub_ref, o_ref,
                 qq_ref, m_ref, l_ref, acc_ref):
    q = q_ref[0]
    lane = lax.broadcasted_iota(jnp.int32, (tq, LANES), 1)
    zero = jnp.zeros_like(q)
    qq_ref[0:tq, :] = jnp.where(lane < ATT_QK_DIM, q, zero)
    qq_ref[tq:2 * tq, :] = jnp.where(lane >= ATT_QK_DIM, q, zero)
    m_ref[...] = jnp.full(m_ref.shape, NEG_INF, F32)
    l_ref[...] = jnp.zeros(l_ref.shape, F32)
    acc_ref[...] = jnp.zeros(acc_ref.shape, F32)

    def step(kb, vb, valid):
        s = lax.dot_general(qq_ref[...], kb, (((1,), (1,)), ((), ())), preferred_element_type=F32)
        if valid is not None:
            s = jnp.where(valid, s, NEG_INF)
        m_prev = m_ref[...]
        m_new = jnp.maximum(m_prev, jnp.max(s, axis=1, keepdims=True))
        alpha = jnp.exp(m_prev - m_new)
        p = jnp.exp(s - m_new[:, 0:1])
        l_ref[...] = alpha * l_ref[...] + jnp.sum(p, axis=1, keepdims=True)
        acc_ref[...] = alpha * acc_ref[...] + jnp.dot(p.astype(BF16), vb, preferred_element_type=F32)
        m_ref[...] = m_new

    def body(j, carry):
        off = pl.multiple_of(j * tk, tk)
        step(k_ref[0, pl.ds(off, tk), :], v_ref[0, pl.ds(off, tk), :], None)
        return carry

    lax.fori_loop(0, nk, body, 0)
    col = lax.broadcasted_iota(jnp.int32, (2 * tq, LANES), 1)
    step(kt_ref[0], vt_ref[0], col < N_META)

    o1 = acc_ref[0:tq, :] / l_ref[0:tq, :]
    o2 = acc_ref[tq:2 * tq, :] / l_ref[tq:2 * tq, :]
    lam = (jnp.exp(jnp.sum(lam_ref[0:1, :] * lam_ref[1:2, :], axis=1, keepdims=True))
           - jnp.exp(jnp.sum(lam_ref[2:3, :] * lam_ref[3:4, :], axis=1, keepdims=True)) + LAM_INIT)
    o = o1 - lam * o2
    ms = jnp.mean(o * o, axis=-1, keepdims=True)
    o_ref[0] = (o * lax.rsqrt(ms + EPS) * gsub_ref[...] * (1.0 - LAM_INIT)).astype(BF16)


def _attn(q, k, v, kt, vt, lam4, g_sub):
    b, s, _ = q.shape
    tq, tk = ATT_TQ, ATT_TK
    hspec = lambda rows: pl.BlockSpec((1, rows, LANES), lambda bi, hi, qi: (bi, 0, hi))
    return pl.pallas_call(
        functools.partial(_attn_kernel, tq, tk, s // tk),
        grid=(b, ATT_HEADS, s // tq),
        in_specs=[pl.BlockSpec((1, tq, LANES), lambda bi, hi, qi: (bi, qi, hi)),
                  hspec(s), hspec(s), hspec(LANES), hspec(LANES),
                  _const_spec((4, ATT_QK_DIM)), _const_spec((1, ATT_V_DIM))],
        out_specs=pl.BlockSpec((1, tq, LANES), lambda bi, hi, qi: (bi, qi, hi)),
        out_shape=jax.ShapeDtypeStruct((b, s, ATT_WIDTH), BF16),
        scratch_shapes=[pltpu.VMEM((2 * tq, LANES), BF16), pltpu.VMEM((2 * tq, LANES), F32),
                        pltpu.VMEM((2 * tq, LANES), F32), pltpu.VMEM((2 * tq, LANES), F32)],
        compiler_params=_cparams(("parallel", "parallel", "parallel")),
        name="diff_attn",
    )(q, k, v, kt, vt, lam4, g_sub)


def _mlstm_chunk(reverse, q, k, v, g_tm, g_fm, s_ref, m_ref, h_ref):
    r = lax.broadcasted_iota(jnp.int32, (CHUNK, CHUNK), 0)
    c = lax.broadcasted_iota(jnp.int32, (CHUNK, CHUNK), 1)
    if reverse:
        causal = c >= r
        last = 0
        gbase = 2 * ML_HEADS
    else:
        causal = c <= r
        last = CHUNK - 1
        gbase = 0
    b_cols = jnp.dot(causal.astype(F32), g_tm, preferred_element_type=F32,
                     precision=lax.Precision.HIGHEST)
    b_rows = jnp.dot(g_fm, (r <= c if not reverse else r >= c).astype(F32),
                     preferred_element_type=F32, precision=lax.Precision.HIGHEST)
    ones_col = (c == 0).astype(BF16)
    for h in range(ML_HEADS):
        ic, fc = gbase + h, gbase + ML_HEADS + h
        b_col = b_cols[:, fc:fc + 1]
        b_row = b_rows[fc:fc + 1, :]
        li_row = g_fm[ic:ic + 1, :]
        li_col = g_tm[:, ic:ic + 1]
        m_prev = m_ref[h:h + 1, 0:1]
        log_d = jnp.where(causal, b_col - b_row + li_row, NEG_INF)
        m_inter = b_col + m_prev
        mj = jnp.maximum(m_inter, jnp.max(log_d, axis=1, keepdims=True))
        qh = q[:, h * ML_DIM:(h + 1) * ML_DIM]
        kh = k[:, h * ML_DIM:(h + 1) * ML_DIM]
        vext = jnp.concatenate([v[:, h * ML_DIM:(h + 1) * ML_DIM], ones_col], axis=1)
        qk = lax.dot_general(qh, kh, (((1,), (1,)), ((), ())), preferred_element_type=F32)
        w = (qk * jnp.exp(log_d - mj)).astype(BF16)
        state = s_ref[h]
        tot = (jnp.exp(m_inter - mj) * jnp.dot(qh, state.astype(BF16), preferred_element_type=F32)
               + jnp.dot(w, vext, preferred_element_type=F32))
        den = tot[:, ML_DIM:ML_DIM + 1]
        h_ref[0, :, h * ML_DIM:(h + 1) * ML_DIM] = (
            tot[:, :ML_DIM] / jnp.maximum(jnp.abs(den), jnp.exp(-mj)))
        b_last = b_col[last:last + 1, :]
        logw_col = b_last - b_col + li_col
        logw_row = b_last - b_row + li_row
        m_new = jnp.maximum(b_last + m_prev, jnp.max(logw_row, axis=1, keepdims=True))
        kw = (kh.astype(F32) * jnp.exp(logw_col - m_new)).astype(BF16)
        s_ref[h] = (jnp.exp(b_last + m_prev - m_new) * state
                    + lax.dot_general(kw, vext, (((0,), (0,)), ((), ())), preferred_element_type=F32))
        m_ref[h:h + 1, :] = jnp.broadcast_to(m_new, (1, LANES))


def _mlstm_kernel(qf_ref, kf_ref, vf_ref, gtf_ref, gff_ref, qb_ref, kb_ref, vb_ref, gtb_ref, gfb_ref,
                  s0_ref, m0_ref, hf_ref, hb_ref, sfin_ref, mfin_ref,
                  sf_ref, sb_ref, mf_ref, mb_ref):
    j = pl.program_id(1)

    @pl.when(j == 0)
    def _():
        sf_ref[...] = s0_ref[0]
        mf_ref[...] = m0_ref[0]
        sb_ref[...] = jnp.zeros(sb_ref.shape, F32)
        mb_ref[...] = jnp.zeros(mb_ref.shape, F32)

    _mlstm_chunk(False, qf_ref[0], kf_ref[0], vf_ref[0], gtf_ref[0], gff_ref[0], sf_ref, mf_ref, hf_ref)
    _mlstm_chunk(True, qb_ref[0], kb_ref[0], vb_ref[0], gtb_ref[0], gfb_ref[0], sb_ref, mb_ref, hb_ref)

    @pl.when(j == pl.num_programs(1) - 1)
    def _():
        sfin_ref[0] = sf_ref[...]
        mfin_ref[0] = mf_ref[...]


def _mlstm(qm, km, vm, g_tm, g_fm, s0, m0):
    b, s, _ = qm.shape
    nc = s // CHUNK
    fwd = lambda width: pl.BlockSpec((1, CHUNK, width), lambda bi, ci: (bi, ci, 0))
    bwd = lambda width: pl.BlockSpec((1, CHUNK, width), lambda bi, ci: (bi, nc - 1 - ci, 0))
    gf_fwd = pl.BlockSpec((1, N_GATES, CHUNK), lambda bi, ci: (bi, 0, ci))
    gf_bwd = pl.BlockSpec((1, N_GATES, CHUNK), lambda bi, ci: (bi, 0, nc - 1 - ci))
    st = pl.BlockSpec((1, ML_HEADS, ML_DIM, 2 * LANES), lambda bi, ci: (bi, 0, 0, 0))
    mx = pl.BlockSpec((1, SUBLANES, LANES), lambda bi, ci: (bi, 0, 0))
    return pl.pallas_call(
        _mlstm_kernel,
        grid=(b, nc),
        in_specs=[fwd(ML_WIDTH)] * 3 + [fwd(N_GATES), gf_fwd] + [bwd(ML_WIDTH)] * 3 + [bwd(N_GATES), gf_bwd]
        + [st, mx],
        out_specs=[fwd(ML_WIDTH), bwd(ML_WIDTH), st, mx],
        out_shape=[jax.ShapeDtypeStruct((b, s, ML_WIDTH), F32)] * 2
        + [jax.ShapeDtypeStruct((b, ML_HEADS, ML_DIM, 2 * LANES), F32),
           jax.ShapeDtypeStruct((b, SUBLANES, LANES), F32)],
        scratch_shapes=[pltpu.VMEM((ML_HEADS, ML_DIM, 2 * LANES), F32)] * 2
        + [pltpu.VMEM((SUBLANES, LANES), F32)] * 2,
        compiler_params=_cparams(("parallel", "arbitrary")),
        name="mlstm",
    )(qm, km, vm, g_tm, g_fm, qm, km, vm, g_tm, g_fm, s0, m0)


def _top_values(x, n):
    vals = []
    cur = x
    for i in range(n):
        mx = jnp.max(cur, axis=0, keepdims=True)
        vals.append(mx)
        if i + 1 < n:
            cur = jnp.where(cur == mx, NEG_INF, cur)
    return vals


_CAND_PAIRS = [(a, b) for a in range(PEER_TOPK + 1) for b in range(PEER_TOPK + 1)
               if (a + 1) * (b + 1) <= PEER_TOPK + 1]
_CAND_ROWS = -(-len(_CAND_PAIRS) // SUBLANES) * SUBLANES


def _mix_kernel(tt, x_ref, att_ref, hf_ref, hb_ref, om_ref, gmh_ref, wout_ref, gffn_ref, wpqt_ref,
                k1_ref, k2_ref, h1_ref, xn2_ref, s1_ref, s2_ref, scal_ref, mix_ref, cand_ref):
    mix_ref[:, 0:ATT_WIDTH] = att_ref[0]
    for h in range(ML_HEADS):
        sl = slice(h * ML_DIM, (h + 1) * ML_DIM)
        hs = hf_ref[0, :, sl] + hb_ref[0, :, sl]
        ms = jnp.mean(hs * hs, axis=-1, keepdims=True)
        hn = hs * lax.rsqrt(ms + EPS) * gmh_ref[...]
        gate = 1.0 / (1.0 + jnp.exp(-om_ref[0, :, sl].astype(F32)))
        mix_ref[:, ATT_WIDTH + h * ML_DIM:ATT_WIDTH + (h + 1) * ML_DIM] = (hn * gate).astype(BF16)
    h1 = x_ref[0] + jnp.dot(mix_ref[...], wout_ref[...], preferred_element_type=F32)
    h1_ref[0] = h1
    ms = jnp.mean(h1 * h1, axis=-1, keepdims=True)
    xn2 = (h1 * lax.rsqrt(ms + EPS) * gffn_ref[...]).astype(BF16)
    xn2_ref[0] = xn2
    qt = lax.dot_general(wpqt_ref[...], xn2, (((1,), (1,)), ((), ())), preferred_element_type=F32)
    half = PEER_QDIM // 2
    cand_ref[...] = jnp.full(cand_ref.shape, NEG_INF, F32)
    for h in range(PEER_HEADS):
        q1 = qt[h * PEER_QDIM:h * PEER_QDIM + half].astype(BF16)
        q2 = qt[h * PEER_QDIM + half:(h + 1) * PEER_QDIM].astype(BF16)
        s1 = jnp.dot(k1_ref[h], q1, preferred_element_type=F32)
        s2 = jnp.dot(k2_ref[h], q2, preferred_element_type=F32)
        s1_ref[0, h] = s1
        s2_ref[0, h] = s2
        v1 = _top_values(s1, PEER_TOPK + 1)
        v2 = _top_values(s2, PEER_TOPK + 1)
        for idx, (a, b) in enumerate(_CAND_PAIRS):
            cand_ref[idx:idx + 1, :] = v1[a] + v2[b]
        cs = _top_values(cand_ref[...], PEER_TOPK + 1)
        z = jnp.zeros_like(cs[0])
        for i in range(PEER_TOPK):
            z = z + jnp.exp(cs[i] - cs[0])
        scal_ref[0, 0, h:h + 1, :] = 0.5 * (cs[PEER_TOPK - 1] + cs[PEER_TOPK])
        scal_ref[0, 1, h:h + 1, :] = v1[0]
        scal_ref[0, 2, h:h + 1, :] = v2[0]
        scal_ref[0, 3, h:h + 1, :] = 1.0 / z


def _mix(x, att, hf, hb, om, w):
    b, s, d = x.shape
    tt = MIX_TILE
    nt = s // tt
    tspec = lambda width: pl.BlockSpec((1, tt, width), lambda bi, ti: (bi, ti, 0))
    sspec = pl.BlockSpec((1, PEER_HEADS, PEER_NKEYS, tt), lambda bi, ti: (bi, 0, 0, ti))
    return pl.pallas_call(
        functools.partial(_mix_kernel, tt),
        grid=(b, nt),
        in_specs=[tspec(d), tspec(ATT_WIDTH), tspec(ML_WIDTH), tspec(ML_WIDTH), tspec(ML_WIDTH),
                  _const_spec((1, ML_DIM)), _const_spec((d, d)), _const_spec((1, d)),
                  _const_spec((PEER_HEADS * PEER_QDIM, d)),
                  _const_spec((PEER_HEADS, PEER_NKEYS, PEER_QDIM // 2)),
                  _const_spec((PEER_HEADS, PEER_NKEYS, PEER_QDIM // 2))],
        out_specs=[tspec(d), tspec(d), sspec, sspec,
                   pl.BlockSpec((1, 4, PEER_HEADS, tt), lambda bi, ti: (bi, 0, 0, ti))],
        out_shape=[jax.ShapeDtypeStruct((b, s, d), F32), jax.ShapeDtypeStruct((b, s, d), BF16),
                   jax.ShapeDtypeStruct((b, PEER_HEADS, PEER_NKEYS, s), F32),
                   jax.ShapeDtypeStruct((b, PEER_HEADS, PEER_NKEYS, s), F32),
                   jax.ShapeDtypeStruct((b, 4, PEER_HEADS, s), F32)],
        scratch_shapes=[pltpu.VMEM((tt, d), BF16), pltpu.VMEM((_CAND_ROWS, tt), F32)],
        compiler_params=_cparams(("parallel", "parallel")),
        name="mix_route",
    )(x, att, hf, hb, om, w["g_mh"], w["w_out"], w["g_ffn"], w["w_pqt"], w["sub_k1"], w["sub_k2"])


def _erf(x):
    ax = jnp.abs(x)
    t = 1.0 / (1.0 + 0.3275911 * ax)
    poly = t * (0.254829592 + t * (-0.284496736 + t * (1.421413741 + t * (-1.453152027 + t * 1.061405429))))
    e = 1.0 - poly * jnp.exp(-ax * ax)
    return jnp.where(x < 0, -e, e)


def _gelu(x):
    return 0.5 * x * (1.0 + lax.erf(x * (2.0 ** -0.5)))


def _peer_kernel(tt, xn2_ref, h1_ref, s1_ref, s2_ref, scal_ref, u_ref, vt_ref, y_ref,
                 e1_ref, thr_ref, e2_ref, w_ref, acc_ref):
    g = pl.program_id(2)
    rows = PEER_GROUP // PEER_NKEYS

    @pl.when(g == 0)
    def _():
        for h in range(PEER_HEADS):
            s1 = s1_ref[0, h]
            e1_ref[h] = jnp.exp(s1 - scal_ref[0, 1, h:h + 1, :]) * scal_ref[0, 3, h:h + 1, :]
            thr_ref[h] = scal_ref[0, 0, h:h + 1, :] - s1
            e2_ref[h] = jnp.exp(s2_ref[0, h] - scal_ref[0, 2, h:h + 1, :])
        acc_ref[...] = jnp.zeros(acc_ref.shape, F32)

    a = lax.dot_general(u_ref[...], xn2_ref[0], (((1,), (1,)), ((), ())), preferred_element_type=F32)
    for cc in range(rows):
        c = g * rows + cc
        act = _gelu(a[cc * PEER_NKEYS:(cc + 1) * PEER_NKEYS])
        wsum = jnp.zeros((PEER_NKEYS, tt), F32)
        for h in range(PEER_HEADS):
            sel = s2_ref[0, h] > thr_ref[h, pl.ds(c, 1), :]
            wsum = wsum + jnp.where(sel, e2_ref[h], 0.0) * e1_ref[h, pl.ds(c, 1), :]
        w_ref[cc * PEER_NKEYS:(cc + 1) * PEER_NKEYS, :] = (wsum * act).astype(BF16)
    acc_ref[...] += jnp.dot(vt_ref[...], w_ref[...], preferred_element_type=F32)

    @pl.when(g == pl.num_programs(2) - 1)
    def _():
        y_ref[0] = h1_ref[0] + acc_ref[...].T


def _peer(xn2, h1, s1t, s2t, scal, w):
    b, s, d = h1.shape
    tt = PEER_TILE
    ng = (PEER_NKEYS * PEER_NKEYS) // PEER_GROUP
    tspec = pl.BlockSpec((1, tt, d), lambda bi, ti, gi: (bi, ti, 0))
    sspec = pl.BlockSpec((1, PEER_HEADS, PEER_NKEYS, tt), lambda bi, ti, gi: (bi, 0, 0, ti))
    return pl.pallas_call(
        functools.partial(_peer_kernel, tt),
        grid=(b, s // tt, ng),
        in_specs=[tspec, tspec, sspec, sspec,
                  pl.BlockSpec((1, 4, PEER_HEADS, tt), lambda bi, ti, gi: (bi, 0, 0, ti)),
                  pl.BlockSpec((PEER_GROUP, d), lambda bi, ti, gi: (gi, 0)),
                  pl.BlockSpec((d, PEER_GROUP), lambda bi, ti, gi: (0, gi))],
        out_specs=tspec,
        out_shape=jax.ShapeDtypeStruct((b, s, d), F32),
        scratch_shapes=[pltpu.VMEM((PEER_HEADS, PEER_NKEYS, tt), F32)] * 3
        + [pltpu.VMEM((PEER_GROUP, tt), BF16), pltpu.VMEM((d, tt), F32)],
        compiler_params=_cparams(("parallel", "parallel", "arbitrary")),
        name="peer_dense",
    )(xn2, h1, s1t, s2t, scal, w["peer_u"], w["peer_vt"])


def _prep_weights(g_mix, w_in, b_gates, conv_w, conv_b, g_qn, g_kn, lam_q1, lam_k1, lam_q2, lam_k2,
                  g_sub, g_mh, w_out, g_ffn, w_pq, sub_k1, sub_k2, peer_u, peer_v):
    half = ROT_DIM // 2
    inv = ROPE_THETA ** (-jnp.arange(half, dtype=F32) * 2.0 / ROT_DIM)
    w_g = w_in[0][:, N_MAIN:]
    return {
        "g_mix": g_mix[0][None, :],
        "w_main": w_in[0][:, :N_MAIN].astype(BF16),
        "w_g": w_g.astype(BF16),
        "w_gt": w_g.T.astype(BF16),
        "b_g": b_gates[0][None, :],
        "b_gt": b_gates[0][:, None],
        "g_qk": jnp.concatenate([jnp.tile(g_qn[0], ATT_WIDTH // ATT_QK_DIM),
                                 jnp.tile(g_kn[0], ATT_WIDTH // ATT_QK_DIM)])[None, :],
        "inv_lane": jnp.tile(inv, LANES // half)[None, :],
        "conv_w": conv_w[0][:, 0, :],
        "conv_b": conv_b[0][None, :],
        "lam4": jnp.stack([lam_q1[0], lam_k1[0], lam_q2[0], lam_k2[0]]),
        "g_sub": g_sub[0][None, :],
        "g_mh": g_mh[0][None, :],
        "w_out": w_out[0].astype(BF16),
        "g_ffn": g_ffn[0][None, :],
        "w_pqt": w_pq[0].T.astype(BF16),
        "sub_k1": sub_k1[0].astype(BF16),
        "sub_k2": sub_k2[0].astype(BF16),
        "peer_u": peer_u[0].astype(BF16),
        "peer_vt": peer_v[0].T.astype(BF16),
    }


def _trunk(x, meta, w):
    b, s, d = x.shape
    tt = PROJ_TILE
    assert s % tt == 0 and s % ATT_TK == 0 and s % MIX_TILE == 0 and s % PEER_TILE == 0
    nt = s // tt
    xr = x.reshape(b, nt, tt, d)
    meta_tail = jnp.broadcast_to(meta[N_META - HALO:][None, None], (b, 1, HALO, d))
    prev = jnp.concatenate([meta_tail, xr[:, :-1, tt - HALO:, :]], axis=1)
    nxt = jnp.concatenate([xr[:, 1:, :HALO, :], jnp.zeros((b, 1, HALO, d), x.dtype)], axis=1)
    q, k, v, qm, km, vm, om, g_tm, g_fm = _proj(x, prev, nxt, N_META, tt, w)

    meta_b = jnp.broadcast_to(meta[None], (b, N_META, d))
    m_prev = jnp.zeros((b, 1, HALO, d), x.dtype)
    m_next = x[:, None, :HALO, :]
    _, k_m, v_m, qm_m, km_m, vm_m, _, gtm_m, gfm_m = _proj(meta_b, m_prev, m_next, 0, N_META, w)

    pad = CHUNK - N_META
    att = _attn(q, k, v, jnp.pad(k_m, ((0, 0), (0, pad), (0, 0))), jnp.pad(v_m, ((0, 0), (0, pad), (0, 0))),
                w["lam4"], w["g_sub"])

    front = lambda a: jnp.pad(a, ((0, 0), (pad, 0), (0, 0)))
    is_input_gate = (jnp.arange(N_GATES) // ML_HEADS) % 2 == 0
    pad_gate = jnp.where(is_input_gate, NEG_INF, 0.0).astype(F32)
    gtm_p = jnp.concatenate([jnp.broadcast_to(pad_gate[None, None, :], (b, pad, N_GATES)), gtm_m], axis=1)
    gfm_p = jnp.concatenate([jnp.broadcast_to(pad_gate[None, :, None], (b, N_GATES, pad)), gfm_m], axis=2)
    s_zero = jnp.zeros((b, ML_HEADS, ML_DIM, 2 * LANES), F32)
    m_zero = jnp.zeros((b, SUBLANES, LANES), F32)
    _, _, s0, m0 = _mlstm(front(qm_m), front(km_m), front(vm_m), gtm_p, gfm_p, s_zero, m_zero)
    hf, hb, _, _ = _mlstm(qm, km, vm, g_tm, g_fm, s0, m0)

    h1, xn2, s1t, s2t, scal = _mix(x, att, hf, hb, om, w)
    return _peer(xn2, h1, s1t, s2t, scal, w)


def kernel(x_prompt, x_sample, meta, g_mix, w_in, b_gates, conv_w, conv_b, g_qn, g_kn, lam_q1, lam_k1,
           lam_q2, lam_k2, g_sub, g_mh, w_out, g_ffn, w_pq, sub_k1, sub_k2, peer_u, peer_v):
    w = _prep_weights(g_mix, w_in, b_gates, conv_w, conv_b, g_qn, g_kn, lam_q1, lam_k1, lam_q2, lam_k2,
                      g_sub, g_mh, w_out, g_ffn, w_pq, sub_k1, sub_k2, peer_u, peer_v)
    return (_trunk(x_prompt, meta, w), _trunk(x_sample, meta, w))
```

```python
import functools
import math

import jax
import jax.numpy as jnp
from jax import lax
from jax.experimental import pallas as pl
from jax.experimental.pallas import tpu as pltpu

F32 = jnp.float32
BF16 = jnp.bfloat16

D_MODEL = 1024
N_META = 16
ATT_HEADS = 4
ATT_QK_DIM = 64
ATT_V_DIM = 128
ATT_WIDTH = ATT_HEADS * ATT_V_DIM
ML_HEADS = 4
ML_DIM = 128
ML_WIDTH = ML_HEADS * ML_DIM
N_GATES = 4 * ML_HEADS
N_MAIN = 3 * ATT_WIDTH + 4 * ML_WIDTH
ROT_DIM = ATT_QK_DIM // 4
ROPE_THETA = 500000.0
CHUNK = 128
PEER_HEADS = 8
PEER_NKEYS = 128
PEER_QDIM = 256
PEER_TOPK = 16
EPS = 1e-6
LAM_INIT = 0.8 - 0.6 * math.exp(-0.3 * 0)
Q_SCALE = ATT_QK_DIM ** -0.5 * math.log2(math.e)

LANES = 128
SUBLANES = 8
HALO = SUBLANES
VMEM_LIMIT = 56 * 1024 * 1024

PROJ_TILE = 512
ATT_TQ = 256
ATT_TK = 1024
MIX_TILE = 512
PEER_TILE = 512
PEER_GROUP = 1024
NEG_INF = float("-inf")


def _cparams(sem):
    return pltpu.CompilerParams(dimension_semantics=sem, vmem_limit_bytes=VMEM_LIMIT)


def _const_spec(shape):
    nd = len(shape)
    return pl.BlockSpec(shape, lambda *_: (0,) * nd)


def _log_sigmoid(x):
    return -(jnp.maximum(-x, 0.0) + jnp.log1p(jnp.exp(-jnp.abs(x))))


def _proj_kernel(pos_base, tt,
                 x_ref, prev_ref, next_ref, gmix_ref, wmain_ref, wg_ref, wgt_ref, bg_ref, bgt_ref,
                 gqk_ref, inv_ref, convw_ref, convb_ref,
                 q_ref, k_ref, v_ref, qm_ref, km_ref, vm_ref, om_ref, gtm_ref, gfm_ref,
                 xe_ref, zc_ref):
    i = pl.program_id(1)
    xe_ref[0:HALO, :] = prev_ref[0, 0]
    xe_ref[HALO:HALO + tt, :] = x_ref[0]
    xe_ref[HALO + tt:2 * HALO + tt, :] = next_ref[0, 0]
    xe = xe_ref[...]
    ms = jnp.mean(xe * xe, axis=-1, keepdims=True)
    xn_ext = (xe * lax.rsqrt(ms + EPS) * gmix_ref[...]).astype(BF16)
    xn = xn_ext[HALO:HALO + tt]

    def mm(a, lo, hi):
        return jnp.dot(a, wmain_ref[:, lo:hi], preferred_element_type=F32)

    zqk = mm(xn, 0, 2 * ATT_WIDTH)
    sq = (zqk * zqk).astype(BF16)
    r = lax.broadcasted_iota(jnp.int32, (2 * LANES, 2 * LANES), 0) // ATT_QK_DIM
    c = lax.broadcasted_iota(jnp.int32, (2 * LANES, 2 * LANES), 1) // ATT_QK_DIM
    seg_ones = (r == c).astype(BF16)
    lane = lax.broadcasted_iota(jnp.int32, (tt, LANES), 1)
    l64 = lane % ATT_QK_DIM
    row = lax.broadcasted_iota(jnp.int32, (tt, LANES), 0)
    pos = (pos_base + i * tt + row).astype(F32)
    ang = pos * inv_ref[...]
    cos_a = jnp.cos(ang)
    sin_a = jnp.sin(ang)
    half = ROT_DIM // 2
    cos_m = jnp.where(l64 < ROT_DIM, cos_a, 1.0)
    sin_lo = jnp.where(l64 < half, -sin_a, 0.0)
    sin_hi = jnp.where((l64 >= half) & (l64 < ROT_DIM), sin_a, 0.0)
    for cb in range(4):
        lo = cb * 2 * LANES
        ss = jnp.dot(sq[:, lo:lo + 2 * LANES], seg_ones, preferred_element_type=F32)
        rn = lax.rsqrt(ss * (1.0 / ATT_QK_DIM) + EPS)
        qn = zqk[:, lo:lo + 2 * LANES] * rn * gqk_ref[:, lo:lo + 2 * LANES]
        for sb in range(2):
            xs = qn[:, sb * LANES:(sb + 1) * LANES]
            rot = (xs * cos_m + pltpu.roll(xs, LANES - half, 1) * sin_lo
                   + pltpu.roll(xs, half, 1) * sin_hi)
            col = lo + sb * LANES
            if col < ATT_WIDTH:
                q_ref[0, :, col:col + LANES] = (rot * Q_SCALE).astype(BF16)
            else:
                k_ref[0, :, col - ATT_WIDTH:col - ATT_WIDTH + LANES] = rot.astype(BF16)

    va = mm(xn, 2 * ATT_WIDTH, 3 * ATT_WIDTH).astype(BF16)
    for h in range(ATT_HEADS):
        v_ref[0, :, 2 * h * LANES:(2 * h + 1) * LANES] = va[:, h * ATT_V_DIM:(h + 1) * ATT_V_DIM]
        v_ref[0, :, (2 * h + 1) * LANES:(2 * h + 2) * LANES] = jnp.ones((tt, LANES), BF16)

    base = 3 * ATT_WIDTH
    zc_ref[...] = mm(xn_ext, base, base + 2 * ML_WIDTH)
    y = (convw_ref[0:1, :] * zc_ref[pl.ds(HALO - 1, tt), :]
         + convw_ref[1:2, :] * zc_ref[pl.ds(HALO, tt), :]
         + convw_ref[2:3, :] * zc_ref[pl.ds(HALO + 1, tt), :]
         + convb_ref[...])
    act = y * (1.0 / (1.0 + jnp.exp(-y)))
    qm_ref[0] = act[:, :ML_WIDTH].astype(BF16)
    km_ref[0] = (act[:, ML_WIDTH:] * (ML_DIM ** -0.5)).astype(BF16)
    vm_ref[0] = mm(xn, base + 2 * ML_WIDTH, base + 3 * ML_WIDTH).astype(BF16)
    om_ref[0] = mm(xn, base + 3 * ML_WIDTH, base + 4 * ML_WIDTH).astype(BF16)

    g_tm = jnp.dot(xn, wg_ref[...], preferred_element_type=F32) + bg_ref[...]
    g_fm = lax.dot_general(wgt_ref[...], xn, (((1,), (1,)), ((), ())),
                           preferred_element_type=F32) + bgt_ref[...]
    col_tm = lax.broadcasted_iota(jnp.int32, (tt, N_GATES), 1)
    row_fm = lax.broadcasted_iota(jnp.int32, (N_GATES, tt), 0)
    gtm_ref[0] = jnp.where((col_tm // ML_HEADS) % 2 == 1, _log_sigmoid(g_tm), g_tm)
    gfm_ref[0] = jnp.where((row_fm // ML_HEADS) % 2 == 1, _log_sigmoid(g_fm), g_fm)


def _proj(x, prev, nxt, pos_base, tt, w):
    b, s, d = x.shape
    nt = s // tt
    tok = lambda width, dt: jax.ShapeDtypeStruct((b, s, width), dt)
    out_shape = ([tok(ATT_WIDTH, BF16)] * 2 + [tok(2 * ATT_WIDTH, BF16)] + [tok(ML_WIDTH, BF16)] * 4
                 + [tok(N_GATES, F32), jax.ShapeDtypeStruct((b, N_GATES, s), F32)])
    tspec = lambda width: pl.BlockSpec((1, tt, width), lambda bi, ti: (bi, ti, 0))
    hspec = pl.BlockSpec((1, 1, HALO, d), lambda bi, ti: (bi, ti, 0, 0))
    return pl.pallas_call(
        functools.partial(_proj_kernel, pos_base, tt),
        grid=(b, nt),
        in_specs=[tspec(d), hspec, hspec, _const_spec((1, d)), _const_spec((d, N_MAIN)),
                  _const_spec((d, N_GATES)), _const_spec((N_GATES, d)), _const_spec((1, N_GATES)),
                  _const_spec((N_GATES, 1)), _const_spec((1, 2 * ATT_WIDTH)), _const_spec((1, LANES)),
                  _const_spec((3, 2 * ML_WIDTH)), _const_spec((1, 2 * ML_WIDTH))],
        out_specs=[tspec(ATT_WIDTH)] * 2 + [tspec(2 * ATT_WIDTH)] + [tspec(ML_WIDTH)] * 4
        + [tspec(N_GATES), pl.BlockSpec((1, N_GATES, tt), lambda bi, ti: (bi, 0, ti))],
        out_shape=out_shape,
        scratch_shapes=[pltpu.VMEM((tt + 2 * HALO, d), F32), pltpu.VMEM((tt + 2 * HALO, 2 * ML_WIDTH), F32)],
        compiler_params=_cparams(("parallel", "parallel")),
        name=f"proj_t{tt}",
    )(x, prev, nxt, w["g_mix"], w["w_main"], w["w_g"], w["w_gt"], w["b_g"], w["b_gt"],
      w["g_qk"], w["inv_lane"], w["conv_w"], w["conv_b"])


def _attn_kernel(tq, tk, nk,
                 q_ref, k_ref, v_ref, kt_ref, vt_ref, lam_ref, gsub_ref, o_ref,
                 qq_ref, m_ref, acc_ref, sa_ref, sb_ref):
    q = q_ref[0]
    lane = lax.broadcasted_iota(jnp.int32, (tq, LANES), 1)
    zero = jnp.zeros_like(q)
    qq_ref[0:tq, :] = jnp.where(lane < ATT_QK_DIM, q, zero)
    qq_ref[tq:2 * tq, :] = jnp.where(lane >= ATT_QK_DIM, q, zero)
    m_ref[...] = jnp.full(m_ref.shape, NEG_INF, F32)
    acc_ref[...] = jnp.zeros(acc_ref.shape, F32)

    def scores(kb):
        return lax.dot_general(qq_ref[...], kb, (((1,), (1,)), ((), ())), preferred_element_type=F32)

    def update(s, vb):
        m_prev = m_ref[...]
        m_new = jnp.maximum(m_prev, jnp.max(s, axis=1, keepdims=True))
        alpha = jnp.exp2(m_prev - m_new)
        p = jnp.exp2(s - m_new[:, 0:1]).astype(BF16)
        pv = jnp.dot(p, vb, preferred_element_type=F32)
        acc_ref[:, 0:LANES] = alpha * acc_ref[:, 0:LANES] + pv[:, 0:LANES]
        acc_ref[:, LANES:2 * LANES] = alpha * acc_ref[:, LANES:2 * LANES] + pv[:, LANES:2 * LANES]
        m_ref[...] = m_new

    def kblock(j):
        return k_ref[0, pl.ds(pl.multiple_of(j * tk, tk), tk), :]

    def vblock(j):
        return v_ref[0, pl.ds(pl.multiple_of(j * tk, tk), tk), :]

    sa_ref[...] = scores(kblock(0))

    def body(i, carry):
        j = 2 * i
        sb_ref[...] = scores(kblock(j + 1))
        update(sa_ref[...], vblock(j))
        sa_ref[...] = scores(kblock(j + 2))
        update(sb_ref[...], vblock(j + 1))
        return carry

    lax.fori_loop(0, nk // 2 - 1, body, 0)
    sb_ref[...] = scores(kblock(nk - 1))
    update(sa_ref[...], vblock(nk - 2))
    col = lax.broadcasted_iota(jnp.int32, (2 * tq, LANES), 1)
    s_meta = jnp.where(col < N_META, scores(kt_ref[0]), NEG_INF)
    update(sb_ref[...], vblock(nk - 1))
    update(s_meta, vt_ref[0])

    o1 = acc_ref[0:tq, 0:LANES] / acc_ref[0:tq, LANES:2 * LANES]
    o2 = acc_ref[tq:2 * tq, 0:LANES] / acc_ref[tq:2 * tq, LANES:2 * LANES]
    lam = (jnp.exp(jnp.sum(lam_ref[0:1, :] * lam_ref[1:2, :], axis=1, keepdims=True))
           - jnp.exp(jnp.sum(lam_ref[2:3, :] * lam_ref[3:4, :], axis=1, keepdims=True)) + LAM_INIT)
    o = o1 - lam * o2
    ms = jnp.mean(o * o, axis=-1, keepdims=True)
    o_ref[0] = (o * lax.rsqrt(ms + EPS) * gsub_ref[...] * (1.0 - LAM_INIT)).astype(BF16)


def _attn(q, k, v, kt, vt, lam4, g_sub):
    b, s, _ = q.shape
    tq, tk = ATT_TQ, ATT_TK
    kspec = lambda rows: pl.BlockSpec((1, rows, LANES), lambda bi, hi, qi: (bi, 0, hi))
    vspec = lambda rows: pl.BlockSpec((1, rows, 2 * LANES), lambda bi, hi, qi: (bi, 0, hi))
    return pl.pallas_call(
        functools.partial(_attn_kernel, tq, tk, s // tk),
        grid=(b, ATT_HEADS, s // tq),
        in_specs=[pl.BlockSpec((1, tq, LANES), lambda bi, hi, qi: (bi, qi, hi)),
                  kspec(s), vspec(s), kspec(LANES), vspec(LANES),
                  _const_spec((4, ATT_QK_DIM)), _const_spec((1, ATT_V_DIM))],
        out_specs=pl.BlockSpec((1, tq, LANES), lambda bi, hi, qi: (bi, qi, hi)),
        out_shape=jax.ShapeDtypeStruct((b, s, ATT_WIDTH), BF16),
        scratch_shapes=[pltpu.VMEM((2 * tq, LANES), BF16), pltpu.VMEM((2 * tq, LANES), F32),
                        pltpu.VMEM((2 * tq, 2 * LANES), F32),
                        pltpu.VMEM((2 * tq, tk), F32), pltpu.VMEM((2 * tq, tk), F32)],
        compiler_params=_cparams(("parallel", "parallel", "parallel")),
        name="diff_attn",
    )(q, k, v, kt, vt, lam4, g_sub)


def _mlstm_chunk(reverse, q, k, v, g_tm, g_fm, s_ref, m_ref, h_ref):
    r = lax.broadcasted_iota(jnp.int32, (CHUNK, CHUNK), 0)
    c = lax.broadcasted_iota(jnp.int32, (CHUNK, CHUNK), 1)
    if reverse:
        causal = c >= r
        last = 0
        gbase = 2 * ML_HEADS
    else:
        causal = c <= r
        last = CHUNK - 1
        gbase = 0
    b_cols = jnp.dot(causal.astype(F32), g_tm, preferred_element_type=F32,
                     precision=lax.Precision.HIGHEST)
    b_rows = jnp.dot(g_fm, (r <= c if not reverse else r >= c).astype(F32),
                     preferred_element_type=F32, precision=lax.Precision.HIGHEST)
    ones_col = (c == 0).astype(BF16)
    for h in range(ML_HEADS):
        ic, fc = gbase + h, gbase + ML_HEADS + h
        b_col = b_cols[:, fc:fc + 1]
        b_row = b_rows[fc:fc + 1, :]
        li_row = g_fm[ic:ic + 1, :]
        li_col = g_tm[:, ic:ic + 1]
        m_prev = m_ref[h:h + 1, 0:1]
        log_d = jnp.where(causal, b_col - b_row + li_row, NEG_INF)
        m_inter = b_col + m_prev
        mj = jnp.maximum(m_inter, jnp.max(log_d, axis=1, keepdims=True))
        qh = q[:, h * ML_DIM:(h + 1) * ML_DIM]
        kh = k[:, h * ML_DIM:(h + 1) * ML_DIM]
        vext = jnp.concatenate([v[:, h * ML_DIM:(h + 1) * ML_DIM], ones_col], axis=1)
        qk = lax.dot_general(qh, kh, (((1,), (1,)), ((), ())), preferred_element_type=F32)
        w = (qk * jnp.exp(log_d - mj)).astype(BF16)
        state = s_ref[h]
        tot = (jnp.exp(m_inter - mj) * jnp.dot(qh, state.astype(BF16), preferred_element_type=F32)
               + jnp.dot(w, vext, preferred_element_type=F32))
        den = tot[:, ML_DIM:ML_DIM + 1]
        h_ref[0, :, h * ML_DIM:(h + 1) * ML_DIM] = (
            tot[:, :ML_DIM] / jnp.maximum(jnp.abs(den), jnp.exp(-mj)))
        b_last = b_col[last:last + 1, :]
        logw_col = b_last - b_col + li_col
        logw_row = b_last - b_row + li_row
        m_new = jnp.maximum(b_last + m_prev, jnp.max(logw_row, axis=1, keepdims=True))
        kw = (kh.astype(F32) * jnp.exp(logw_col - m_new)).astype(BF16)
        s_ref[h] = (jnp.exp(b_last + m_prev - m_new) * state
                    + lax.dot_general(kw, vext, (((0,), (0,)), ((), ())), preferred_element_type=F32))
        m_ref[h:h + 1, :] = jnp.broadcast_to(m_new, (1, LANES))


def _mlstm_kernel(qf_ref, kf_ref, vf_ref, gtf_ref, gff_ref, qb_ref, kb_ref, vb_ref, gtb_ref, gfb_ref,
                  s0_ref, m0_ref, hf_ref, hb_ref, sfin_ref, mfin_ref,
                  sf_ref, sb_ref, mf_ref, mb_ref):
    j = pl.program_id(1)

    @pl.when(j == 0)
    def _():
        sf_ref[...] = s0_ref[0]
        mf_ref[...] = m0_ref[0]
        sb_ref[...] = jnp.zeros(sb_ref.shape, F32)
        mb_ref[...] = jnp.zeros(mb_ref.shape, F32)

    _mlstm_chunk(False, qf_ref[0], kf_ref[0], vf_ref[0], gtf_ref[0], gff_ref[0], sf_ref, mf_ref, hf_ref)
    _mlstm_chunk(True, qb_ref[0], kb_ref[0], vb_ref[0], gtb_ref[0], gfb_ref[0], sb_ref, mb_ref, hb_ref)

    @pl.when(j == pl.num_programs(1) - 1)
    def _():
        sfin_ref[0] = sf_ref[...]
        mfin_ref[0] = mf_ref[...]


def _mlstm(qm, km, vm, g_tm, g_fm, s0, m0):
    b, s, _ = qm.shape
    nc = s // CHUNK
    fwd = lambda width: pl.BlockSpec((1, CHUNK, width), lambda bi, ci: (bi, ci, 0))
    bwd = lambda width: pl.BlockSpec((1, CHUNK, width), lambda bi, ci: (bi, nc - 1 - ci, 0))
    gf_fwd = pl.BlockSpec((1, N_GATES, CHUNK), lambda bi, ci: (bi, 0, ci))
    gf_bwd = pl.BlockSpec((1, N_GATES, CHUNK), lambda bi, ci: (bi, 0, nc - 1 - ci))
    st = pl.BlockSpec((1, ML_HEADS, ML_DIM, 2 * LANES), lambda bi, ci: (bi, 0, 0, 0))
    mx = pl.BlockSpec((1, SUBLANES, LANES), lambda bi, ci: (bi, 0, 0))
    return pl.pallas_call(
        _mlstm_kernel,
        grid=(b, nc),
        in_specs=[fwd(ML_WIDTH)] * 3 + [fwd(N_GATES), gf_fwd] + [bwd(ML_WIDTH)] * 3 + [bwd(N_GATES), gf_bwd]
        + [st, mx],
        out_specs=[fwd(ML_WIDTH), bwd(ML_WIDTH), st, mx],
        out_shape=[jax.ShapeDtypeStruct((b, s, ML_WIDTH), F32)] * 2
        + [jax.ShapeDtypeStruct((b, ML_HEADS, ML_DIM, 2 * LANES), F32),
           jax.ShapeDtypeStruct((b, SUBLANES, LANES), F32)],
        scratch_shapes=[pltpu.VMEM((ML_HEADS, ML_DIM, 2 * LANES), F32)] * 2
        + [pltpu.VMEM((SUBLANES, LANES), F32)] * 2,
        compiler_params=_cparams(("parallel", "arbitrary")),
        name="mlstm",
    )(qm, km, vm, g_tm, g_fm, qm, km, vm, g_tm, g_fm, s0, m0)


def _top_values(x, n, with_rank=False):
    vals = []
    cur = x
    rank = jnp.full(x.shape, float(n), F32) if with_rank else None
    for i in range(n):
        mx = jnp.max(cur, axis=0, keepdims=True)
        vals.append(mx)
        hit = cur == mx
        if with_rank:
            rank = jnp.where(hit, float(i), rank)
        if i + 1 < n:
            cur = jnp.where(hit, NEG_INF, cur)
    return (vals, rank) if with_rank else vals


_CAND_PAIRS = [(a, b) for a in range(PEER_TOPK + 1) for b in range(PEER_TOPK + 1)
               if (a + 1) * (b + 1) <= PEER_TOPK + 1]
_CAND_ROWS = -(-len(_CAND_PAIRS) // SUBLANES) * SUBLANES


def _mix_kernel(tt, x_ref, att_ref, hf_ref, hb_ref, om_ref, gmh_ref, wout_ref, gffn_ref, wpqt_ref,
                k1_ref, k2_ref, h1_ref, xn2_ref, r2_ref, e2_ref, n1_ref, e1_ref, mix_ref, cand_ref):
    mix_ref[:, 0:ATT_WIDTH] = att_ref[0]
    for h in range(ML_HEADS):
        sl = slice(h * ML_DIM, (h + 1) * ML_DIM)
        hs = hf_ref[0, :, sl] + hb_ref[0, :, sl]
        ms = jnp.mean(hs * hs, axis=-1, keepdims=True)
        hn = hs * lax.rsqrt(ms + EPS) * gmh_ref[...]
        gate = 1.0 / (1.0 + jnp.exp(-om_ref[0, :, sl].astype(F32)))
        mix_ref[:, ATT_WIDTH + h * ML_DIM:ATT_WIDTH + (h + 1) * ML_DIM] = (hn * gate).astype(BF16)
    h1 = x_ref[0] + jnp.dot(mix_ref[...], wout_ref[...], preferred_element_type=F32)
    h1_ref[0] = h1
    ms = jnp.mean(h1 * h1, axis=-1, keepdims=True)
    xn2 = (h1 * lax.rsqrt(ms + EPS) * gffn_ref[...]).astype(BF16)
    xn2_ref[0] = xn2
    qt = lax.dot_general(wpqt_ref[...], xn2, (((1,), (1,)), ((), ())), preferred_element_type=F32)
    half = PEER_QDIM // 2
    cand_ref[...] = jnp.full(cand_ref.shape, NEG_INF, F32)
    for h in range(PEER_HEADS):
        q1 = qt[h * PEER_QDIM:h * PEER_QDIM + half].astype(BF16)
        q2 = qt[h * PEER_QDIM + half:(h + 1) * PEER_QDIM].astype(BF16)
        s1 = jnp.dot(k1_ref[h], q1, preferred_element_type=F32)
        s2 = jnp.dot(k2_ref[h], q2, preferred_element_type=F32)
        v1 = _top_values(s1, PEER_TOPK + 1)
        v2, rank2 = _top_values(s2, PEER_TOPK + 1, with_rank=True)
        for idx, (a, b) in enumerate(_CAND_PAIRS):
            cand_ref[idx:idx + 1, :] = v1[a] + v2[b]
        cs = _top_values(cand_ref[...], PEER_TOPK + 1)
        z = jnp.zeros_like(cs[0])
        for i in range(PEER_TOPK):
            z = z + jnp.exp(cs[i] - cs[0])
        thr1 = 0.5 * (cs[PEER_TOPK - 1] + cs[PEER_TOPK]) - s1
        n1 = jnp.zeros_like(s1)
        for b in range(PEER_TOPK):
            n1 = n1 + jnp.where(v2[b] > thr1, 1.0, 0.0)
        r2_ref[0, h] = rank2.astype(BF16)
        n1_ref[0, h] = n1
        e1_ref[0, h] = jnp.exp(s1 - v1[0]) * (1.0 / z)
        e2_ref[0, h] = jnp.exp(s2 - v2[0]).astype(BF16)


def _mix(x, att, hf, hb, om, w):
    b, s, d = x.shape
    tt = MIX_TILE
    nt = s // tt
    tspec = lambda width: pl.BlockSpec((1, tt, width), lambda bi, ti: (bi, ti, 0))
    sspec = pl.BlockSpec((1, PEER_HEADS, PEER_NKEYS, tt), lambda bi, ti: (bi, 0, 0, ti))
    return pl.pallas_call(
        functools.partial(_mix_kernel, tt),
        grid=(b, nt),
        in_specs=[tspec(d), tspec(ATT_WIDTH), tspec(ML_WIDTH), tspec(ML_WIDTH), tspec(ML_WIDTH),
                  _const_spec((1, ML_DIM)), _const_spec((d, d)), _const_spec((1, d)),
                  _const_spec((PEER_HEADS * PEER_QDIM, d)),
                  _const_spec((PEER_HEADS, PEER_NKEYS, PEER_QDIM // 2)),
                  _const_spec((PEER_HEADS, PEER_NKEYS, PEER_QDIM // 2))],
        out_specs=[tspec(d), tspec(d), sspec, sspec, sspec, sspec],
        out_shape=[jax.ShapeDtypeStruct((b, s, d), F32), jax.ShapeDtypeStruct((b, s, d), BF16),
                   jax.ShapeDtypeStruct((b, PEER_HEADS, PEER_NKEYS, s), BF16),
                   jax.ShapeDtypeStruct((b, PEER_HEADS, PEER_NKEYS, s), BF16),
                   jax.ShapeDtypeStruct((b, PEER_HEADS, PEER_NKEYS, s), F32),
                   jax.ShapeDtypeStruct((b, PEER_HEADS, PEER_NKEYS, s), F32)],
        scratch_shapes=[pltpu.VMEM((tt, d), BF16), pltpu.VMEM((_CAND_ROWS, tt), F32)],
        compiler_params=_cparams(("parallel", "parallel")),
        name="mix_route",
    )(x, att, hf, hb, om, w["g_mh"], w["w_out"], w["g_ffn"], w["w_pqt"], w["sub_k1"], w["sub_k2"])


def _gelu(x):
    return 0.5 * x * (1.0 + lax.erf(x * (2.0 ** -0.5)))


def _peer_kernel(tt, ng, xn2_ref, h1_ref, r2_ref, e2_ref, n1_ref, e1_ref, u_ref, vt_ref, y_ref,
                 a_ref, w_ref, acc_ref):
    s = pl.program_id(1)
    rows = PEER_GROUP // PEER_NKEYS

    @pl.when(s == 0)
    def _():
        a_ref[...] = jnp.zeros(a_ref.shape, F32)
        w_ref[...] = jnp.zeros(w_ref.shape, BF16)

    @pl.when((s < 2) | ((s + ng - 2) % ng == 0))
    def _():
        acc_ref[...] = jnp.zeros(acc_ref.shape, F32)

    g2 = (s + ng - 1) % ng

    def stages(cur, oth):
        acc_ref[...] += jnp.dot(vt_ref[...], w_ref[cur], preferred_element_type=F32)
        a_ref[cur] = lax.dot_general(u_ref[...], xn2_ref[0], (((1,), (1,)), ((), ())),
                                     preferred_element_type=F32)
        zero = jnp.zeros((PEER_NKEYS, tt), BF16)
        for cc in range(rows):
            c = g2 * rows + cc
            act = _gelu(a_ref[oth, cc * PEER_NKEYS:(cc + 1) * PEER_NKEYS, :])
            wsum = zero
            for h in range(PEER_HEADS):
                sel = r2_ref[0, h] < n1_ref[0, h, pl.ds(c, 1), :].astype(BF16)
                wsum = wsum + jnp.where(sel, e2_ref[0, h], zero) * e1_ref[0, h, pl.ds(c, 1), :].astype(BF16)
            w_ref[oth, cc * PEER_NKEYS:(cc + 1) * PEER_NKEYS, :] = wsum * act.astype(BF16)

    @pl.when(s % 2 == 0)
    def _():
        stages(0, 1)

    @pl.when(s % 2 == 1)
    def _():
        stages(1, 0)

    @pl.when((s >= 2) & ((s + ng - 2) % ng == ng - 1))
    def _():
        y_ref[0] = h1_ref[0] + acc_ref[...].T


def _peer(xn2, h1, r2, e2, n1, e1, w):
    b, s, d = h1.shape
    tt = PEER_TILE
    nt = s // tt
    ng = (PEER_NKEYS * PEER_NKEYS) // PEER_GROUP
    tile = lambda si, lag: jnp.clip((si - lag) // ng, 0, nt - 1)
    tspec = lambda lag: pl.BlockSpec((1, tt, d), lambda bi, si: (bi, tile(si, lag), 0))
    sspec = pl.BlockSpec((1, PEER_HEADS, PEER_NKEYS, tt), lambda bi, si: (bi, 0, 0, tile(si, 1)))
    return pl.pallas_call(
        functools.partial(_peer_kernel, tt, ng),
        grid=(b, nt * ng + 2),
        in_specs=[tspec(0), tspec(2), sspec, sspec, sspec, sspec,
                  pl.BlockSpec((PEER_GROUP, d), lambda bi, si: (si % ng, 0)),
                  pl.BlockSpec((d, PEER_GROUP), lambda bi, si: (0, (si + ng - 2) % ng))],
        out_specs=tspec(2),
        out_shape=jax.ShapeDtypeStruct((b, s, d), F32),
        scratch_shapes=[pltpu.VMEM((2, PEER_GROUP, tt), F32), pltpu.VMEM((2, PEER_GROUP, tt), BF16),
                        pltpu.VMEM((d, tt), F32)],
        compiler_params=_cparams(("parallel", "arbitrary")),
        name="peer_dense",
    )(xn2, h1, r2, e2, n1, e1, w["peer_u"], w["peer_vt"])


def _prep_weights(g_mix, w_in, b_gates, conv_w, conv_b, g_qn, g_kn, lam_q1, lam_k1, lam_q2, lam_k2,
                  g_sub, g_mh, w_out, g_ffn, w_pq, sub_k1, sub_k2, peer_u, peer_v):
    half = ROT_DIM // 2
    inv = ROPE_THETA ** (-jnp.arange(half, dtype=F32) * 2.0 / ROT_DIM)
    w_g = w_in[0][:, N_MAIN:]
    return {
        "g_mix": g_mix[0][None, :],
        "w_main": w_in[0][:, :N_MAIN].astype(BF16),
        "w_g": w_g.astype(BF16),
        "w_gt": w_g.T.astype(BF16),
        "b_g": b_gates[0][None, :],
        "b_gt": b_gates[0][:, None],
        "g_qk": jnp.concatenate([jnp.tile(g_qn[0], ATT_WIDTH // ATT_QK_DIM),
                                 jnp.tile(g_kn[0], ATT_WIDTH // ATT_QK_DIM)])[None, :],
        "inv_lane": jnp.tile(inv, LANES // half)[None, :],
        "conv_w": conv_w[0][:, 0, :],
        "conv_b": conv_b[0][None, :],
        "lam4": jnp.stack([lam_q1[0], lam_k1[0], lam_q2[0], lam_k2[0]]),
        "g_sub": g_sub[0][None, :],
        "g_mh": g_mh[0][None, :],
        "w_out": w_out[0].astype(BF16),
        "g_ffn": g_ffn[0][None, :],
        "w_pqt": w_pq[0].T.astype(BF16),
        "sub_k1": sub_k1[0].astype(BF16),
        "sub_k2": sub_k2[0].astype(BF16),
        "peer_u": peer_u[0].astype(BF16),
        "peer_vt": peer_v[0].T.astype(BF16),
    }


def _trunk(x, meta, w):
    b, s, d = x.shape
    tt = PROJ_TILE
    assert s % tt == 0 and s % (2 * ATT_TK) == 0 and s % MIX_TILE == 0 and s % PEER_TILE == 0
    nt = s // tt
    xr = x.reshape(b, nt, tt, d)
    meta_tail = jnp.broadcast_to(meta[N_META - HALO:][None, None], (b, 1, HALO, d))
    prev = jnp.concatenate([meta_tail, xr[:, :-1, tt - HALO:, :]], axis=1)
    nxt = jnp.concatenate([xr[:, 1:, :HALO, :], jnp.zeros((b, 1, HALO, d), x.dtype)], axis=1)
    q, k, v, qm, km, vm, om, g_tm, g_fm = _proj(x, prev, nxt, N_META, tt, w)

    meta_b = jnp.broadcast_to(meta[None], (b, N_META, d))
    m_prev = jnp.zeros((b, 1, HALO, d), x.dtype)
    m_next = x[:, None, :HALO, :]
    _, k_m, v_m, qm_m, km_m, vm_m, _, gtm_m, gfm_m = _proj(meta_b, m_prev, m_next, 0, N_META, w)

    pad = CHUNK - N_META
    att = _attn(q, k, v, jnp.pad(k_m, ((0, 0), (0, pad), (0, 0))), jnp.pad(v_m, ((0, 0), (0, pad), (0, 0))),
                w["lam4"], w["g_sub"])

    front = lambda a: jnp.pad(a, ((0, 0), (pad, 0), (0, 0)))
    is_input_gate = (jnp.arange(N_GATES) // ML_HEADS) % 2 == 0
    pad_gate = jnp.where(is_input_gate, NEG_INF, 0.0).astype(F32)
    gtm_p = jnp.concatenate([jnp.broadcast_to(pad_gate[None, None, :], (b, pad, N_GATES)), gtm_m], axis=1)
    gfm_p = jnp.concatenate([jnp.broadcast_to(pad_gate[None, :, None], (b, N_GATES, pad)), gfm_m], axis=2)
    s_zero = jnp.zeros((b, ML_HEADS, ML_DIM, 2 * LANES), F32)
    m_zero = jnp.zeros((b, SUBLANES, LANES), F32)
    _, _, s0, m0 = _mlstm(front(qm_m), front(km_m), front(vm_m), gtm_p, gfm_p, s_zero, m_zero)
    hf, hb, _, _ = _mlstm(qm, km, vm, g_tm, g_fm, s0, m0)

    h1, xn2, r2, e2, n1, e1 = _mix(x, att, hf, hb, om, w)
    return _peer(xn2, h1, r2, e2, n1, e1, w)


def kernel(x_prompt, x_sample, meta, g_mix, w_in, b_gates, conv_w, conv_b, g_qn, g_kn, lam_q1, lam_k1,
           lam_q2, lam_k2, g_sub, g_mh, w_out, g_ffn, w_pq, sub_k1, sub_k2, peer_u, peer_v):
    w = _prep_weights(g_mix, w_in, b_gates, conv_w, conv_b, g_qn, g_kn, lam_q1, lam_k1, lam_q2, lam_k2,
                      g_sub, g_mh, w_out, g_ffn, w_pq, sub_k1, sub_k2, peer_u, peer_v)
    return (_trunk(x_prompt, meta, w), _trunk(x_sample, meta, w))
```

```python
import functools
import math

import jax
import jax.numpy as jnp
from jax import lax
from jax.experimental import pallas as pl
from jax.experimental.pallas import tpu as pltpu

F32 = jnp.float32
BF16 = jnp.bfloat16

D_MODEL = 1024
N_META = 16
ATT_HEADS = 4
ATT_QK_DIM = 64
ATT_V_DIM = 128
ATT_WIDTH = ATT_HEADS * ATT_V_DIM
ML_HEADS = 4
ML_DIM = 128
ML_WIDTH = ML_HEADS * ML_DIM
N_GATES = 4 * ML_HEADS
N_MAIN = 3 * ATT_WIDTH + 4 * ML_WIDTH
ROT_DIM = ATT_QK_DIM // 4
ROPE_THETA = 500000.0
CHUNK = 128
PEER_HEADS = 8
PEER_NKEYS = 128
PEER_QDIM = 256
PEER_TOPK = 16
EPS = 1e-6
LAM_INIT = 0.8 - 0.6 * math.exp(-0.3 * 0)
Q_SCALE = ATT_QK_DIM ** -0.5 * math.log2(math.e)

LANES = 128
SUBLANES = 8
HALO = SUBLANES
VMEM_LIMIT = 56 * 1024 * 1024

PROJ_TILE = 512
ATT_TQ = 256
ATT_TK = 1024
MIX_TILE = 512
PEER_TILE = 512
PEER_GROUP = 1024
PEER_PASS_ROWS = 64
PEER_VBLOCK = 256
NEG_INF = float("-inf")


def _cparams(sem):
    return pltpu.CompilerParams(dimension_semantics=sem, vmem_limit_bytes=VMEM_LIMIT)


def _const_spec(shape):
    nd = len(shape)
    return pl.BlockSpec(shape, lambda *_: (0,) * nd)


def _log_sigmoid(x):
    return -(jnp.maximum(-x, 0.0) + jnp.log1p(jnp.exp(-jnp.abs(x))))


def _proj_kernel(pos_base, tt,
                 x_ref, prev_ref, next_ref, gmix_ref, wmain_ref, wg_ref, wgt_ref, bg_ref, bgt_ref,
                 gqk_ref, inv_ref, convw_ref, convb_ref,
                 q_ref, k_ref, v_ref, qm_ref, km_ref, vm_ref, om_ref, gtm_ref, gfm_ref,
                 xe_ref, zc_ref):
    i = pl.program_id(1)
    xe_ref[0:HALO, :] = prev_ref[0, 0]
    xe_ref[HALO:HALO + tt, :] = x_ref[0]
    xe_ref[HALO + tt:2 * HALO + tt, :] = next_ref[0, 0]
    xe = xe_ref[...]
    ms = jnp.mean(xe * xe, axis=-1, keepdims=True)
    xn_ext = (xe * lax.rsqrt(ms + EPS) * gmix_ref[...]).astype(BF16)
    xn = xn_ext[HALO:HALO + tt]

    def mm(a, lo, hi):
        return jnp.dot(a, wmain_ref[:, lo:hi], preferred_element_type=F32)

    zqk = mm(xn, 0, 2 * ATT_WIDTH)
    sq = (zqk * zqk).astype(BF16)
    r = lax.broadcasted_iota(jnp.int32, (2 * LANES, 2 * LANES), 0) // ATT_QK_DIM
    c = lax.broadcasted_iota(jnp.int32, (2 * LANES, 2 * LANES), 1) // ATT_QK_DIM
    seg_ones = (r == c).astype(BF16)
    lane = lax.broadcasted_iota(jnp.int32, (tt, LANES), 1)
    l64 = lane % ATT_QK_DIM
    row = lax.broadcasted_iota(jnp.int32, (tt, LANES), 0)
    pos = (pos_base + i * tt + row).astype(F32)
    ang = pos * inv_ref[...]
    cos_a = jnp.cos(ang)
    sin_a = jnp.sin(ang)
    half = ROT_DIM // 2
    cos_m = jnp.where(l64 < ROT_DIM, cos_a, 1.0)
    sin_lo = jnp.where(l64 < half, -sin_a, 0.0)
    sin_hi = jnp.where((l64 >= half) & (l64 < ROT_DIM), sin_a, 0.0)
    for cb in range(4):
        lo = cb * 2 * LANES
        ss = jnp.dot(sq[:, lo:lo + 2 * LANES], seg_ones, preferred_element_type=F32)
        rn = lax.rsqrt(ss * (1.0 / ATT_QK_DIM) + EPS)
        qn = zqk[:, lo:lo + 2 * LANES] * rn * gqk_ref[:, lo:lo + 2 * LANES]
        for sb in range(2):
            xs = qn[:, sb * LANES:(sb + 1) * LANES]
            rot = (xs * cos_m + pltpu.roll(xs, LANES - half, 1) * sin_lo
                   + pltpu.roll(xs, half, 1) * sin_hi)
            col = lo + sb * LANES
            if col < ATT_WIDTH:
                q_ref[0, :, col:col + LANES] = (rot * Q_SCALE).astype(BF16)
            else:
                k_ref[0, :, col - ATT_WIDTH:col - ATT_WIDTH + LANES] = rot.astype(BF16)

    va = mm(xn, 2 * ATT_WIDTH, 3 * ATT_WIDTH).astype(BF16)
    for h in range(ATT_HEADS):
        v_ref[0, :, 2 * h * LANES:(2 * h + 1) * LANES] = va[:, h * ATT_V_DIM:(h + 1) * ATT_V_DIM]
        v_ref[0, :, (2 * h + 1) * LANES:(2 * h + 2) * LANES] = jnp.ones((tt, LANES), BF16)

    base = 3 * ATT_WIDTH
    zc_ref[...] = mm(xn_ext, base, base + 2 * ML_WIDTH)
    y = (convw_ref[0:1, :] * zc_ref[pl.ds(HALO - 1, tt), :]
         + convw_ref[1:2, :] * zc_ref[pl.ds(HALO, tt), :]
         + convw_ref[2:3, :] * zc_ref[pl.ds(HALO + 1, tt), :]
         + convb_ref[...])
    act = y * (1.0 / (1.0 + jnp.exp(-y)))
    qm_ref[0] = act[:, :ML_WIDTH].astype(BF16)
    km_ref[0] = (act[:, ML_WIDTH:] * (ML_DIM ** -0.5)).astype(BF16)
    vm_ref[0] = mm(xn, base + 2 * ML_WIDTH, base + 3 * ML_WIDTH).astype(BF16)
    om_ref[0] = mm(xn, base + 3 * ML_WIDTH, base + 4 * ML_WIDTH).astype(BF16)

    g_tm = jnp.dot(xn, wg_ref[...], preferred_element_type=F32) + bg_ref[...]
    g_fm = lax.dot_general(wgt_ref[...], xn, (((1,), (1,)), ((), ())),
                           preferred_element_type=F32) + bgt_ref[...]
    col_tm = lax.broadcasted_iota(jnp.int32, (tt, N_GATES), 1)
    row_fm = lax.broadcasted_iota(jnp.int32, (N_GATES, tt), 0)
    gtm_ref[0] = jnp.where((col_tm // ML_HEADS) % 2 == 1, _log_sigmoid(g_tm), g_tm)
    gfm_ref[0] = jnp.where((row_fm // ML_HEADS) % 2 == 1, _log_sigmoid(g_fm), g_fm)


def _proj(x, prev, nxt, pos_base, tt, w):
    b, s, d = x.shape
    nt = s // tt
    tok = lambda width, dt: jax.ShapeDtypeStruct((b, s, width), dt)
    out_shape = ([tok(ATT_WIDTH, BF16)] * 2 + [tok(2 * ATT_WIDTH, BF16)] + [tok(ML_WIDTH, BF16)] * 4
                 + [tok(N_GATES, F32), jax.ShapeDtypeStruct((b, N_GATES, s), F32)])
    tspec = lambda width: pl.BlockSpec((1, tt, width), lambda bi, ti: (bi, ti, 0))
    hspec = pl.BlockSpec((1, 1, HALO, d), lambda bi, ti: (bi, ti, 0, 0))
    return pl.pallas_call(
        functools.partial(_proj_kernel, pos_base, tt),
        grid=(b, nt),
        in_specs=[tspec(d), hspec, hspec, _const_spec((1, d)), _const_spec((d, N_MAIN)),
                  _const_spec((d, N_GATES)), _const_spec((N_GATES, d)), _const_spec((1, N_GATES)),
                  _const_spec((N_GATES, 1)), _const_spec((1, 2 * ATT_WIDTH)), _const_spec((1, LANES)),
                  _const_spec((3, 2 * ML_WIDTH)), _const_spec((1, 2 * ML_WIDTH))],
        out_specs=[tspec(ATT_WIDTH)] * 2 + [tspec(2 * ATT_WIDTH)] + [tspec(ML_WIDTH)] * 4
        + [tspec(N_GATES), pl.BlockSpec((1, N_GATES, tt), lambda bi, ti: (bi, 0, ti))],
        out_shape=out_shape,
        scratch_shapes=[pltpu.VMEM((tt + 2 * HALO, d), F32), pltpu.VMEM((tt + 2 * HALO, 2 * ML_WIDTH), F32)],
        compiler_params=_cparams(("parallel", "parallel")),
        name=f"proj_t{tt}",
    )(x, prev, nxt, w["g_mix"], w["w_main"], w["w_g"], w["w_gt"], w["b_g"], w["b_gt"],
      w["g_qk"], w["inv_lane"], w["conv_w"], w["conv_b"])


def _attn_kernel(tq, tk, nk,
                 q_ref, k_ref, v_ref, kt_ref, vt_ref, lam_ref, gsub_ref, o_ref,
                 qq_ref, m_ref, acc_ref, sa_ref, sb_ref):
    q = q_ref[0]
    lane = lax.broadcasted_iota(jnp.int32, (tq, LANES), 1)
    zero = jnp.zeros_like(q)
    qq_ref[0:tq, :] = jnp.where(lane < ATT_QK_DIM, q, zero)
    qq_ref[tq:2 * tq, :] = jnp.where(lane >= ATT_QK_DIM, q, zero)
    m_ref[...] = jnp.full(m_ref.shape, NEG_INF, F32)
    acc_ref[...] = jnp.zeros(acc_ref.shape, F32)

    def scores(kb):
        return lax.dot_general(qq_ref[...], kb, (((1,), (1,)), ((), ())), preferred_element_type=F32)

    def update(s, vb):
        m_prev = m_ref[...]
        m_new = jnp.maximum(m_prev, jnp.max(s, axis=1, keepdims=True))
        alpha = jnp.exp2(m_prev - m_new)
        p = jnp.exp2(s - m_new[:, 0:1]).astype(BF16)
        pv = jnp.dot(p, vb, preferred_element_type=F32)
        acc_ref[:, 0:LANES] = alpha * acc_ref[:, 0:LANES] + pv[:, 0:LANES]
        acc_ref[:, LANES:2 * LANES] = alpha * acc_ref[:, LANES:2 * LANES] + pv[:, LANES:2 * LANES]
        m_ref[...] = m_new

    def kblock(j):
        return k_ref[0, pl.ds(pl.multiple_of(j * tk, tk), tk), :]

    def vblock(j):
        return v_ref[0, pl.ds(pl.multiple_of(j * tk, tk), tk), :]

    sa_ref[...] = scores(kblock(0))

    def body(i, carry):
        j = 2 * i
        sb_ref[...] = scores(kblock(j + 1))
        update(sa_ref[...], vblock(j))
        sa_ref[...] = scores(kblock(j + 2))
        update(sb_ref[...], vblock(j + 1))
        return carry

    lax.fori_loop(0, nk // 2 - 1, body, 0)
    sb_ref[...] = scores(kblock(nk - 1))
    update(sa_ref[...], vblock(nk - 2))
    col = lax.broadcasted_iota(jnp.int32, (2 * tq, LANES), 1)
    s_meta = jnp.where(col < N_META, scores(kt_ref[0]), NEG_INF)
    update(sb_ref[...], vblock(nk - 1))
    update(s_meta, vt_ref[0])

    o1 = acc_ref[0:tq, 0:LANES] / acc_ref[0:tq, LANES:2 * LANES]
    o2 = acc_ref[tq:2 * tq, 0:LANES] / acc_ref[tq:2 * tq, LANES:2 * LANES]
    lam = (jnp.exp(jnp.sum(lam_ref[0:1, :] * lam_ref[1:2, :], axis=1, keepdims=True))
           - jnp.exp(jnp.sum(lam_ref[2:3, :] * lam_ref[3:4, :], axis=1, keepdims=True)) + LAM_INIT)
    o = o1 - lam * o2
    ms = jnp.mean(o * o, axis=-1, keepdims=True)
    o_ref[0] = (o * lax.rsqrt(ms + EPS) * gsub_ref[...] * (1.0 - LAM_INIT)).astype(BF16)


def _attn(q, k, v, kt, vt, lam4, g_sub):
    b, s, _ = q.shape
    tq, tk = ATT_TQ, ATT_TK
    kspec = lambda rows: pl.BlockSpec((1, rows, LANES), lambda bi, hi, qi: (bi, 0, hi))
    vspec = lambda rows: pl.BlockSpec((1, rows, 2 * LANES), lambda bi, hi, qi: (bi, 0, hi))
    return pl.pallas_call(
        functools.partial(_attn_kernel, tq, tk, s // tk),
        grid=(b, ATT_HEADS, s // tq),
        in_specs=[pl.BlockSpec((1, tq, LANES), lambda bi, hi, qi: (bi, qi, hi)),
                  kspec(s), vspec(s), kspec(LANES), vspec(LANES),
                  _const_spec((4, ATT_QK_DIM)), _const_spec((1, ATT_V_DIM))],
        out_specs=pl.BlockSpec((1, tq, LANES), lambda bi, hi, qi: (bi, qi, hi)),
        out_shape=jax.ShapeDtypeStruct((b, s, ATT_WIDTH), BF16),
        scratch_shapes=[pltpu.VMEM((2 * tq, LANES), BF16), pltpu.VMEM((2 * tq, LANES), F32),
                        pltpu.VMEM((2 * tq, 2 * LANES), F32),
                        pltpu.VMEM((2 * tq, tk), F32), pltpu.VMEM((2 * tq, tk), F32)],
        compiler_params=_cparams(("parallel", "parallel", "parallel")),
        name="diff_attn",
    )(q, k, v, kt, vt, lam4, g_sub)


def _mlstm_chunk(reverse, q, k, v, g_tm, g_fm, s_ref, m_ref, h_ref):
    r = lax.broadcasted_iota(jnp.int32, (CHUNK, CHUNK), 0)
    c = lax.broadcasted_iota(jnp.int32, (CHUNK, CHUNK), 1)
    if reverse:
        causal = c >= r
        last = 0
        gbase = 2 * ML_HEADS
    else:
        causal = c <= r
        last = CHUNK - 1
        gbase = 0
    b_cols = jnp.dot(causal.astype(F32), g_tm, preferred_element_type=F32,
                     precision=lax.Precision.HIGHEST)
    b_rows = jnp.dot(g_fm, (r <= c if not reverse else r >= c).astype(F32),
                     preferred_element_type=F32, precision=lax.Precision.HIGHEST)
    ones_col = (c == 0).astype(BF16)
    for h in range(ML_HEADS):
        ic, fc = gbase + h, gbase + ML_HEADS + h
        b_col = b_cols[:, fc:fc + 1]
        b_row = b_rows[fc:fc + 1, :]
        li_row = g_fm[ic:ic + 1, :]
        li_col = g_tm[:, ic:ic + 1]
        m_prev = m_ref[h:h + 1, 0:1]
        log_d = jnp.where(causal, b_col - b_row + li_row, NEG_INF)
        m_inter = b_col + m_prev
        mj = jnp.maximum(m_inter, jnp.max(log_d, axis=1, keepdims=True))
        qh = q[:, h * ML_DIM:(h + 1) * ML_DIM]
        kh = k[:, h * ML_DIM:(h + 1) * ML_DIM]
        vext = jnp.concatenate([v[:, h * ML_DIM:(h + 1) * ML_DIM], ones_col], axis=1)
        qk = lax.dot_general(qh, kh, (((1,), (1,)), ((), ())), preferred_element_type=F32)
        w = (qk * jnp.exp(log_d - mj)).astype(BF16)
        state = s_ref[h]
        tot = (jnp.exp(m_inter - mj) * jnp.dot(qh, state.astype(BF16), preferred_element_type=F32)
               + jnp.dot(w, vext, preferred_element_type=F32))
        den = tot[:, ML_DIM:ML_DIM + 1]
        h_ref[0, :, h * ML_DIM:(h + 1) * ML_DIM] = (
            tot[:, :ML_DIM] / jnp.maximum(jnp.abs(den), jnp.exp(-mj)))
        b_last = b_col[last:last + 1, :]
        logw_col = b_last - b_col + li_col
        logw_row = b_last - b_row + li_row
        m_new = jnp.maximum(b_last + m_prev, jnp.max(logw_row, axis=1, keepdims=True))
        kw = (kh.astype(F32) * jnp.exp(logw_col - m_new)).astype(BF16)
        s_ref[h] = (jnp.exp(b_last + m_prev - m_new) * state
                    + lax.dot_general(kw, vext, (((0,), (0,)), ((), ())), preferred_element_type=F32))
        m_ref[h:h + 1, :] = jnp.broadcast_to(m_new, (1, LANES))


def _mlstm_kernel(qf_ref, kf_ref, vf_ref, gtf_ref, gff_ref, qb_ref, kb_ref, vb_ref, gtb_ref, gfb_ref,
                  s0_ref, m0_ref, hf_ref, hb_ref, sfin_ref, mfin_ref,
                  sf_ref, sb_ref, mf_ref, mb_ref):
    j = pl.program_id(1)

    @pl.when(j == 0)
    def _():
        sf_ref[...] = s0_ref[0]
        mf_ref[...] = m0_ref[0]
        sb_ref[...] = jnp.zeros(sb_ref.shape, F32)
        mb_ref[...] = jnp.zeros(mb_ref.shape, F32)

    _mlstm_chunk(False, qf_ref[0], kf_ref[0], vf_ref[0], gtf_ref[0], gff_ref[0], sf_ref, mf_ref, hf_ref)
    _mlstm_chunk(True, qb_ref[0], kb_ref[0], vb_ref[0], gtb_ref[0], gfb_ref[0], sb_ref, mb_ref, hb_ref)

    @pl.when(j == pl.num_programs(1) - 1)
    def _():
        sfin_ref[0] = sf_ref[...]
        mfin_ref[0] = mf_ref[...]


def _mlstm(qm, km, vm, g_tm, g_fm, s0, m0):
    b, s, _ = qm.shape
    nc = s // CHUNK
    fwd = lambda width: pl.BlockSpec((1, CHUNK, width), lambda bi, ci: (bi, ci, 0))
    bwd = lambda width: pl.BlockSpec((1, CHUNK, width), lambda bi, ci: (bi, nc - 1 - ci, 0))
    gf_fwd = pl.BlockSpec((1, N_GATES, CHUNK), lambda bi, ci: (bi, 0, ci))
    gf_bwd = pl.BlockSpec((1, N_GATES, CHUNK), lambda bi, ci: (bi, 0, nc - 1 - ci))
    st = pl.BlockSpec((1, ML_HEADS, ML_DIM, 2 * LANES), lambda bi, ci: (bi, 0, 0, 0))
    mx = pl.BlockSpec((1, SUBLANES, LANES), lambda bi, ci: (bi, 0, 0))
    return pl.pallas_call(
        _mlstm_kernel,
        grid=(b, nc),
        in_specs=[fwd(ML_WIDTH)] * 3 + [fwd(N_GATES), gf_fwd] + [bwd(ML_WIDTH)] * 3 + [bwd(N_GATES), gf_bwd]
        + [st, mx],
        out_specs=[fwd(ML_WIDTH), bwd(ML_WIDTH), st, mx],
        out_shape=[jax.ShapeDtypeStruct((b, s, ML_WIDTH), F32)] * 2
        + [jax.ShapeDtypeStruct((b, ML_HEADS, ML_DIM, 2 * LANES), F32),
           jax.ShapeDtypeStruct((b, SUBLANES, LANES), F32)],
        scratch_shapes=[pltpu.VMEM((ML_HEADS, ML_DIM, 2 * LANES), F32)] * 2
        + [pltpu.VMEM((SUBLANES, LANES), F32)] * 2,
        compiler_params=_cparams(("parallel", "arbitrary")),
        name="mlstm",
    )(qm, km, vm, g_tm, g_fm, qm, km, vm, g_tm, g_fm, s0, m0)


def _top_values(x, n, with_rank=False):
    vals = []
    cur = x
    rank = jnp.full(x.shape, float(n), F32) if with_rank else None
    for i in range(n):
        mx = jnp.max(cur, axis=0, keepdims=True)
        vals.append(mx)
        hit = cur == mx
        if with_rank:
            rank = jnp.where(hit, float(i), rank)
        if i + 1 < n:
            cur = jnp.where(hit, NEG_INF, cur)
    return (vals, rank) if with_rank else vals


def _bf16_pair_bits(x):
    u = lax.bitcast_convert_type(x.astype(BF16).astype(F32), jnp.uint32)
    return u | (u >> 16)


_CAND_PAIRS = [(a, b) for a in range(PEER_TOPK + 1) for b in range(PEER_TOPK + 1)
               if (a + 1) * (b + 1) <= PEER_TOPK + 1]
_CAND_ROWS = -(-len(_CAND_PAIRS) // SUBLANES) * SUBLANES


def _mix_kernel(tt, x_ref, att_ref, hf_ref, hb_ref, om_ref, gmh_ref, wout_ref, gffn_ref, wpqt_ref,
                k1_ref, k2_ref, h1_ref, xn2_ref, r2_ref, e2_ref, n1_ref, e1_ref, mix_ref, cand_ref):
    mix_ref[:, 0:ATT_WIDTH] = att_ref[0]
    for h in range(ML_HEADS):
        sl = slice(h * ML_DIM, (h + 1) * ML_DIM)
        hs = hf_ref[0, :, sl] + hb_ref[0, :, sl]
        ms = jnp.mean(hs * hs, axis=-1, keepdims=True)
        hn = hs * lax.rsqrt(ms + EPS) * gmh_ref[...]
        gate = 1.0 / (1.0 + jnp.exp(-om_ref[0, :, sl].astype(F32)))
        mix_ref[:, ATT_WIDTH + h * ML_DIM:ATT_WIDTH + (h + 1) * ML_DIM] = (hn * gate).astype(BF16)
    h1 = x_ref[0] + jnp.dot(mix_ref[...], wout_ref[...], preferred_element_type=F32)
    h1_ref[0] = h1
    ms = jnp.mean(h1 * h1, axis=-1, keepdims=True)
    xn2 = (h1 * lax.rsqrt(ms + EPS) * gffn_ref[...]).astype(BF16)
    xn2_ref[0] = xn2
    qt = lax.dot_general(wpqt_ref[...], xn2, (((1,), (1,)), ((), ())), preferred_element_type=F32)
    half = PEER_QDIM // 2
    cand_ref[...] = jnp.full(cand_ref.shape, NEG_INF, F32)
    for h in range(PEER_HEADS):
        q1 = qt[h * PEER_QDIM:h * PEER_QDIM + half].astype(BF16)
        q2 = qt[h * PEER_QDIM + half:(h + 1) * PEER_QDIM].astype(BF16)
        s1 = jnp.dot(k1_ref[h], q1, preferred_element_type=F32)
        s2 = jnp.dot(k2_ref[h], q2, preferred_element_type=F32)
        for sl in range(tt // LANES):
            lanes = slice(sl * LANES, (sl + 1) * LANES)
            a1 = s1[:, lanes]
            a2 = s2[:, lanes]
            v1 = _top_values(a1, PEER_TOPK + 1)
            v2, rank2 = _top_values(a2, PEER_TOPK + 1, with_rank=True)
            for idx, (a, b) in enumerate(_CAND_PAIRS):
                cand_ref[idx:idx + 1, lanes] = v1[a] + v2[b]
            cs = _top_values(cand_ref[:, lanes], PEER_TOPK + 1)
            z = jnp.zeros_like(cs[0])
            for i in range(PEER_TOPK):
                z = z + jnp.exp(cs[i] - cs[0])
            thr = 0.5 * (cs[PEER_TOPK - 1] + cs[PEER_TOPK])
            thr1 = thr - a1
            n1 = jnp.zeros_like(a1)
            for b in range(PEER_TOPK // 2):
                n1 = n1 + jnp.where(v2[b] > thr1, 1.0, 0.0)
            n_best = jnp.zeros_like(thr)
            for b in range(PEER_TOPK // 2, PEER_TOPK):
                n_best = n_best + jnp.where(v2[b] > thr - v1[0], 1.0, 0.0)
            n1 = n1 + jnp.where(a1 == v1[0], n_best, 0.0)
            r2_ref[0, h, :, lanes] = rank2.astype(BF16)
            n1_ref[0, h, :, lanes] = _bf16_pair_bits(n1)
            e1_ref[0, h, :, lanes] = _bf16_pair_bits(jnp.exp(a1 - v1[0]) * (1.0 / z))
            e2_ref[0, h, :, lanes] = jnp.exp(a2 - v2[0]).astype(BF16)


def _mix(x, att, hf, hb, om, w):
    b, s, d = x.shape
    tt = MIX_TILE
    nt = s // tt
    tspec = lambda width: pl.BlockSpec((1, tt, width), lambda bi, ti: (bi, ti, 0))
    sspec = pl.BlockSpec((1, PEER_HEADS, PEER_NKEYS, tt), lambda bi, ti: (bi, 0, 0, ti))
    return pl.pallas_call(
        functools.partial(_mix_kernel, tt),
        grid=(b, nt),
        in_specs=[tspec(d), tspec(ATT_WIDTH), tspec(ML_WIDTH), tspec(ML_WIDTH), tspec(ML_WIDTH),
                  _const_spec((1, ML_DIM)), _const_spec((d, d)), _const_spec((1, d)),
                  _const_spec((PEER_HEADS * PEER_QDIM, d)),
                  _const_spec((PEER_HEADS, PEER_NKEYS, PEER_QDIM // 2)),
                  _const_spec((PEER_HEADS, PEER_NKEYS, PEER_QDIM // 2))],
        out_specs=[tspec(d), tspec(d), sspec, sspec, sspec, sspec],
        out_shape=[jax.ShapeDtypeStruct((b, s, d), F32), jax.ShapeDtypeStruct((b, s, d), BF16),
                   jax.ShapeDtypeStruct((b, PEER_HEADS, PEER_NKEYS, s), BF16),
                   jax.ShapeDtypeStruct((b, PEER_HEADS, PEER_NKEYS, s), BF16),
                   jax.ShapeDtypeStruct((b, PEER_HEADS, PEER_NKEYS, s), jnp.uint32),
                   jax.ShapeDtypeStruct((b, PEER_HEADS, PEER_NKEYS, s), jnp.uint32)],
        scratch_shapes=[pltpu.VMEM((tt, d), BF16), pltpu.VMEM((_CAND_ROWS, tt), F32)],
        compiler_params=_cparams(("parallel", "parallel")),
        name="mix_route",
    )(x, att, hf, hb, om, w["g_mh"], w["w_out"], w["g_ffn"], w["w_pqt"], w["sub_k1"], w["sub_k2"])


def _gelu(x):
    return 0.5 * x * (1.0 + lax.erf(x * (2.0 ** -0.5)))


def _peer_kernel(tt, ng, xn2_ref, h1_ref, r2_ref, e2_ref, n1_ref, e1_ref, u_ref, vt_ref, y_ref,
                 a0_ref, a1_ref, w_ref, acc_ref):
    s = pl.program_id(1)
    rows = PEER_GROUP // PEER_NKEYS

    @pl.when(s == 0)
    def _():
        a0_ref[...] = jnp.zeros(a0_ref.shape, F32)
        a1_ref[...] = jnp.zeros(a1_ref.shape, F32)

    @pl.when((s < 1) | ((s + ng - 1) % ng == 0))
    def _():
        acc_ref[...] = jnp.zeros(acc_ref.shape, F32)

    g2 = (s + ng - 1) % ng

    def row(ref, h, c):
        words = jnp.broadcast_to(ref[0, h, pl.ds(c, 1), :], (PEER_PASS_ROWS // 2, tt))
        return pltpu.bitcast(words, BF16)

    def stages(a_cur, a_oth):
        a_cur[...] = lax.dot_general(u_ref[...], xn2_ref[0], (((1,), (1,)), ((), ())),
                                     preferred_element_type=F32)
        zero = jnp.zeros((PEER_PASS_ROWS, tt), BF16)
        for kb in range(PEER_GROUP // PEER_VBLOCK):
            for cc in range(kb * PEER_VBLOCK // PEER_NKEYS, (kb + 1) * PEER_VBLOCK // PEER_NKEYS):
                c = g2 * rows + cc
                for rb in range(PEER_NKEYS // PEER_PASS_ROWS):
                    i2 = slice(rb * PEER_PASS_ROWS, (rb + 1) * PEER_PASS_ROWS)
                    ex = slice(cc * PEER_NKEYS + rb * PEER_PASS_ROWS,
                               cc * PEER_NKEYS + (rb + 1) * PEER_PASS_ROWS)
                    wsum = zero
                    for h in range(PEER_HEADS):
                        sel = r2_ref[0, h, i2, :] < row(n1_ref, h, c)
                        wsum = wsum + jnp.where(sel, e2_ref[0, h, i2, :], zero) * row(e1_ref, h, c)
                    w_ref[ex, :] = wsum * _gelu(a_oth[ex, :]).astype(BF16)
            blk = slice(kb * PEER_VBLOCK, (kb + 1) * PEER_VBLOCK)
            acc_ref[...] += jnp.dot(vt_ref[:, blk], w_ref[blk, :], preferred_element_type=F32)

    @pl.when(s % 2 == 0)
    def _():
        stages(a0_ref, a1_ref)

    @pl.when(s % 2 == 1)
    def _():
        stages(a1_ref, a0_ref)

    @pl.when((s >= 1) & ((s + ng - 1) % ng == ng - 1))
    def _():
        y_ref[0] = h1_ref[0] + acc_ref[...].T


def _peer(xn2, h1, r2, e2, n1, e1, w):
    b, s, d = h1.shape
    tt = PEER_TILE
    nt = s // tt
    ng = (PEER_NKEYS * PEER_NKEYS) // PEER_GROUP
    tile = lambda si, lag: jnp.clip((si - lag) // ng, 0, nt - 1)
    tspec = lambda lag: pl.BlockSpec((1, tt, d), lambda bi, si: (bi, tile(si, lag), 0))
    sspec = pl.BlockSpec((1, PEER_HEADS, PEER_NKEYS, tt), lambda bi, si: (bi, 0, 0, tile(si, 1)))
    return pl.pallas_call(
        functools.partial(_peer_kernel, tt, ng),
        grid=(b, nt * ng + 1),
        in_specs=[tspec(0), tspec(1), sspec, sspec, sspec, sspec,
                  pl.BlockSpec((PEER_GROUP, d), lambda bi, si: (si % ng, 0)),
                  pl.BlockSpec((d, PEER_GROUP), lambda bi, si: (0, (si + ng - 1) % ng))],
        out_specs=tspec(1),
        out_shape=jax.ShapeDtypeStruct((b, s, d), F32),
        scratch_shapes=[pltpu.VMEM((PEER_GROUP, tt), F32)] * 2 + [pltpu.VMEM((PEER_GROUP, tt), BF16)]
        + [pltpu.VMEM((d, tt), F32)],
        compiler_params=_cparams(("parallel", "arbitrary")),
        name="peer_dense",
    )(xn2, h1, r2, e2, n1, e1, w["peer_u"], w["peer_vt"])


def _prep_weights(g_mix, w_in, b_gates, conv_w, conv_b, g_qn, g_kn, lam_q1, lam_k1, lam_q2, lam_k2,
                  g_sub, g_mh, w_out, g_ffn, w_pq, sub_k1, sub_k2, peer_u, peer_v):
    half = ROT_DIM // 2
    inv = ROPE_THETA ** (-jnp.arange(half, dtype=F32) * 2.0 / ROT_DIM)
    w_g = w_in[0][:, N_MAIN:]
    return {
        "g_mix": g_mix[0][None, :],
        "w_main": w_in[0][:, :N_MAIN].astype(BF16),
        "w_g": w_g.astype(BF16),
        "w_gt": w_g.T.astype(BF16),
        "b_g": b_gates[0][None, :],
        "b_gt": b_gates[0][:, None],
        "g_qk": jnp.concatenate([jnp.tile(g_qn[0], ATT_WIDTH // ATT_QK_DIM),
                                 jnp.tile(g_kn[0], ATT_WIDTH // ATT_QK_DIM)])[None, :],
        "inv_lane": jnp.tile(inv, LANES // half)[None, :],
        "conv_w": conv_w[0][:, 0, :],
        "conv_b": conv_b[0][None, :],
        "lam4": jnp.stack([lam_q1[0], lam_k1[0], lam_q2[0], lam_k2[0]]),
        "g_sub": g_sub[0][None, :],
        "g_mh": g_mh[0][None, :],
        "w_out": w_out[0].astype(BF16),
        "g_ffn": g_ffn[0][None, :],
        "w_pqt": w_pq[0].T.astype(BF16),
        "sub_k1": sub_k1[0].astype(BF16),
        "sub_k2": sub_k2[0].astype(BF16),
        "peer_u": peer_u[0].astype(BF16),
        "peer_vt": peer_v[0].T.astype(BF16),
    }


def _trunk(x, meta, w):
    b, s, d = x.shape
    tt = PROJ_TILE
    assert s % tt == 0 and s % (2 * ATT_TK) == 0 and s % MIX_TILE == 0 and s % PEER_TILE == 0
    nt = s // tt
    xr = x.reshape(b, nt, tt, d)
    meta_tail = jnp.broadcast_to(meta[N_META - HALO:][None, None], (b, 1, HALO, d))
    prev = jnp.concatenate([meta_tail, xr[:, :-1, tt - HALO:, :]], axis=1)
    nxt = jnp.concatenate([xr[:, 1:, :HALO, :], jnp.zeros((b, 1, HALO, d), x.dtype)], axis=1)
    q, k, v, qm, km, vm, om, g_tm, g_fm = _proj(x, prev, nxt, N_META, tt, w)

    meta_b = jnp.broadcast_to(meta[None], (b, N_META, d))
    m_prev = jnp.zeros((b, 1, HALO, d), x.dtype)
    m_next = x[:, None, :HALO, :]
    _, k_m, v_m, qm_m, km_m, vm_m, _, gtm_m, gfm_m = _proj(meta_b, m_prev, m_next, 0, N_META, w)

    pad = CHUNK - N_META
    att = _attn(q, k, v, jnp.pad(k_m, ((0, 0), (0, pad), (0, 0))), jnp.pad(v_m, ((0, 0), (0, pad), (0, 0))),
                w["lam4"], w["g_sub"])

    front = lambda a: jnp.pad(a, ((0, 0), (pad, 0), (0, 0)))
    is_input_gate = (jnp.arange(N_GATES) // ML_HEADS) % 2 == 0
    pad_gate = jnp.where(is_input_gate, NEG_INF, 0.0).astype(F32)
    gtm_p = jnp.concatenate([jnp.broadcast_to(pad_gate[None, None, :], (b, pad, N_GATES)), gtm_m], axis=1)
    gfm_p = jnp.concatenate([jnp.broadcast_to(pad_gate[None, :, None], (b, N_GATES, pad)), gfm_m], axis=2)
    s_zero = jnp.zeros((b, ML_HEADS, ML_DIM, 2 * LANES), F32)
    m_zero = jnp.zeros((b, SUBLANES, LANES), F32)
    _, _, s0, m0 = _mlstm(front(qm_m), front(km_m), front(vm_m), gtm_p, gfm_p, s_zero, m_zero)
    hf, hb, _, _ = _mlstm(qm, km, vm, g_tm, g_fm, s0, m0)

    h1, xn2, r2, e2, n1, e1 = _mix(x, att, hf, hb, om, w)
    return _peer(xn2, h1, r2, e2, n1, e1, w)


def kernel(x_prompt, x_sample, meta, g_mix, w_in, b_gates, conv_w, conv_b, g_qn, g_kn, lam_q1, lam_k1,
           lam_q2, lam_k2, g_sub, g_mh, w_out, g_ffn, w_pq, sub_k1, sub_k2, peer_u, peer_v):
    w = _prep_weights(g_mix, w_in, b_gates, conv_w, conv_b, g_qn, g_kn, lam_q1, lam_k1, lam_q2, lam_k2,
                      g_sub, g_mh, w_out, g_ffn, w_pq, sub_k1, sub_k2, peer_u, peer_v)
    return (_trunk(x_prompt, meta, w), _trunk(x_sample, meta, w))
```

```python
import functools
import math

import jax
import jax.numpy as jnp
from jax import lax
from jax.experimental import pallas as pl
from jax.experimental.pallas import tpu as pltpu

F32 = jnp.float32
BF16 = jnp.bfloat16

D_MODEL = 1024
N_META = 16
ATT_HEADS = 4
ATT_QK_DIM = 64
ATT_V_DIM = 128
ATT_WIDTH = ATT_HEADS * ATT_V_DIM
ML_HEADS = 4
ML_DIM = 128
ML_WIDTH = ML_HEADS * ML_DIM
N_GATES = 4 * ML_HEADS
N_MAIN = 3 * ATT_WIDTH + 4 * ML_WIDTH
ROT_DIM = ATT_QK_DIM // 4
ROPE_THETA = 500000.0
CHUNK = 128
PEER_HEADS = 8
PEER_NKEYS = 128
PEER_QDIM = 256
PEER_TOPK = 16
EPS = 1e-6
LAM_INIT = 0.8 - 0.6 * math.exp(-0.3 * 0)
Q_SCALE = ATT_QK_DIM ** -0.5 * math.log2(math.e)

LANES = 128
SUBLANES = 8
HALO = SUBLANES
VMEM_LIMIT = 56 * 1024 * 1024

PROJ_TILE = 512
ATT_TQ = 512
ATT_TK = 1024
MIX_TILE = 512
PEER_TILE = 512
PEER_GROUP = 1024
PEER_PASS_ROWS = 64
PEER_VBLOCK = 256
NEG_INF = float("-inf")


def _cparams(sem):
    return pltpu.CompilerParams(dimension_semantics=sem, vmem_limit_bytes=VMEM_LIMIT)


def _const_spec(shape):
    nd = len(shape)
    return pl.BlockSpec(shape, lambda *_: (0,) * nd)


def _log_sigmoid(x):
    return -(jnp.maximum(-x, 0.0) + jnp.log1p(jnp.exp(-jnp.abs(x))))


def _proj_kernel(pos_base, tt,
                 x_ref, prev_ref, next_ref, gmix_ref, wmain_ref, wg_ref, wgt_ref, bg_ref, bgt_ref,
                 gqk_ref, inv_ref, convw_ref, convb_ref,
                 q_ref, k_ref, v_ref, qm_ref, km_ref, vm_ref, om_ref, gtm_ref, gfm_ref,
                 xe_ref, zc_ref):
    i = pl.program_id(1)
    xe_ref[0:HALO, :] = prev_ref[0, 0]
    xe_ref[HALO:HALO + tt, :] = x_ref[0]
    xe_ref[HALO + tt:2 * HALO + tt, :] = next_ref[0, 0]
    xe = xe_ref[...]
    ms = jnp.mean(xe * xe, axis=-1, keepdims=True)
    xn_ext = (xe * lax.rsqrt(ms + EPS) * gmix_ref[...]).astype(BF16)
    xn = xn_ext[HALO:HALO + tt]

    def mm(a, lo, hi):
        return jnp.dot(a, wmain_ref[:, lo:hi], preferred_element_type=F32)

    zqk = mm(xn, 0, 2 * ATT_WIDTH)
    sq = (zqk * zqk).astype(BF16)
    r = lax.broadcasted_iota(jnp.int32, (2 * LANES, 2 * LANES), 0) // ATT_QK_DIM
    c = lax.broadcasted_iota(jnp.int32, (2 * LANES, 2 * LANES), 1) // ATT_QK_DIM
    seg_ones = (r == c).astype(BF16)
    lane = lax.broadcasted_iota(jnp.int32, (tt, LANES), 1)
    l64 = lane % ATT_QK_DIM
    row = lax.broadcasted_iota(jnp.int32, (tt, LANES), 0)
    pos = (pos_base + i * tt + row).astype(F32)
    ang = pos * inv_ref[...]
    cos_a = jnp.cos(ang)
    sin_a = jnp.sin(ang)
    half = ROT_DIM // 2
    cos_m = jnp.where(l64 < ROT_DIM, cos_a, 1.0)
    sin_lo = jnp.where(l64 < half, -sin_a, 0.0)
    sin_hi = jnp.where((l64 >= half) & (l64 < ROT_DIM), sin_a, 0.0)
    for cb in range(4):
        lo = cb * 2 * LANES
        ss = jnp.dot(sq[:, lo:lo + 2 * LANES], seg_ones, preferred_element_type=F32)
        rn = lax.rsqrt(ss * (1.0 / ATT_QK_DIM) + EPS)
        qn = zqk[:, lo:lo + 2 * LANES] * rn * gqk_ref[:, lo:lo + 2 * LANES]
        for sb in range(2):
            xs = qn[:, sb * LANES:(sb + 1) * LANES]
            rot = (xs * cos_m + pltpu.roll(xs, LANES - half, 1) * sin_lo
                   + pltpu.roll(xs, half, 1) * sin_hi)
            col = lo + sb * LANES
            if col < ATT_WIDTH:
                q_ref[0, :, col:col + LANES] = (rot * Q_SCALE).astype(BF16)
            else:
                k_ref[0, :, col - ATT_WIDTH:col - ATT_WIDTH + LANES] = rot.astype(BF16)

    va = mm(xn, 2 * ATT_WIDTH, 3 * ATT_WIDTH).astype(BF16)
    for h in range(ATT_HEADS):
        v_ref[0, :, 2 * h * LANES:(2 * h + 1) * LANES] = va[:, h * ATT_V_DIM:(h + 1) * ATT_V_DIM]
        v_ref[0, :, (2 * h + 1) * LANES:(2 * h + 2) * LANES] = jnp.ones((tt, LANES), BF16)

    base = 3 * ATT_WIDTH
    zc_ref[...] = mm(xn_ext, base, base + 2 * ML_WIDTH)
    y = (convw_ref[0:1, :] * zc_ref[pl.ds(HALO - 1, tt), :]
         + convw_ref[1:2, :] * zc_ref[pl.ds(HALO, tt), :]
         + convw_ref[2:3, :] * zc_ref[pl.ds(HALO + 1, tt), :]
         + convb_ref[...])
    act = y * (1.0 / (1.0 + jnp.exp(-y)))
    qm_ref[0] = act[:, :ML_WIDTH].astype(BF16)
    km_ref[0] = (act[:, ML_WIDTH:] * (ML_DIM ** -0.5)).astype(BF16)
    vm_ref[0] = mm(xn, base + 2 * ML_WIDTH, base + 3 * ML_WIDTH).astype(BF16)
    om_ref[0] = mm(xn, base + 3 * ML_WIDTH, base + 4 * ML_WIDTH).astype(BF16)

    g_tm = jnp.dot(xn, wg_ref[...], preferred_element_type=F32) + bg_ref[...]
    g_fm = lax.dot_general(wgt_ref[...], xn, (((1,), (1,)), ((), ())),
                           preferred_element_type=F32) + bgt_ref[...]
    col_tm = lax.broadcasted_iota(jnp.int32, (tt, N_GATES), 1)
    row_fm = lax.broadcasted_iota(jnp.int32, (N_GATES, tt), 0)
    gtm_ref[0] = jnp.where((col_tm // ML_HEADS) % 2 == 1, _log_sigmoid(g_tm), g_tm)
    gfm_ref[0] = jnp.where((row_fm // ML_HEADS) % 2 == 1, _log_sigmoid(g_fm), g_fm)


def _proj(x, prev, nxt, pos_base, tt, w):
    b, s, d = x.shape
    nt = s // tt
    tok = lambda width, dt: jax.ShapeDtypeStruct((b, s, width), dt)
    out_shape = ([tok(ATT_WIDTH, BF16)] * 2 + [tok(2 * ATT_WIDTH, BF16)] + [tok(ML_WIDTH, BF16)] * 4
                 + [tok(N_GATES, F32), jax.ShapeDtypeStruct((b, N_GATES, s), F32)])
    tspec = lambda width: pl.BlockSpec((1, tt, width), lambda bi, ti: (bi, ti, 0))
    hspec = pl.BlockSpec((1, 1, HALO, d), lambda bi, ti: (bi, ti, 0, 0))
    return pl.pallas_call(
        functools.partial(_proj_kernel, pos_base, tt),
        grid=(b, nt),
        in_specs=[tspec(d), hspec, hspec, _const_spec((1, d)), _const_spec((d, N_MAIN)),
                  _const_spec((d, N_GATES)), _const_spec((N_GATES, d)), _const_spec((1, N_GATES)),
                  _const_spec((N_GATES, 1)), _const_spec((1, 2 * ATT_WIDTH)), _const_spec((1, LANES)),
                  _const_spec((3, 2 * ML_WIDTH)), _const_spec((1, 2 * ML_WIDTH))],
        out_specs=[tspec(ATT_WIDTH)] * 2 + [tspec(2 * ATT_WIDTH)] + [tspec(ML_WIDTH)] * 4
        + [tspec(N_GATES), pl.BlockSpec((1, N_GATES, tt), lambda bi, ti: (bi, 0, ti))],
        out_shape=out_shape,
        scratch_shapes=[pltpu.VMEM((tt + 2 * HALO, d), F32), pltpu.VMEM((tt + 2 * HALO, 2 * ML_WIDTH), F32)],
        compiler_params=_cparams(("parallel", "parallel")),
        name=f"proj_t{tt}",
    )(x, prev, nxt, w["g_mix"], w["w_main"], w["w_g"], w["w_gt"], w["b_g"], w["b_gt"],
      w["g_qk"], w["inv_lane"], w["conv_w"], w["conv_b"])


def _attn_kernel(tq, tk, nk,
                 q_ref, k_ref, v_ref, kt_ref, vt_ref, lam_ref, gsub_ref, o_ref,
                 qq_ref, m_ref, acc_ref, sa_ref, sb_ref):
    q = q_ref[0]
    lane = lax.broadcasted_iota(jnp.int32, (tq, LANES), 1)
    zero = jnp.zeros_like(q)
    qq_ref[0:tq, :] = jnp.where(lane < ATT_QK_DIM, q, zero)
    qq_ref[tq:2 * tq, :] = jnp.where(lane >= ATT_QK_DIM, q, zero)
    m_ref[...] = jnp.full(m_ref.shape, NEG_INF, F32)
    acc_ref[...] = jnp.zeros(acc_ref.shape, F32)

    def scores(kb):
        return lax.dot_general(qq_ref[...], kb, (((1,), (1,)), ((), ())), preferred_element_type=F32)

    def update(s, vb):
        m_prev = m_ref[...]
        m_new = jnp.maximum(m_prev, jnp.max(s, axis=1, keepdims=True))
        alpha = jnp.exp2(m_prev - m_new)
        p = jnp.exp2(s - m_new[:, 0:1]).astype(BF16)
        pv = jnp.dot(p, vb, preferred_element_type=F32)
        acc_ref[:, 0:LANES] = alpha * acc_ref[:, 0:LANES] + pv[:, 0:LANES]
        acc_ref[:, LANES:2 * LANES] = alpha * acc_ref[:, LANES:2 * LANES] + pv[:, LANES:2 * LANES]
        m_ref[...] = m_new

    def kblock(j):
        return k_ref[0, pl.ds(pl.multiple_of(j * tk, tk), tk), :]

    def vblock(j):
        return v_ref[0, pl.ds(pl.multiple_of(j * tk, tk), tk), :]

    sa_ref[...] = scores(kblock(0))

    def body(i, carry):
        j = 2 * i
        sb_ref[...] = scores(kblock(j + 1))
        update(sa_ref[...], vblock(j))
        sa_ref[...] = scores(kblock(j + 2))
        update(sb_ref[...], vblock(j + 1))
        return carry

    lax.fori_loop(0, nk // 2 - 1, body, 0)
    sb_ref[...] = scores(kblock(nk - 1))
    update(sa_ref[...], vblock(nk - 2))
    col = lax.broadcasted_iota(jnp.int32, (2 * tq, LANES), 1)
    s_meta = jnp.where(col < N_META, scores(kt_ref[0]), NEG_INF)
    update(sb_ref[...], vblock(nk - 1))
    update(s_meta, vt_ref[0])

    o1 = acc_ref[0:tq, 0:LANES] / acc_ref[0:tq, LANES:2 * LANES]
    o2 = acc_ref[tq:2 * tq, 0:LANES] / acc_ref[tq:2 * tq, LANES:2 * LANES]
    lam = (jnp.exp(jnp.sum(lam_ref[0:1, :] * lam_ref[1:2, :], axis=1, keepdims=True))
           - jnp.exp(jnp.sum(lam_ref[2:3, :] * lam_ref[3:4, :], axis=1, keepdims=True)) + LAM_INIT)
    o = o1 - lam * o2
    ms = jnp.mean(o * o, axis=-1, keepdims=True)
    o_ref[0] = (o * lax.rsqrt(ms + EPS) * gsub_ref[...] * (1.0 - LAM_INIT)).astype(BF16)


def _attn(q, k, v, kt, vt, lam4, g_sub):
    b, s, _ = q.shape
    tq, tk = ATT_TQ, ATT_TK
    kspec = lambda rows: pl.BlockSpec((1, rows, LANES), lambda bi, hi, qi: (bi, 0, hi))
    vspec = lambda rows: pl.BlockSpec((1, rows, 2 * LANES), lambda bi, hi, qi: (bi, 0, hi))
    return pl.pallas_call(
        functools.partial(_attn_kernel, tq, tk, s // tk),
        grid=(b, ATT_HEADS, s // tq),
        in_specs=[pl.BlockSpec((1, tq, LANES), lambda bi, hi, qi: (bi, qi, hi)),
                  kspec(s), vspec(s), kspec(LANES), vspec(LANES),
                  _const_spec((4, ATT_QK_DIM)), _const_spec((1, ATT_V_DIM))],
        out_specs=pl.BlockSpec((1, tq, LANES), lambda bi, hi, qi: (bi, qi, hi)),
        out_shape=jax.ShapeDtypeStruct((b, s, ATT_WIDTH), BF16),
        scratch_shapes=[pltpu.VMEM((2 * tq, LANES), BF16), pltpu.VMEM((2 * tq, LANES), F32),
                        pltpu.VMEM((2 * tq, 2 * LANES), F32),
                        pltpu.VMEM((2 * tq, tk), F32), pltpu.VMEM((2 * tq, tk), F32)],
        compiler_params=_cparams(("parallel", "parallel", "parallel")),
        name="diff_attn",
    )(q, k, v, kt, vt, lam4, g_sub)


def _mlstm_chunk(reverse, q, k, v, g_tm, g_fm, s_ref, m_ref, h_ref):
    r = lax.broadcasted_iota(jnp.int32, (CHUNK, CHUNK), 0)
    c = lax.broadcasted_iota(jnp.int32, (CHUNK, CHUNK), 1)
    if reverse:
        causal = c >= r
        last = 0
        gbase = 2 * ML_HEADS
    else:
        causal = c <= r
        last = CHUNK - 1
        gbase = 0
    b_cols = jnp.dot(causal.astype(F32), g_tm, preferred_element_type=F32,
                     precision=lax.Precision.HIGHEST)
    b_rows = jnp.dot(g_fm, (r <= c if not reverse else r >= c).astype(F32),
                     preferred_element_type=F32, precision=lax.Precision.HIGHEST)
    ones_col = (c == 0).astype(BF16)
    for h in range(ML_HEADS):
        ic, fc = gbase + h, gbase + ML_HEADS + h
        b_col = b_cols[:, fc:fc + 1]
        b_row = b_rows[fc:fc + 1, :]
        li_row = g_fm[ic:ic + 1, :]
        li_col = g_tm[:, ic:ic + 1]
        m_prev = m_ref[h:h + 1, 0:1]
        log_d = jnp.where(causal, b_col - b_row + li_row, NEG_INF)
        m_inter = b_col + m_prev
        mj = jnp.maximum(m_inter, jnp.max(log_d, axis=1, keepdims=True))
        qh = q[:, h * ML_DIM:(h + 1) * ML_DIM]
        kh = k[:, h * ML_DIM:(h + 1) * ML_DIM]
        vext = jnp.concatenate([v[:, h * ML_DIM:(h + 1) * ML_DIM], ones_col], axis=1)
        qk = lax.dot_general(qh, kh, (((1,), (1,)), ((), ())), preferred_element_type=F32)
        w = (qk * jnp.exp(log_d - mj)).astype(BF16)
        state = s_ref[h]
        tot = (jnp.exp(m_inter - mj) * jnp.dot(qh, state.astype(BF16), preferred_element_type=F32)
               + jnp.dot(w, vext, preferred_element_type=F32))
        den = tot[:, ML_DIM:ML_DIM + 1]
        h_ref[0, :, h * ML_DIM:(h + 1) * ML_DIM] = (
            tot[:, :ML_DIM] / jnp.maximum(jnp.abs(den), jnp.exp(-mj)))
        b_last = b_col[last:last + 1, :]
        logw_col = b_last - b_col + li_col
        logw_row = b_last - b_row + li_row
        m_new = jnp.maximum(b_last + m_prev, jnp.max(logw_row, axis=1, keepdims=True))
        kw = (kh.astype(F32) * jnp.exp(logw_col - m_new)).astype(BF16)
        s_ref[h] = (jnp.exp(b_last + m_prev - m_new) * state
                    + lax.dot_general(kw, vext, (((0,), (0,)), ((), ())), preferred_element_type=F32))
        m_ref[h:h + 1, :] = jnp.broadcast_to(m_new, (1, LANES))


def _mlstm_kernel(qf_ref, kf_ref, vf_ref, gtf_ref, gff_ref, qb_ref, kb_ref, vb_ref, gtb_ref, gfb_ref,
                  s0_ref, m0_ref, hf_ref, hb_ref, sfin_ref, mfin_ref,
                  sf_ref, sb_ref, mf_ref, mb_ref):
    j = pl.program_id(1)

    @pl.when(j == 0)
    def _():
        sf_ref[...] = s0_ref[0]
        mf_ref[...] = m0_ref[0]
        sb_ref[...] = jnp.zeros(sb_ref.shape, F32)
        mb_ref[...] = jnp.zeros(mb_ref.shape, F32)

    _mlstm_chunk(False, qf_ref[0], kf_ref[0], vf_ref[0], gtf_ref[0], gff_ref[0], sf_ref, mf_ref, hf_ref)
    _mlstm_chunk(True, qb_ref[0], kb_ref[0], vb_ref[0], gtb_ref[0], gfb_ref[0], sb_ref, mb_ref, hb_ref)

    @pl.when(j == pl.num_programs(1) - 1)
    def _():
        sfin_ref[0] = sf_ref[...]
        mfin_ref[0] = mf_ref[...]


def _mlstm(qm, km, vm, g_tm, g_fm, s0, m0):
    b, s, _ = qm.shape
    nc = s // CHUNK
    fwd = lambda width: pl.BlockSpec((1, CHUNK, width), lambda bi, ci: (bi, ci, 0))
    bwd = lambda width: pl.BlockSpec((1, CHUNK, width), lambda bi, ci: (bi, nc - 1 - ci, 0))
    gf_fwd = pl.BlockSpec((1, N_GATES, CHUNK), lambda bi, ci: (bi, 0, ci))
    gf_bwd = pl.BlockSpec((1, N_GATES, CHUNK), lambda bi, ci: (bi, 0, nc - 1 - ci))
    st = pl.BlockSpec((1, ML_HEADS, ML_DIM, 2 * LANES), lambda bi, ci: (bi, 0, 0, 0))
    mx = pl.BlockSpec((1, SUBLANES, LANES), lambda bi, ci: (bi, 0, 0))
    return pl.pallas_call(
        _mlstm_kernel,
        grid=(b, nc),
        in_specs=[fwd(ML_WIDTH)] * 3 + [fwd(N_GATES), gf_fwd] + [bwd(ML_WIDTH)] * 3 + [bwd(N_GATES), gf_bwd]
        + [st, mx],
        out_specs=[fwd(ML_WIDTH), bwd(ML_WIDTH), st, mx],
        out_shape=[jax.ShapeDtypeStruct((b, s, ML_WIDTH), F32)] * 2
        + [jax.ShapeDtypeStruct((b, ML_HEADS, ML_DIM, 2 * LANES), F32),
           jax.ShapeDtypeStruct((b, SUBLANES, LANES), F32)],
        scratch_shapes=[pltpu.VMEM((ML_HEADS, ML_DIM, 2 * LANES), F32)] * 2
        + [pltpu.VMEM((SUBLANES, LANES), F32)] * 2,
        compiler_params=_cparams(("parallel", "arbitrary")),
        name="mlstm",
    )(qm, km, vm, g_tm, g_fm, qm, km, vm, g_tm, g_fm, s0, m0)


def _top_values(x, n, with_rank=False):
    vals = []
    cur = x
    rank = jnp.full(x.shape, float(n), F32) if with_rank else None
    for i in range(n):
        mx = jnp.max(cur, axis=0, keepdims=True)
        vals.append(mx)
        hit = cur == mx
        if with_rank:
            rank = jnp.where(hit, float(i), rank)
        if i + 1 < n:
            cur = jnp.where(hit, NEG_INF, cur)
    return (vals, rank) if with_rank else vals


def _bf16_pair_bits(x):
    u = lax.bitcast_convert_type(x.astype(BF16).astype(F32), jnp.uint32)
    return u | (u >> 16)


_CAND_PAIRS = [(a, b) for a in range(PEER_TOPK + 1) for b in range(PEER_TOPK + 1)
               if (a + 1) * (b + 1) <= PEER_TOPK + 1]
_CAND_ROWS = -(-len(_CAND_PAIRS) // SUBLANES) * SUBLANES


def _mix_kernel(tt, x_ref, att_ref, hf_ref, hb_ref, om_ref, gmh_ref, wout_ref, gffn_ref, wpqt_ref,
                k1_ref, k2_ref, h1_ref, xn2_ref, r2_ref, e2_ref, n1_ref, e1_ref, mix_ref, cand_ref):
    mix_ref[:, 0:ATT_WIDTH] = att_ref[0]
    for h in range(ML_HEADS):
        sl = slice(h * ML_DIM, (h + 1) * ML_DIM)
        hs = hf_ref[0, :, sl] + hb_ref[0, :, sl]
        ms = jnp.mean(hs * hs, axis=-1, keepdims=True)
        hn = hs * lax.rsqrt(ms + EPS) * gmh_ref[...]
        gate = 1.0 / (1.0 + jnp.exp(-om_ref[0, :, sl].astype(F32)))
        mix_ref[:, ATT_WIDTH + h * ML_DIM:ATT_WIDTH + (h + 1) * ML_DIM] = (hn * gate).astype(BF16)
    h1 = x_ref[0] + jnp.dot(mix_ref[...], wout_ref[...], preferred_element_type=F32)
    h1_ref[0] = h1
    ms = jnp.mean(h1 * h1, axis=-1, keepdims=True)
    xn2 = (h1 * lax.rsqrt(ms + EPS) * gffn_ref[...]).astype(BF16)
    xn2_ref[0] = xn2
    qt = lax.dot_general(wpqt_ref[...], xn2, (((1,), (1,)), ((), ())), preferred_element_type=F32)
    half = PEER_QDIM // 2
    cand_ref[...] = jnp.full(cand_ref.shape, NEG_INF, F32)
    for h in range(PEER_HEADS):
        q1 = qt[h * PEER_QDIM:h * PEER_QDIM + half].astype(BF16)
        q2 = qt[h * PEER_QDIM + half:(h + 1) * PEER_QDIM].astype(BF16)
        s1 = jnp.dot(k1_ref[h], q1, preferred_element_type=F32)
        s2 = jnp.dot(k2_ref[h], q2, preferred_element_type=F32)
        for sl in range(tt // LANES):
            lanes = slice(sl * LANES, (sl + 1) * LANES)
            a1 = s1[:, lanes]
            a2 = s2[:, lanes]
            v1 = _top_values(a1, PEER_TOPK + 1)
            v2, rank2 = _top_values(a2, PEER_TOPK + 1, with_rank=True)
            for idx, (a, b) in enumerate(_CAND_PAIRS):
                cand_ref[idx:idx + 1, lanes] = v1[a] + v2[b]
            cs = _top_values(cand_ref[:, lanes], PEER_TOPK + 1)
            z = jnp.zeros_like(cs[0])
            for i in range(PEER_TOPK):
                z = z + jnp.exp(cs[i] - cs[0])
            thr = 0.5 * (cs[PEER_TOPK - 1] + cs[PEER_TOPK])
            thr1 = thr - a1
            n1 = jnp.zeros_like(a1)
            for b in range(PEER_TOPK // 2):
                n1 = n1 + jnp.where(v2[b] > thr1, 1.0, 0.0)
            n_best = jnp.zeros_like(thr)
            for b in range(PEER_TOPK // 2, PEER_TOPK):
                n_best = n_best + jnp.where(v2[b] > thr - v1[0], 1.0, 0.0)
            n1 = n1 + jnp.where(a1 == v1[0], n_best, 0.0)
            r2_ref[0, h, :, lanes] = rank2.astype(BF16)
            n1_ref[0, h, :, lanes] = _bf16_pair_bits(n1)
            e1_ref[0, h, :, lanes] = _bf16_pair_bits(jnp.exp(a1 - v1[0]) * (1.0 / z))
            e2_ref[0, h, :, lanes] = jnp.exp(a2 - v2[0]).astype(BF16)


def _mix(x, att, hf, hb, om, w):
    b, s, d = x.shape
    tt = MIX_TILE
    nt = s // tt
    tspec = lambda width: pl.BlockSpec((1, tt, width), lambda bi, ti: (bi, ti, 0))
    sspec = pl.BlockSpec((1, PEER_HEADS, PEER_NKEYS, tt), lambda bi, ti: (bi, 0, 0, ti))
    return pl.pallas_call(
        functools.partial(_mix_kernel, tt),
        grid=(b, nt),
        in_specs=[tspec(d), tspec(ATT_WIDTH), tspec(ML_WIDTH), tspec(ML_WIDTH), tspec(ML_WIDTH),
                  _const_spec((1, ML_DIM)), _const_spec((d, d)), _const_spec((1, d)),
                  _const_spec((PEER_HEADS * PEER_QDIM, d)),
                  _const_spec((PEER_HEADS, PEER_NKEYS, PEER_QDIM // 2)),
                  _const_spec((PEER_HEADS, PEER_NKEYS, PEER_QDIM // 2))],
        out_specs=[tspec(d), tspec(d), sspec, sspec, sspec, sspec],
        out_shape=[jax.ShapeDtypeStruct((b, s, d), F32), jax.ShapeDtypeStruct((b, s, d), BF16),
                   jax.ShapeDtypeStruct((b, PEER_HEADS, PEER_NKEYS, s), BF16),
                   jax.ShapeDtypeStruct((b, PEER_HEADS, PEER_NKEYS, s), BF16),
                   jax.ShapeDtypeStruct((b, PEER_HEADS, PEER_NKEYS, s), jnp.uint32),
                   jax.ShapeDtypeStruct((b, PEER_HEADS, PEER_NKEYS, s), jnp.uint32)],
        scratch_shapes=[pltpu.VMEM((tt, d), BF16), pltpu.VMEM((_CAND_ROWS, tt), F32)],
        compiler_params=_cparams(("parallel", "parallel")),
        name="mix_route",
    )(x, att, hf, hb, om, w["g_mh"], w["w_out"], w["g_ffn"], w["w_pqt"], w["sub_k1"], w["sub_k2"])


def _gelu(x):
    return 0.5 * x * (1.0 + lax.erf(x * (2.0 ** -0.5)))


def _peer_kernel(tt, ng, xn2_ref, h1_ref, r2_ref, e2_ref, n1_ref, e1_ref, u_ref, vt_ref, y_ref,
                 a0_ref, a1_ref, w_ref, acc_ref):
    s = pl.program_id(1)
    rows = PEER_GROUP // PEER_NKEYS

    @pl.when(s == 0)
    def _():
        a0_ref[...] = jnp.zeros(a0_ref.shape, F32)
        a1_ref[...] = jnp.zeros(a1_ref.shape, F32)

    @pl.when((s < 1) | ((s + ng - 1) % ng == 0))
    def _():
        acc_ref[...] = jnp.zeros(acc_ref.shape, F32)

    g2 = (s + ng - 1) % ng

    def row(ref, h, c):
        words = jnp.broadcast_to(ref[0, h, pl.ds(c, 1), :], (SUBLANES, tt))
        tile = pltpu.bitcast(words, BF16)
        return jnp.concatenate([tile] * (PEER_PASS_ROWS // (2 * SUBLANES)), axis=0)

    def stages(a_cur, a_oth):
        a_cur[...] = lax.dot_general(u_ref[...], xn2_ref[0], (((1,), (1,)), ((), ())),
                                     preferred_element_type=F32)
        zero = jnp.zeros((PEER_PASS_ROWS, tt), BF16)
        for kb in range(PEER_GROUP // PEER_VBLOCK):
            for cc in range(kb * PEER_VBLOCK // PEER_NKEYS, (kb + 1) * PEER_VBLOCK // PEER_NKEYS):
                c = g2 * rows + cc
                for rb in range(PEER_NKEYS // PEER_PASS_ROWS):
                    i2 = slice(rb * PEER_PASS_ROWS, (rb + 1) * PEER_PASS_ROWS)
                    ex = slice(cc * PEER_NKEYS + rb * PEER_PASS_ROWS,
                               cc * PEER_NKEYS + (rb + 1) * PEER_PASS_ROWS)
                    wsum = zero
                    for h in range(PEER_HEADS):
                        sel = r2_ref[0, h, i2, :] < row(n1_ref, h, c)
                        wsum = wsum + jnp.where(sel, e2_ref[0, h, i2, :], zero) * row(e1_ref, h, c)
                    w_ref[ex, :] = wsum * _gelu(a_oth[ex, :]).astype(BF16)
            blk = slice(kb * PEER_VBLOCK, (kb + 1) * PEER_VBLOCK)
            acc_ref[...] += jnp.dot(vt_ref[0, :, blk], w_ref[blk, :], preferred_element_type=F32)

    @pl.when(s % 2 == 0)
    def _():
        stages(a0_ref, a1_ref)

    @pl.when(s % 2 == 1)
    def _():
        stages(a1_ref, a0_ref)

    @pl.when((s >= 1) & ((s + ng - 1) % ng == ng - 1))
    def _():
        y_ref[0] = h1_ref[0] + acc_ref[...].T


def _peer(xn2, h1, r2, e2, n1, e1, w):
    b, s, d = h1.shape
    tt = PEER_TILE
    nt = s // tt
    ng = (PEER_NKEYS * PEER_NKEYS) // PEER_GROUP
    tile = lambda si, lag: jnp.clip((si - lag) // ng, 0, nt - 1)
    tspec = lambda lag: pl.BlockSpec((1, tt, d), lambda bi, si: (bi, tile(si, lag), 0))
    sspec = pl.BlockSpec((1, PEER_HEADS, PEER_NKEYS, tt), lambda bi, si: (bi, 0, 0, tile(si, 1)))
    return pl.pallas_call(
        functools.partial(_peer_kernel, tt, ng),
        grid=(b, nt * ng + 1),
        in_specs=[tspec(0), tspec(1), sspec, sspec, sspec, sspec,
                  pl.BlockSpec((PEER_GROUP, d), lambda bi, si: (si % ng, 0)),
                  pl.BlockSpec((1, d, PEER_GROUP), lambda bi, si: ((si + ng - 1) % ng, 0, 0))],
        out_specs=tspec(1),
        out_shape=jax.ShapeDtypeStruct((b, s, d), F32),
        scratch_shapes=[pltpu.VMEM((PEER_GROUP, tt), F32)] * 2 + [pltpu.VMEM((PEER_GROUP, tt), BF16)]
        + [pltpu.VMEM((d, tt), F32)],
        compiler_params=_cparams(("parallel", "arbitrary")),
        name="peer_dense",
    )(xn2, h1, r2, e2, n1, e1, w["peer_u"], w["peer_vt"])


def _prep_weights(g_mix, w_in, b_gates, conv_w, conv_b, g_qn, g_kn, lam_q1, lam_k1, lam_q2, lam_k2,
                  g_sub, g_mh, w_out, g_ffn, w_pq, sub_k1, sub_k2, peer_u, peer_v):
    half = ROT_DIM // 2
    inv = ROPE_THETA ** (-jnp.arange(half, dtype=F32) * 2.0 / ROT_DIM)
    w_g = w_in[0][:, N_MAIN:]
    return {
        "g_mix": g_mix[0][None, :],
        "w_main": w_in[0][:, :N_MAIN].astype(BF16),
        "w_g": w_g.astype(BF16),
        "w_gt": w_g.T.astype(BF16),
        "b_g": b_gates[0][None, :],
        "b_gt": b_gates[0][:, None],
        "g_qk": jnp.concatenate([jnp.tile(g_qn[0], ATT_WIDTH // ATT_QK_DIM),
                                 jnp.tile(g_kn[0], ATT_WIDTH // ATT_QK_DIM)])[None, :],
        "inv_lane": jnp.tile(inv, LANES // half)[None, :],
        "conv_w": conv_w[0][:, 0, :],
        "conv_b": conv_b[0][None, :],
        "lam4": jnp.stack([lam_q1[0], lam_k1[0], lam_q2[0], lam_k2[0]]),
        "g_sub": g_sub[0][None, :],
        "g_mh": g_mh[0][None, :],
        "w_out": w_out[0].astype(BF16),
        "g_ffn": g_ffn[0][None, :],
        "w_pqt": w_pq[0].T.astype(BF16),
        "sub_k1": sub_k1[0].astype(BF16),
        "sub_k2": sub_k2[0].astype(BF16),
        "peer_u": peer_u[0].astype(BF16),
        "peer_vt": peer_v[0].astype(BF16).reshape(-1, PEER_GROUP, D_MODEL).transpose(0, 2, 1),
    }


def _trunk(x, meta, w):
    b, s, d = x.shape
    tt = PROJ_TILE
    assert s % tt == 0 and s % (2 * ATT_TK) == 0 and s % MIX_TILE == 0 and s % PEER_TILE == 0
    nt = s // tt
    xr = x.reshape(b, nt, tt, d)
    meta_tail = jnp.broadcast_to(meta[N_META - HALO:][None, None], (b, 1, HALO, d))
    prev = jnp.concatenate([meta_tail, xr[:, :-1, tt - HALO:, :]], axis=1)
    nxt = jnp.concatenate([xr[:, 1:, :HALO, :], jnp.zeros((b, 1, HALO, d), x.dtype)], axis=1)
    q, k, v, qm, km, vm, om, g_tm, g_fm = _proj(x, prev, nxt, N_META, tt, w)

    meta_b = jnp.broadcast_to(meta[None], (b, N_META, d))
    m_prev = jnp.zeros((b, 1, HALO, d), x.dtype)
    m_next = x[:, None, :HALO, :]
    _, k_m, v_m, qm_m, km_m, vm_m, _, gtm_m, gfm_m = _proj(meta_b, m_prev, m_next, 0, N_META, w)

    pad = CHUNK - N_META
    att = _attn(q, k, v, jnp.pad(k_m, ((0, 0), (0, pad), (0, 0))), jnp.pad(v_m, ((0, 0), (0, pad), (0, 0))),
                w["lam4"], w["g_sub"])

    front = lambda a: jnp.pad(a, ((0, 0), (pad, 0), (0, 0)))
    is_input_gate = (jnp.arange(N_GATES) // ML_HEADS) % 2 == 0
    pad_gate = jnp.where(is_input_gate, NEG_INF, 0.0).astype(F32)
    gtm_p = jnp.concatenate([jnp.broadcast_to(pad_gate[None, None, :], (b, pad, N_GATES)), gtm_m], axis=1)
    gfm_p = jnp.concatenate([jnp.broadcast_to(pad_gate[None, :, None], (b, N_GATES, pad)), gfm_m], axis=2)
    s_zero = jnp.zeros((b, ML_HEADS, ML_DIM, 2 * LANES), F32)
    m_zero = jnp.zeros((b, SUBLANES, LANES), F32)
    _, _, s0, m0 = _mlstm(front(qm_m), front(km_m), front(vm_m), gtm_p, gfm_p, s_zero, m_zero)
    hf, hb, _, _ = _mlstm(qm, km, vm, g_tm, g_fm, s0, m0)

    h1, xn2, r2, e2, n1, e1 = _mix(x, att, hf, hb, om, w)
    return _peer(xn2, h1, r2, e2, n1, e1, w)


def kernel(x_prompt, x_sample, meta, g_mix, w_in, b_gates, conv_w, conv_b, g_qn, g_kn, lam_q1, lam_k1,
           lam_q2, lam_k2, g_sub, g_mh, w_out, g_ffn, w_pq, sub_k1, sub_k2, peer_u, peer_v):
    w = _prep_weights(g_mix, w_in, b_gates, conv_w, conv_b, g_qn, g_kn, lam_q1, lam_k1, lam_q2, lam_k2,
                      g_sub, g_mh, w_out, g_ffn, w_pq, sub_k1, sub_k2, peer_u, peer_v)
    return (_trunk(x_prompt, meta, w), _trunk(x_sample, meta, w))
```

```python
import functools
import math

import jax
import jax.numpy as jnp
from jax import lax
from jax.experimental import pallas as pl
from jax.experimental.pallas import tpu as pltpu

F32 = jnp.float32
BF16 = jnp.bfloat16

D_MODEL = 1024
N_META = 16
ATT_HEADS = 4
ATT_QK_DIM = 64
ATT_V_DIM = 128
ATT_WIDTH = ATT_HEADS * ATT_V_DIM
ML_HEADS = 4
ML_DIM = 128
ML_WIDTH = ML_HEADS * ML_DIM
N_GATES = 4 * ML_HEADS
N_MAIN = 3 * ATT_WIDTH + 4 * ML_WIDTH
ROT_DIM = ATT_QK_DIM // 4
ROPE_THETA = 500000.0
CHUNK = 128
PEER_HEADS = 8
PEER_NKEYS = 128
PEER_QDIM = 256
PEER_TOPK = 16
EPS = 1e-6
LAM_INIT = 0.8 - 0.6 * math.exp(-0.3 * 0)
Q_SCALE = ATT_QK_DIM ** -0.5 * math.log2(math.e)

LANES = 128
SUBLANES = 8
HALO = SUBLANES
VMEM_LIMIT = 56 * 1024 * 1024

PROJ_TILE = 512
ATT_TQ = 512
ATT_TK = 1024
MIX_TILE = 512
PEER_TILE = 512
PEER_GROUP = 2048
PEER_PASS_LANES = 256
PEER_VBLOCK = 256
NEG_INF = float("-inf")


def _cparams(sem):
    return pltpu.CompilerParams(dimension_semantics=sem, vmem_limit_bytes=VMEM_LIMIT)


def _const_spec(shape):
    nd = len(shape)
    return pl.BlockSpec(shape, lambda *_: (0,) * nd)


def _log_sigmoid(x):
    return -(jnp.maximum(-x, 0.0) + jnp.log1p(jnp.exp(-jnp.abs(x))))


def _proj_kernel(pos_base, tt,
                 x_ref, prev_ref, next_ref, gmix_ref, wmain_ref, wg_ref, wgt_ref, bg_ref, bgt_ref,
                 gqk_ref, inv_ref, convw_ref, convb_ref,
                 q_ref, k_ref, v_ref, qm_ref, km_ref, vm_ref, om_ref, gtm_ref, gfm_ref,
                 xe_ref, zc_ref):
    i = pl.program_id(1)
    xe_ref[0:HALO, :] = prev_ref[0, 0]
    xe_ref[HALO:HALO + tt, :] = x_ref[0]
    xe_ref[HALO + tt:2 * HALO + tt, :] = next_ref[0, 0]
    xe = xe_ref[...]
    ms = jnp.mean(xe * xe, axis=-1, keepdims=True)
    xn_ext = (xe * lax.rsqrt(ms + EPS) * gmix_ref[...]).astype(BF16)
    xn = xn_ext[HALO:HALO + tt]

    def mm(a, lo, hi):
        return jnp.dot(a, wmain_ref[:, lo:hi], preferred_element_type=F32)

    zqk = mm(xn, 0, 2 * ATT_WIDTH)
    sq = (zqk * zqk).astype(BF16)
    r = lax.broadcasted_iota(jnp.int32, (2 * LANES, 2 * LANES), 0) // ATT_QK_DIM
    c = lax.broadcasted_iota(jnp.int32, (2 * LANES, 2 * LANES), 1) // ATT_QK_DIM
    seg_ones = (r == c).astype(BF16)
    lane = lax.broadcasted_iota(jnp.int32, (tt, LANES), 1)
    l64 = lane % ATT_QK_DIM
    row = lax.broadcasted_iota(jnp.int32, (tt, LANES), 0)
    pos = (pos_base + i * tt + row).astype(F32)
    ang = pos * inv_ref[...]
    cos_a = jnp.cos(ang)
    sin_a = jnp.sin(ang)
    half = ROT_DIM // 2
    cos_m = jnp.where(l64 < ROT_DIM, cos_a, 1.0)
    sin_lo = jnp.where(l64 < half, -sin_a, 0.0)
    sin_hi = jnp.where((l64 >= half) & (l64 < ROT_DIM), sin_a, 0.0)
    for cb in range(4):
        lo = cb * 2 * LANES
        ss = jnp.dot(sq[:, lo:lo + 2 * LANES], seg_ones, preferred_element_type=F32)
        rn = lax.rsqrt(ss * (1.0 / ATT_QK_DIM) + EPS)
        qn = zqk[:, lo:lo + 2 * LANES] * rn * gqk_ref[:, lo:lo + 2 * LANES]
        for sb in range(2):
            xs = qn[:, sb * LANES:(sb + 1) * LANES]
            rot = (xs * cos_m + pltpu.roll(xs, LANES - half, 1) * sin_lo
                   + pltpu.roll(xs, half, 1) * sin_hi)
            col = lo + sb * LANES
            if col < ATT_WIDTH:
                q_ref[0, :, col:col + LANES] = (rot * Q_SCALE).astype(BF16)
            else:
                k_ref[0, :, col - ATT_WIDTH:col - ATT_WIDTH + LANES] = rot.astype(BF16)

    va = mm(xn, 2 * ATT_WIDTH, 3 * ATT_WIDTH).astype(BF16)
    for h in range(ATT_HEADS):
        v_ref[0, :, 2 * h * LANES:(2 * h + 1) * LANES] = va[:, h * ATT_V_DIM:(h + 1) * ATT_V_DIM]
        v_ref[0, :, (2 * h + 1) * LANES:(2 * h + 2) * LANES] = jnp.ones((tt, LANES), BF16)

    base = 3 * ATT_WIDTH
    zc_ref[...] = mm(xn_ext, base, base + 2 * ML_WIDTH)
    y = (convw_ref[0:1, :] * zc_ref[pl.ds(HALO - 1, tt), :]
         + convw_ref[1:2, :] * zc_ref[pl.ds(HALO, tt), :]
         + convw_ref[2:3, :] * zc_ref[pl.ds(HALO + 1, tt), :]
         + convb_ref[...])
    act = y * (1.0 / (1.0 + jnp.exp(-y)))
    qm_ref[0] = act[:, :ML_WIDTH].astype(BF16)
    km_ref[0] = (act[:, ML_WIDTH:] * (ML_DIM ** -0.5)).astype(BF16)
    vm_ref[0] = mm(xn, base + 2 * ML_WIDTH, base + 3 * ML_WIDTH).astype(BF16)
    om_ref[0] = mm(xn, base + 3 * ML_WIDTH, base + 4 * ML_WIDTH).astype(BF16)

    g_tm = jnp.dot(xn, wg_ref[...], preferred_element_type=F32) + bg_ref[...]
    g_fm = lax.dot_general(wgt_ref[...], xn, (((1,), (1,)), ((), ())),
                           preferred_element_type=F32) + bgt_ref[...]
    col_tm = lax.broadcasted_iota(jnp.int32, (tt, N_GATES), 1)
    row_fm = lax.broadcasted_iota(jnp.int32, (N_GATES, tt), 0)
    gtm_ref[0] = jnp.where((col_tm // ML_HEADS) % 2 == 1, _log_sigmoid(g_tm), g_tm)
    gfm_ref[0] = jnp.where((row_fm // ML_HEADS) % 2 == 1, _log_sigmoid(g_fm), g_fm)


def _proj(x, prev, nxt, pos_base, tt, w):
    b, s, d = x.shape
    nt = s // tt
    tok = lambda width, dt: jax.ShapeDtypeStruct((b, s, width), dt)
    out_shape = ([tok(ATT_WIDTH, BF16)] * 2 + [tok(2 * ATT_WIDTH, BF16)] + [tok(ML_WIDTH, BF16)] * 4
                 + [tok(N_GATES, F32), jax.ShapeDtypeStruct((b, N_GATES, s), F32)])
    tspec = lambda width: pl.BlockSpec((1, tt, width), lambda bi, ti: (bi, ti, 0))
    hspec = pl.BlockSpec((1, 1, HALO, d), lambda bi, ti: (bi, ti, 0, 0))
    return pl.pallas_call(
        functools.partial(_proj_kernel, pos_base, tt),
        grid=(b, nt),
        in_specs=[tspec(d), hspec, hspec, _const_spec((1, d)), _const_spec((d, N_MAIN)),
                  _const_spec((d, N_GATES)), _const_spec((N_GATES, d)), _const_spec((1, N_GATES)),
                  _const_spec((N_GATES, 1)), _const_spec((1, 2 * ATT_WIDTH)), _const_spec((1, LANES)),
                  _const_spec((3, 2 * ML_WIDTH)), _const_spec((1, 2 * ML_WIDTH))],
        out_specs=[tspec(ATT_WIDTH)] * 2 + [tspec(2 * ATT_WIDTH)] + [tspec(ML_WIDTH)] * 4
        + [tspec(N_GATES), pl.BlockSpec((1, N_GATES, tt), lambda bi, ti: (bi, 0, ti))],
        out_shape=out_shape,
        scratch_shapes=[pltpu.VMEM((tt + 2 * HALO, d), F32), pltpu.VMEM((tt + 2 * HALO, 2 * ML_WIDTH), F32)],
        compiler_params=_cparams(("parallel", "parallel")),
        name=f"proj_t{tt}",
    )(x, prev, nxt, w["g_mix"], w["w_main"], w["w_g"], w["w_gt"], w["b_g"], w["b_gt"],
      w["g_qk"], w["inv_lane"], w["conv_w"], w["conv_b"])


def _attn_kernel(tq, tk, nk,
                 q_ref, k_ref, v_ref, kt_ref, vt_ref, lam_ref, gsub_ref, o_ref,
                 qq_ref, m_ref, acc_ref, sa_ref, sb_ref):
    q = q_ref[0]
    lane = lax.broadcasted_iota(jnp.int32, (tq, LANES), 1)
    zero = jnp.zeros_like(q)
    qq_ref[0:tq, :] = jnp.where(lane < ATT_QK_DIM, q, zero)
    qq_ref[tq:2 * tq, :] = jnp.where(lane >= ATT_QK_DIM, q, zero)
    m_ref[...] = jnp.full(m_ref.shape, NEG_INF, F32)
    acc_ref[...] = jnp.zeros(acc_ref.shape, F32)

    def scores(kb):
        return lax.dot_general(qq_ref[...], kb, (((1,), (1,)), ((), ())), preferred_element_type=F32)

    def update(s, vb):
        m_prev = m_ref[...]
        m_new = jnp.maximum(m_prev, jnp.max(s, axis=1, keepdims=True))
        alpha = jnp.exp2(m_prev - m_new)
        p = jnp.exp2(s - m_new[:, 0:1]).astype(BF16)
        pv = jnp.dot(p, vb, preferred_element_type=F32)
        acc_ref[:, 0:LANES] = alpha * acc_ref[:, 0:LANES] + pv[:, 0:LANES]
        acc_ref[:, LANES:2 * LANES] = alpha * acc_ref[:, LANES:2 * LANES] + pv[:, LANES:2 * LANES]
        m_ref[...] = m_new

    def kblock(j):
        return k_ref[0, pl.ds(pl.multiple_of(j * tk, tk), tk), :]

    def vblock(j):
        return v_ref[0, pl.ds(pl.multiple_of(j * tk, tk), tk), :]

    sa_ref[...] = scores(kblock(0))

    def body(i, carry):
        j = 2 * i
        sb_ref[...] = scores(kblock(j + 1))
        update(sa_ref[...], vblock(j))
        sa_ref[...] = scores(kblock(j + 2))
        update(sb_ref[...], vblock(j + 1))
        return carry

    lax.fori_loop(0, nk // 2 - 1, body, 0)
    sb_ref[...] = scores(kblock(nk - 1))
    update(sa_ref[...], vblock(nk - 2))
    col = lax.broadcasted_iota(jnp.int32, (2 * tq, LANES), 1)
    s_meta = jnp.where(col < N_META, scores(kt_ref[0]), NEG_INF)
    update(sb_ref[...], vblock(nk - 1))
    update(s_meta, vt_ref[0])

    o1 = acc_ref[0:tq, 0:LANES] / acc_ref[0:tq, LANES:2 * LANES]
    o2 = acc_ref[tq:2 * tq, 0:LANES] / acc_ref[tq:2 * tq, LANES:2 * LANES]
    lam = (jnp.exp(jnp.sum(lam_ref[0:1, :] * lam_ref[1:2, :], axis=1, keepdims=True))
           - jnp.exp(jnp.sum(lam_ref[2:3, :] * lam_ref[3:4, :], axis=1, keepdims=True)) + LAM_INIT)
    o = o1 - lam * o2
    ms = jnp.mean(o * o, axis=-1, keepdims=True)
    o_ref[0] = (o * lax.rsqrt(ms + EPS) * gsub_ref[...] * (1.0 - LAM_INIT)).astype(BF16)


def _attn(q, k, v, kt, vt, lam4, g_sub):
    b, s, _ = q.shape
    tq, tk = ATT_TQ, ATT_TK
    kspec = lambda rows: pl.BlockSpec((1, rows, LANES), lambda bi, hi, qi: (bi, 0, hi))
    vspec = lambda rows: pl.BlockSpec((1, rows, 2 * LANES), lambda bi, hi, qi: (bi, 0, hi))
    return pl.pallas_call(
        functools.partial(_attn_kernel, tq, tk, s // tk),
        grid=(b, ATT_HEADS, s // tq),
        in_specs=[pl.BlockSpec((1, tq, LANES), lambda bi, hi, qi: (bi, qi, hi)),
                  kspec(s), vspec(s), kspec(LANES), vspec(LANES),
                  _const_spec((4, ATT_QK_DIM)), _const_spec((1, ATT_V_DIM))],
        out_specs=pl.BlockSpec((1, tq, LANES), lambda bi, hi, qi: (bi, qi, hi)),
        out_shape=jax.ShapeDtypeStruct((b, s, ATT_WIDTH), BF16),
        scratch_shapes=[pltpu.VMEM((2 * tq, LANES), BF16), pltpu.VMEM((2 * tq, LANES), F32),
                        pltpu.VMEM((2 * tq, 2 * LANES), F32),
                        pltpu.VMEM((2 * tq, tk), F32), pltpu.VMEM((2 * tq, tk), F32)],
        compiler_params=_cparams(("parallel", "parallel", "parallel")),
        name="diff_attn",
    )(q, k, v, kt, vt, lam4, g_sub)


def _mlstm_chunk(reverse, q, k, v, g_tm, g_fm, s_ref, m_ref, h_ref):
    r = lax.broadcasted_iota(jnp.int32, (CHUNK, CHUNK), 0)
    c = lax.broadcasted_iota(jnp.int32, (CHUNK, CHUNK), 1)
    if reverse:
        causal = c >= r
        last = 0
        gbase = 2 * ML_HEADS
    else:
        causal = c <= r
        last = CHUNK - 1
        gbase = 0
    b_cols = jnp.dot(causal.astype(F32), g_tm, preferred_element_type=F32,
                     precision=lax.Precision.HIGHEST)
    b_rows = jnp.dot(g_fm, (r <= c if not reverse else r >= c).astype(F32),
                     preferred_element_type=F32, precision=lax.Precision.HIGHEST)
    ones_col = (c == 0).astype(BF16)
    for h in range(ML_HEADS):
        ic, fc = gbase + h, gbase + ML_HEADS + h
        b_col = b_cols[:, fc:fc + 1]
        b_row = b_rows[fc:fc + 1, :]
        li_row = g_fm[ic:ic + 1, :]
        li_col = g_tm[:, ic:ic + 1]
        m_prev = m_ref[h:h + 1, 0:1]
        log_d = jnp.where(causal, b_col - b_row + li_row, NEG_INF)
        m_inter = b_col + m_prev
        mj = jnp.maximum(m_inter, jnp.max(log_d, axis=1, keepdims=True))
        qh = q[:, h * ML_DIM:(h + 1) * ML_DIM]
        kh = k[:, h * ML_DIM:(h + 1) * ML_DIM]
        vext = jnp.concatenate([v[:, h * ML_DIM:(h + 1) * ML_DIM], ones_col], axis=1)
        qk = lax.dot_general(qh, kh, (((1,), (1,)), ((), ())), preferred_element_type=F32)
        w = (qk * jnp.exp(log_d - mj)).astype(BF16)
        state = s_ref[h]
        tot = (jnp.exp(m_inter - mj) * jnp.dot(qh, state.astype(BF16), preferred_element_type=F32)
               + jnp.dot(w, vext, preferred_element_type=F32))
        den = tot[:, ML_DIM:ML_DIM + 1]
        h_ref[0, :, h * ML_DIM:(h + 1) * ML_DIM] = (
            tot[:, :ML_DIM] / jnp.maximum(jnp.abs(den), jnp.exp(-mj)))
        b_last = b_col[last:last + 1, :]
        logw_col = b_last - b_col + li_col
        logw_row = b_last - b_row + li_row
        m_new = jnp.maximum(b_last + m_prev, jnp.max(logw_row, axis=1, keepdims=True))
        kw = (kh.astype(F32) * jnp.exp(logw_col - m_new)).astype(BF16)
        s_ref[h] = (jnp.exp(b_last + m_prev - m_new) * state
                    + lax.dot_general(kw, vext, (((0,), (0,)), ((), ())), preferred_element_type=F32))
        m_ref[h:h + 1, :] = jnp.broadcast_to(m_new, (1, LANES))


def _mlstm_kernel(qf_ref, kf_ref, vf_ref, gtf_ref, gff_ref, qb_ref, kb_ref, vb_ref, gtb_ref, gfb_ref,
                  s0_ref, m0_ref, hf_ref, hb_ref, sfin_ref, mfin_ref,
                  sf_ref, sb_ref, mf_ref, mb_ref):
    j = pl.program_id(1)

    @pl.when(j == 0)
    def _():
        sf_ref[...] = s0_ref[0]
        mf_ref[...] = m0_ref[0]
        sb_ref[...] = jnp.zeros(sb_ref.shape, F32)
        mb_ref[...] = jnp.zeros(mb_ref.shape, F32)

    _mlstm_chunk(False, qf_ref[0], kf_ref[0], vf_ref[0], gtf_ref[0], gff_ref[0], sf_ref, mf_ref, hf_ref)
    _mlstm_chunk(True, qb_ref[0], kb_ref[0], vb_ref[0], gtb_ref[0], gfb_ref[0], sb_ref, mb_ref, hb_ref)

    @pl.when(j == pl.num_programs(1) - 1)
    def _():
        sfin_ref[0] = sf_ref[...]
        mfin_ref[0] = mf_ref[...]


def _mlstm(qm, km, vm, g_tm, g_fm, s0, m0):
    b, s, _ = qm.shape
    nc = s // CHUNK
    fwd = lambda width: pl.BlockSpec((1, CHUNK, width), lambda bi, ci: (bi, ci, 0))
    bwd = lambda width: pl.BlockSpec((1, CHUNK, width), lambda bi, ci: (bi, nc - 1 - ci, 0))
    gf_fwd = pl.BlockSpec((1, N_GATES, CHUNK), lambda bi, ci: (bi, 0, ci))
    gf_bwd = pl.BlockSpec((1, N_GATES, CHUNK), lambda bi, ci: (bi, 0, nc - 1 - ci))
    st = pl.BlockSpec((1, ML_HEADS, ML_DIM, 2 * LANES), lambda bi, ci: (bi, 0, 0, 0))
    mx = pl.BlockSpec((1, SUBLANES, LANES), lambda bi, ci: (bi, 0, 0))
    return pl.pallas_call(
        _mlstm_kernel,
        grid=(b, nc),
        in_specs=[fwd(ML_WIDTH)] * 3 + [fwd(N_GATES), gf_fwd] + [bwd(ML_WIDTH)] * 3 + [bwd(N_GATES), gf_bwd]
        + [st, mx],
        out_specs=[fwd(ML_WIDTH), bwd(ML_WIDTH), st, mx],
        out_shape=[jax.ShapeDtypeStruct((b, s, ML_WIDTH), F32)] * 2
        + [jax.ShapeDtypeStruct((b, ML_HEADS, ML_DIM, 2 * LANES), F32),
           jax.ShapeDtypeStruct((b, SUBLANES, LANES), F32)],
        scratch_shapes=[pltpu.VMEM((ML_HEADS, ML_DIM, 2 * LANES), F32)] * 2
        + [pltpu.VMEM((SUBLANES, LANES), F32)] * 2,
        compiler_params=_cparams(("parallel", "arbitrary")),
        name="mlstm",
    )(qm, km, vm, g_tm, g_fm, qm, km, vm, g_tm, g_fm, s0, m0)


def _top_values(x, n, with_rank=False):
    vals = []
    cur = x
    rank = jnp.full(x.shape, float(n), F32) if with_rank else None
    for i in range(n):
        mx = jnp.max(cur, axis=0, keepdims=True)
        vals.append(mx)
        hit = cur == mx
        if with_rank:
            rank = jnp.where(hit, float(i), rank)
        if i + 1 < n:
            cur = jnp.where(hit, NEG_INF, cur)
    return (vals, rank) if with_rank else vals


def _bf16_pair_bits(x):
    u = lax.bitcast_convert_type(x.astype(BF16).astype(F32), jnp.uint32)
    return u | (u >> 16)


_CAND_PAIRS = [(a, b) for a in range(PEER_TOPK + 1) for b in range(PEER_TOPK + 1)
               if (a + 1) * (b + 1) <= PEER_TOPK + 1]
_CAND_ROWS = -(-len(_CAND_PAIRS) // SUBLANES) * SUBLANES


def _mix_kernel(tt, x_ref, att_ref, hf_ref, hb_ref, om_ref, gmh_ref, wout_ref, gffn_ref, wpqt_ref,
                k1_ref, k2_ref, h1_ref, xn2_ref, r2_ref, e2_ref, n1_ref, e1_ref, mix_ref, cand_ref):
    mix_ref[:, 0:ATT_WIDTH] = att_ref[0]
    for h in range(ML_HEADS):
        sl = slice(h * ML_DIM, (h + 1) * ML_DIM)
        hs = hf_ref[0, :, sl] + hb_ref[0, :, sl]
        ms = jnp.mean(hs * hs, axis=-1, keepdims=True)
        hn = hs * lax.rsqrt(ms + EPS) * gmh_ref[...]
        gate = 1.0 / (1.0 + jnp.exp(-om_ref[0, :, sl].astype(F32)))
        mix_ref[:, ATT_WIDTH + h * ML_DIM:ATT_WIDTH + (h + 1) * ML_DIM] = (hn * gate).astype(BF16)
    h1 = x_ref[0] + jnp.dot(mix_ref[...], wout_ref[...], preferred_element_type=F32)
    h1_ref[0] = h1
    ms = jnp.mean(h1 * h1, axis=-1, keepdims=True)
    xn2 = (h1 * lax.rsqrt(ms + EPS) * gffn_ref[...]).astype(BF16)
    xn2_ref[0] = xn2
    qt = lax.dot_general(wpqt_ref[...], xn2, (((1,), (1,)), ((), ())), preferred_element_type=F32)
    half = PEER_QDIM // 2
    cand_ref[...] = jnp.full(cand_ref.shape, NEG_INF, F32)
    for h in range(PEER_HEADS):
        q1 = qt[h * PEER_QDIM:h * PEER_QDIM + half].astype(BF16)
        q2 = qt[h * PEER_QDIM + half:(h + 1) * PEER_QDIM].astype(BF16)
        s1 = jnp.dot(k1_ref[h], q1, preferred_element_type=F32)
        s2 = jnp.dot(k2_ref[h], q2, preferred_element_type=F32)
        for sl in range(tt // LANES):
            lanes = slice(sl * LANES, (sl + 1) * LANES)
            a1 = s1[:, lanes]
            a2 = s2[:, lanes]
            v1 = _top_values(a1, PEER_TOPK + 1)
            v2, rank2 = _top_values(a2, PEER_TOPK + 1, with_rank=True)
            for idx, (a, b) in enumerate(_CAND_PAIRS):
                cand_ref[idx:idx + 1, lanes] = v1[a] + v2[b]
            cs = _top_values(cand_ref[:, lanes], PEER_TOPK + 1)
            z = jnp.zeros_like(cs[0])
            for i in range(PEER_TOPK):
                z = z + jnp.exp(cs[i] - cs[0])
            thr = 0.5 * (cs[PEER_TOPK - 1] + cs[PEER_TOPK])
            thr1 = thr - a1
            n1 = jnp.zeros_like(a1)
            for b in range(PEER_TOPK // 2):
                n1 = n1 + jnp.where(v2[b] > thr1, 1.0, 0.0)
            n_best = jnp.zeros_like(thr)
            for b in range(PEER_TOPK // 2, PEER_TOPK):
                n_best = n_best + jnp.where(v2[b] > thr - v1[0], 1.0, 0.0)
            n1 = n1 + jnp.where(a1 == v1[0], n_best, 0.0)
            r2_ref[0, h, :, lanes] = rank2.astype(BF16)
            n1_ref[0, h, :, lanes] = _bf16_pair_bits(n1)
            e1_ref[0, h, :, lanes] = _bf16_pair_bits(jnp.exp(a1 - v1[0]) * (0.5 / z))
            e2_ref[0, h, :, lanes] = jnp.exp(a2 - v2[0]).astype(BF16)


def _mix(x, att, hf, hb, om, w):
    b, s, d = x.shape
    tt = MIX_TILE
    nt = s // tt
    tspec = lambda width: pl.BlockSpec((1, tt, width), lambda bi, ti: (bi, ti, 0))
    sspec = pl.BlockSpec((1, PEER_HEADS, PEER_NKEYS, tt), lambda bi, ti: (bi, 0, 0, ti))
    return pl.pallas_call(
        functools.partial(_mix_kernel, tt),
        grid=(b, nt),
        in_specs=[tspec(d), tspec(ATT_WIDTH), tspec(ML_WIDTH), tspec(ML_WIDTH), tspec(ML_WIDTH),
                  _const_spec((1, ML_DIM)), _const_spec((d, d)), _const_spec((1, d)),
                  _const_spec((PEER_HEADS * PEER_QDIM, d)),
                  _const_spec((PEER_HEADS, PEER_NKEYS, PEER_QDIM // 2)),
                  _const_spec((PEER_HEADS, PEER_NKEYS, PEER_QDIM // 2))],
        out_specs=[tspec(d), tspec(d), sspec, sspec, sspec, sspec],
        out_shape=[jax.ShapeDtypeStruct((b, s, d), F32), jax.ShapeDtypeStruct((b, s, d), BF16),
                   jax.ShapeDtypeStruct((b, PEER_HEADS, PEER_NKEYS, s), BF16),
                   jax.ShapeDtypeStruct((b, PEER_HEADS, PEER_NKEYS, s), BF16),
                   jax.ShapeDtypeStruct((b, PEER_HEADS, PEER_NKEYS, s), jnp.uint32),
                   jax.ShapeDtypeStruct((b, PEER_HEADS, PEER_NKEYS, s), jnp.uint32)],
        scratch_shapes=[pltpu.VMEM((tt, d), BF16), pltpu.VMEM((_CAND_ROWS, tt), F32)],
        compiler_params=_cparams(("parallel", "parallel")),
        name="mix_route",
    )(x, att, hf, hb, om, w["g_mh"], w["w_out"], w["g_ffn"], w["w_pqt"], w["sub_k1"], w["sub_k2"])


def _gelu_x2(x):
    return x + x * lax.erf(x * (2.0 ** -0.5))


def _peer_kernel(tt, ng, xn2_ref, h1_ref, r2_ref, e2_ref, n1_ref, e1_ref, u_ref, vt_ref, y_ref,
                 a0_ref, a1_ref, w_ref, acc_ref):
    s = pl.program_id(1)
    rows = PEER_GROUP // PEER_NKEYS

    @pl.when(s == 0)
    def _():
        a0_ref[...] = jnp.zeros(a0_ref.shape, F32)
        a1_ref[...] = jnp.zeros(a1_ref.shape, F32)

    @pl.when((s < 1) | ((s + ng - 1) % ng == 0))
    def _():
        acc_ref[...] = jnp.zeros(acc_ref.shape, F32)

    g2 = (s + ng - 1) % ng

    def row(ref, h, c, lanes):
        words = jnp.broadcast_to(ref[0, h, pl.ds(c, 1), lanes], (SUBLANES, PEER_PASS_LANES))
        tile = pltpu.bitcast(words, BF16)
        return jnp.concatenate([tile] * (PEER_NKEYS // (2 * SUBLANES)), axis=0)

    def stages(a_cur, a_oth):
        a_cur[...] = lax.dot_general(u_ref[...], xn2_ref[0], (((1,), (1,)), ((), ())),
                                     preferred_element_type=F32)
        zero = jnp.zeros((PEER_NKEYS, PEER_PASS_LANES), BF16)
        for kb in range(PEER_GROUP // PEER_VBLOCK):
            for cc in range(kb * PEER_VBLOCK // PEER_NKEYS, (kb + 1) * PEER_VBLOCK // PEER_NKEYS):
                c = g2 * rows + cc
                ex = slice(cc * PEER_NKEYS, (cc + 1) * PEER_NKEYS)
                for lb in range(tt // PEER_PASS_LANES):
                    lanes = pl.ds(lb * PEER_PASS_LANES, PEER_PASS_LANES)
                    wsum = zero
                    for h in range(PEER_HEADS):
                        sel = r2_ref[0, h, :, lanes] < row(n1_ref, h, c, lanes)
                        wsum = wsum + jnp.where(sel, e2_ref[0, h, :, lanes], zero) * row(e1_ref, h, c, lanes)
                    w_ref[ex, lanes] = wsum * _gelu_x2(a_oth[ex, lanes]).astype(BF16)
            blk = slice(kb * PEER_VBLOCK, (kb + 1) * PEER_VBLOCK)
            acc_ref[...] += jnp.dot(vt_ref[0, :, blk], w_ref[blk, :], preferred_element_type=F32)

    @pl.when(s % 2 == 0)
    def _():
        stages(a0_ref, a1_ref)

    @pl.when(s % 2 == 1)
    def _():
        stages(a1_ref, a0_ref)

    @pl.when((s >= 1) & ((s + ng - 1) % ng == ng - 1))
    def _():
        y_ref[0] = h1_ref[0] + acc_ref[...].T


def _peer(xn2, h1, r2, e2, n1, e1, w):
    b, s, d = h1.shape
    tt = PEER_TILE
    nt = s // tt
    ng = (PEER_NKEYS * PEER_NKEYS) // PEER_GROUP
    tile = lambda si, lag: jnp.clip((si - lag) // ng, 0, nt - 1)
    tspec = lambda lag: pl.BlockSpec((1, tt, d), lambda bi, si: (bi, tile(si, lag), 0))
    sspec = pl.BlockSpec((1, PEER_HEADS, PEER_NKEYS, tt), lambda bi, si: (bi, 0, 0, tile(si, 1)))
    return pl.pallas_call(
        functools.partial(_peer_kernel, tt, ng),
        grid=(b, nt * ng + 1),
        in_specs=[tspec(0), tspec(1), sspec, sspec, sspec, sspec,
                  pl.BlockSpec((PEER_GROUP, d), lambda bi, si: (si % ng, 0)),
                  pl.BlockSpec((1, d, PEER_GROUP), lambda bi, si: ((si + ng - 1) % ng, 0, 0))],
        out_specs=tspec(1),
        out_shape=jax.ShapeDtypeStruct((b, s, d), F32),
        scratch_shapes=[pltpu.VMEM((PEER_GROUP, tt), F32)] * 2 + [pltpu.VMEM((PEER_GROUP, tt), BF16)]
        + [pltpu.VMEM((d, tt), F32)],
        compiler_params=_cparams(("parallel", "arbitrary")),
        name="peer_dense",
    )(xn2, h1, r2, e2, n1, e1, w["peer_u"], w["peer_vt"])


def _prep_weights(g_mix, w_in, b_gates, conv_w, conv_b, g_qn, g_kn, lam_q1, lam_k1, lam_q2, lam_k2,
                  g_sub, g_mh, w_out, g_ffn, w_pq, sub_k1, sub_k2, peer_u, peer_v):
    half = ROT_DIM // 2
    inv = ROPE_THETA ** (-jnp.arange(half, dtype=F32) * 2.0 / ROT_DIM)
    w_g = w_in[0][:, N_MAIN:]
    return {
        "g_mix": g_mix[0][None, :],
        "w_main": w_in[0][:, :N_MAIN].astype(BF16),
        "w_g": w_g.astype(BF16),
        "w_gt": w_g.T.astype(BF16),
        "b_g": b_gates[0][None, :],
        "b_gt": b_gates[0][:, None],
        "g_qk": jnp.concatenate([jnp.tile(g_qn[0], ATT_WIDTH // ATT_QK_DIM),
                                 jnp.tile(g_kn[0], ATT_WIDTH // ATT_QK_DIM)])[None, :],
        "inv_lane": jnp.tile(inv, LANES // half)[None, :],
        "conv_w": conv_w[0][:, 0, :],
        "conv_b": conv_b[0][None, :],
        "lam4": jnp.stack([lam_q1[0], lam_k1[0], lam_q2[0], lam_k2[0]]),
        "g_sub": g_sub[0][None, :],
        "g_mh": g_mh[0][None, :],
        "w_out": w_out[0].astype(BF16),
        "g_ffn": g_ffn[0][None, :],
        "w_pqt": w_pq[0].T.astype(BF16),
        "sub_k1": sub_k1[0].astype(BF16),
        "sub_k2": sub_k2[0].astype(BF16),
        "peer_u": peer_u[0].astype(BF16),
        "peer_vt": peer_v[0].astype(BF16).reshape(-1, PEER_GROUP, D_MODEL).transpose(0, 2, 1),
    }


def _trunk(x, meta, w):
    b, s, d = x.shape
    tt = PROJ_TILE
    assert s % tt == 0 and s % (2 * ATT_TK) == 0 and s % MIX_TILE == 0 and s % PEER_TILE == 0
    nt = s // tt
    xr = x.reshape(b, nt, tt, d)
    meta_tail = jnp.broadcast_to(meta[N_META - HALO:][None, None], (b, 1, HALO, d))
    prev = jnp.concatenate([meta_tail, xr[:, :-1, tt - HALO:, :]], axis=1)
    nxt = jnp.concatenate([xr[:, 1:, :HALO, :], jnp.zeros((b, 1, HALO, d), x.dtype)], axis=1)
    q, k, v, qm, km, vm, om, g_tm, g_fm = _proj(x, prev, nxt, N_META, tt, w)

    meta_b = jnp.broadcast_to(meta[None], (b, N_META, d))
    m_prev = jnp.zeros((b, 1, HALO, d), x.dtype)
    m_next = x[:, None, :HALO, :]
    _, k_m, v_m, qm_m, km_m, vm_m, _, gtm_m, gfm_m = _proj(meta_b, m_prev, m_next, 0, N_META, w)

    pad = CHUNK - N_META
    att = _attn(q, k, v, jnp.pad(k_m, ((0, 0), (0, pad), (0, 0))), jnp.pad(v_m, ((0, 0), (0, pad), (0, 0))),
                w["lam4"], w["g_sub"])

    front = lambda a: jnp.pad(a, ((0, 0), (pad, 0), (0, 0)))
    is_input_gate = (jnp.arange(N_GATES) // ML_HEADS) % 2 == 0
    pad_gate = jnp.where(is_input_gate, NEG_INF, 0.0).astype(F32)
    gtm_p = jnp.concatenate([jnp.broadcast_to(pad_gate[None, None, :], (b, pad, N_GATES)), gtm_m], axis=1)
    gfm_p = jnp.concatenate([jnp.broadcast_to(pad_gate[None, :, None], (b, N_GATES, pad)), gfm_m], axis=2)
    s_zero = jnp.zeros((b, ML_HEADS, ML_DIM, 2 * LANES), F32)
    m_zero = jnp.zeros((b, SUBLANES, LANES), F32)
    _, _, s0, m0 = _mlstm(front(qm_m), front(km_m), front(vm_m), gtm_p, gfm_p, s_zero, m_zero)
    hf, hb, _, _ = _mlstm(qm, km, vm, g_tm, g_fm, s0, m0)

    h1, xn2, r2, e2, n1, e1 = _mix(x, att, hf, hb, om, w)
    return _peer(xn2, h1, r2, e2, n1, e1, w)


def kernel(x_prompt, x_sample, meta, g_mix, w_in, b_gates, conv_w, conv_b, g_qn, g_kn, lam_q1, lam_k1,
           lam_q2, lam_k2, g_sub, g_mh, w_out, g_ffn, w_pq, sub_k1, sub_k2, peer_u, peer_v):
    w = _prep_weights(g_mix, w_in, b_gates, conv_w, conv_b, g_qn, g_kn, lam_q1, lam_k1, lam_q2, lam_k2,
                      g_sub, g_mh, w_out, g_ffn, w_pq, sub_k1, sub_k2, peer_u, peer_v)
    return (_trunk(x_prompt, meta, w), _trunk(x_sample, meta, w))
```

```python
import functools
import math

import jax
import jax.numpy as jnp
from jax import lax
from jax.experimental import pallas as pl
from jax.experimental.pallas import tpu as pltpu

F32 = jnp.float32
BF16 = jnp.bfloat16

D_MODEL = 1024
N_META = 16
ATT_HEADS = 4
ATT_QK_DIM = 64
ATT_V_DIM = 128
ATT_WIDTH = ATT_HEADS * ATT_V_DIM
ML_HEADS = 4
ML_DIM = 128
ML_WIDTH = ML_HEADS * ML_DIM
N_GATES = 4 * ML_HEADS
N_MAIN = 3 * ATT_WIDTH + 4 * ML_WIDTH
ROT_DIM = ATT_QK_DIM // 4
ROPE_THETA = 500000.0
CHUNK = 128
PEER_HEADS = 8
PEER_NKEYS = 128
PEER_QDIM = 256
PEER_TOPK = 16
EPS = 1e-6
LAM_INIT = 0.8 - 0.6 * math.exp(-0.3 * 0)
Q_SCALE = ATT_QK_DIM ** -0.5 * math.log2(math.e)

LANES = 128
SUBLANES = 8
HALO = SUBLANES
VMEM_LIMIT = 56 * 1024 * 1024

PROJ_TILE = 512
ATT_TQ = 512
ATT_TK = 1024
MIX_TILE = 512
PEER_TILE = 512
PEER_GROUP = 2048
PEER_PASS_LANES = 256
PEER_VBLOCK = 256
NEG_INF = float("-inf")


def _cparams(sem):
    return pltpu.CompilerParams(dimension_semantics=sem, vmem_limit_bytes=VMEM_LIMIT)


def _const_spec(shape):
    nd = len(shape)
    return pl.BlockSpec(shape, lambda *_: (0,) * nd)


def _log_sigmoid(x):
    return -(jnp.maximum(-x, 0.0) + jnp.log1p(jnp.exp(-jnp.abs(x))))


def _proj_kernel(pos_base, tt,
                 x_ref, prev_ref, next_ref, gmix_ref, wmain_ref, wg_ref, wgt_ref, bg_ref, bgt_ref,
                 gqk_ref, inv_ref, convw_ref, convb_ref,
                 q_ref, k_ref, v_ref, qm_ref, km_ref, vm_ref, om_ref, gtm_ref, gfm_ref,
                 xe_ref, zc_ref):
    i = pl.program_id(1)
    xe_ref[0:HALO, :] = prev_ref[0, 0]
    xe_ref[HALO:HALO + tt, :] = x_ref[0]
    xe_ref[HALO + tt:2 * HALO + tt, :] = next_ref[0, 0]
    xe = xe_ref[...]
    ms = jnp.mean(xe * xe, axis=-1, keepdims=True)
    xn_ext = (xe * lax.rsqrt(ms + EPS) * gmix_ref[...]).astype(BF16)
    xn = xn_ext[HALO:HALO + tt]

    def mm(a, lo, hi):
        return jnp.dot(a, wmain_ref[:, lo:hi], preferred_element_type=F32)

    zqk = mm(xn, 0, 2 * ATT_WIDTH)
    sq = (zqk * zqk).astype(BF16)
    r = lax.broadcasted_iota(jnp.int32, (2 * LANES, 2 * LANES), 0) // ATT_QK_DIM
    c = lax.broadcasted_iota(jnp.int32, (2 * LANES, 2 * LANES), 1) // ATT_QK_DIM
    seg_ones = (r == c).astype(BF16)
    lane = lax.broadcasted_iota(jnp.int32, (tt, LANES), 1)
    l64 = lane % ATT_QK_DIM
    row = lax.broadcasted_iota(jnp.int32, (tt, LANES), 0)
    pos = (pos_base + i * tt + row).astype(F32)
    ang = pos * inv_ref[...]
    cos_a = jnp.cos(ang)
    sin_a = jnp.sin(ang)
    half = ROT_DIM // 2
    cos_m = jnp.where(l64 < ROT_DIM, cos_a, 1.0)
    sin_lo = jnp.where(l64 < half, -sin_a, 0.0)
    sin_hi = jnp.where((l64 >= half) & (l64 < ROT_DIM), sin_a, 0.0)
    for cb in range(4):
        lo = cb * 2 * LANES
        ss = jnp.dot(sq[:, lo:lo + 2 * LANES], seg_ones, preferred_element_type=F32)
        rn = lax.rsqrt(ss * (1.0 / ATT_QK_DIM) + EPS)
        qn = zqk[:, lo:lo + 2 * LANES] * rn * gqk_ref[:, lo:lo + 2 * LANES]
        for sb in range(2):
            xs = qn[:, sb * LANES:(sb + 1) * LANES]
            rot = (xs * cos_m + pltpu.roll(xs, LANES - half, 1) * sin_lo
                   + pltpu.roll(xs, half, 1) * sin_hi)
            col = lo + sb * LANES
            if col < ATT_WIDTH:
                q_ref[0, :, col:col + LANES] = (rot * Q_SCALE).astype(BF16)
            else:
                k_ref[0, :, col - ATT_WIDTH:col - ATT_WIDTH + LANES] = rot.astype(BF16)

    va = mm(xn, 2 * ATT_WIDTH, 3 * ATT_WIDTH).astype(BF16)
    for h in range(ATT_HEADS):
        v_ref[0, :, 2 * h * LANES:(2 * h + 1) * LANES] = va[:, h * ATT_V_DIM:(h + 1) * ATT_V_DIM]
        v_ref[0, :, (2 * h + 1) * LANES:(2 * h + 2) * LANES] = jnp.ones((tt, LANES), BF16)

    base = 3 * ATT_WIDTH
    zc_ref[...] = mm(xn_ext, base, base + 2 * ML_WIDTH)
    y = (convw_ref[0:1, :] * zc_ref[pl.ds(HALO - 1, tt), :]
         + convw_ref[1:2, :] * zc_ref[pl.ds(HALO, tt), :]
         + convw_ref[2:3, :] * zc_ref[pl.ds(HALO + 1, tt), :]
         + convb_ref[...])
    act = y * (1.0 / (1.0 + jnp.exp(-y)))
    qm_ref[0] = act[:, :ML_WIDTH].astype(BF16)
    km_ref[0] = (act[:, ML_WIDTH:] * (ML_DIM ** -0.5)).astype(BF16)
    vm_ref[0] = mm(xn, base + 2 * ML_WIDTH, base + 3 * ML_WIDTH).astype(BF16)
    om_ref[0] = mm(xn, base + 3 * ML_WIDTH, base + 4 * ML_WIDTH).astype(BF16)

    g_tm = jnp.dot(xn, wg_ref[...], preferred_element_type=F32) + bg_ref[...]
    g_fm = lax.dot_general(wgt_ref[...], xn, (((1,), (1,)), ((), ())),
                           preferred_element_type=F32) + bgt_ref[...]
    col_tm = lax.broadcasted_iota(jnp.int32, (tt, N_GATES), 1)
    row_fm = lax.broadcasted_iota(jnp.int32, (N_GATES, tt), 0)
    gtm_ref[0] = jnp.where((col_tm // ML_HEADS) % 2 == 1, _log_sigmoid(g_tm), g_tm)
    gfm_ref[0] = jnp.where((row_fm // ML_HEADS) % 2 == 1, _log_sigmoid(g_fm), g_fm)


def _proj(x, prev, nxt, pos_base, tt, w):
    b, s, d = x.shape
    nt = s // tt
    tok = lambda width, dt: jax.ShapeDtypeStruct((b, s, width), dt)
    out_shape = ([tok(ATT_WIDTH, BF16)] * 2 + [tok(2 * ATT_WIDTH, BF16)] + [tok(ML_WIDTH, BF16)] * 4
                 + [tok(N_GATES, F32), jax.ShapeDtypeStruct((b, N_GATES, s), F32)])
    tspec = lambda width: pl.BlockSpec((1, tt, width), lambda bi, ti: (bi, ti, 0))
    hspec = pl.BlockSpec((1, 1, HALO, d), lambda bi, ti: (bi, ti, 0, 0))
    return pl.pallas_call(
        functools.partial(_proj_kernel, pos_base, tt),
        grid=(b, nt),
        in_specs=[tspec(d), hspec, hspec, _const_spec((1, d)), _const_spec((d, N_MAIN)),
                  _const_spec((d, N_GATES)), _const_spec((N_GATES, d)), _const_spec((1, N_GATES)),
                  _const_spec((N_GATES, 1)), _const_spec((1, 2 * ATT_WIDTH)), _const_spec((1, LANES)),
                  _const_spec((3, 2 * ML_WIDTH)), _const_spec((1, 2 * ML_WIDTH))],
        out_specs=[tspec(ATT_WIDTH)] * 2 + [tspec(2 * ATT_WIDTH)] + [tspec(ML_WIDTH)] * 4
        + [tspec(N_GATES), pl.BlockSpec((1, N_GATES, tt), lambda bi, ti: (bi, 0, ti))],
        out_shape=out_shape,
        scratch_shapes=[pltpu.VMEM((tt + 2 * HALO, d), F32), pltpu.VMEM((tt + 2 * HALO, 2 * ML_WIDTH), F32)],
        compiler_params=_cparams(("parallel", "parallel")),
        name=f"proj_t{tt}",
    )(x, prev, nxt, w["g_mix"], w["w_main"], w["w_g"], w["w_gt"], w["b_g"], w["b_gt"],
      w["g_qk"], w["inv_lane"], w["conv_w"], w["conv_b"])


def _attn_kernel(tq, tk, nk,
                 q_ref, k_ref, v_ref, kt_ref, vt_ref, lam_ref, gsub_ref, o_ref,
                 qq_ref, m_ref, acc_ref, sa_ref, sb_ref):
    q = q_ref[0]
    lane = lax.broadcasted_iota(jnp.int32, (tq, LANES), 1)
    zero = jnp.zeros_like(q)
    qq_ref[0:tq, :] = jnp.where(lane < ATT_QK_DIM, q, zero)
    qq_ref[tq:2 * tq, :] = jnp.where(lane >= ATT_QK_DIM, q, zero)
    m_ref[...] = jnp.full(m_ref.shape, NEG_INF, F32)
    acc_ref[...] = jnp.zeros(acc_ref.shape, F32)

    def scores(kb):
        return lax.dot_general(qq_ref[...], kb, (((1,), (1,)), ((), ())), preferred_element_type=F32)

    def update(s, vb):
        m_prev = m_ref[...]
        m_new = jnp.maximum(m_prev, jnp.max(s, axis=1, keepdims=True))
        alpha = jnp.exp2(m_prev - m_new)
        p = jnp.exp2(s - m_new[:, 0:1]).astype(BF16)
        pv = jnp.dot(p, vb, preferred_element_type=F32)
        acc_ref[:, 0:LANES] = alpha * acc_ref[:, 0:LANES] + pv[:, 0:LANES]
        acc_ref[:, LANES:2 * LANES] = alpha * acc_ref[:, LANES:2 * LANES] + pv[:, LANES:2 * LANES]
        m_ref[...] = m_new

    def kblock(j):
        return k_ref[0, pl.ds(pl.multiple_of(j * tk, tk), tk), :]

    def vblock(j):
        return v_ref[0, pl.ds(pl.multiple_of(j * tk, tk), tk), :]

    sa_ref[...] = scores(kblock(0))

    def body(i, carry):
        j = 2 * i
        sb_ref[...] = scores(kblock(j + 1))
        update(sa_ref[...], vblock(j))
        sa_ref[...] = scores(kblock(j + 2))
        update(sb_ref[...], vblock(j + 1))
        return carry

    lax.fori_loop(0, nk // 2 - 1, body, 0)
    sb_ref[...] = scores(kblock(nk - 1))
    update(sa_ref[...], vblock(nk - 2))
    col = lax.broadcasted_iota(jnp.int32, (2 * tq, LANES), 1)
    s_meta = jnp.where(col < N_META, scores(kt_ref[0]), NEG_INF)
    update(sb_ref[...], vblock(nk - 1))
    update(s_meta, vt_ref[0])

    o1 = acc_ref[0:tq, 0:LANES] / acc_ref[0:tq, LANES:2 * LANES]
    o2 = acc_ref[tq:2 * tq, 0:LANES] / acc_ref[tq:2 * tq, LANES:2 * LANES]
    lam = (jnp.exp(jnp.sum(lam_ref[0:1, :] * lam_ref[1:2, :], axis=1, keepdims=True))
           - jnp.exp(jnp.sum(lam_ref[2:3, :] * lam_ref[3:4, :], axis=1, keepdims=True)) + LAM_INIT)
    o = o1 - lam * o2
    ms = jnp.mean(o * o, axis=-1, keepdims=True)
    o_ref[0] = (o * lax.rsqrt(ms + EPS) * gsub_ref[...] * (1.0 - LAM_INIT)).astype(BF16)


def _attn(q, k, v, kt, vt, lam4, g_sub):
    b, s, _ = q.shape
    tq, tk = ATT_TQ, ATT_TK
    kspec = lambda rows: pl.BlockSpec((1, rows, LANES), lambda bi, hi, qi: (bi, 0, hi))
    vspec = lambda rows: pl.BlockSpec((1, rows, 2 * LANES), lambda bi, hi, qi: (bi, 0, hi))
    return pl.pallas_call(
        functools.partial(_attn_kernel, tq, tk, s // tk),
        grid=(b, ATT_HEADS, s // tq),
        in_specs=[pl.BlockSpec((1, tq, LANES), lambda bi, hi, qi: (bi, qi, hi)),
                  kspec(s), vspec(s), kspec(LANES), vspec(LANES),
                  _const_spec((4, ATT_QK_DIM)), _const_spec((1, ATT_V_DIM))],
        out_specs=pl.BlockSpec((1, tq, LANES), lambda bi, hi, qi: (bi, qi, hi)),
        out_shape=jax.ShapeDtypeStruct((b, s, ATT_WIDTH), BF16),
        scratch_shapes=[pltpu.VMEM((2 * tq, LANES), BF16), pltpu.VMEM((2 * tq, LANES), F32),
                        pltpu.VMEM((2 * tq, 2 * LANES), F32),
                        pltpu.VMEM((2 * tq, tk), F32), pltpu.VMEM((2 * tq, tk), F32)],
        compiler_params=_cparams(("parallel", "parallel", "parallel")),
        name="diff_attn",
    )(q, k, v, kt, vt, lam4, g_sub)


def _mlstm_chunk(reverse, q, k, v, g_tm, g_fm, s_ref, m_ref, h_ref):
    r = lax.broadcasted_iota(jnp.int32, (CHUNK, CHUNK), 0)
    c = lax.broadcasted_iota(jnp.int32, (CHUNK, CHUNK), 1)
    if reverse:
        causal = c >= r
        last = 0
        gbase = 2 * ML_HEADS
    else:
        causal = c <= r
        last = CHUNK - 1
        gbase = 0
    b_cols = jnp.dot(causal.astype(F32), g_tm, preferred_element_type=F32,
                     precision=lax.Precision.HIGHEST)
    b_rows = jnp.dot(g_fm, (r <= c if not reverse else r >= c).astype(F32),
                     preferred_element_type=F32, precision=lax.Precision.HIGHEST)
    ones_col = (c == 0).astype(BF16)
    for h in range(ML_HEADS):
        ic, fc = gbase + h, gbase + ML_HEADS + h
        b_col = b_cols[:, fc:fc + 1]
        b_row = b_rows[fc:fc + 1, :]
        li_row = g_fm[ic:ic + 1, :]
        li_col = g_tm[:, ic:ic + 1]
        m_prev = m_ref[h:h + 1, 0:1]
        log_d = jnp.where(causal, b_col - b_row + li_row, NEG_INF)
        m_inter = b_col + m_prev
        mj = jnp.maximum(m_inter, jnp.max(log_d, axis=1, keepdims=True))
        qh = q[:, h * ML_DIM:(h + 1) * ML_DIM]
        kh = k[:, h * ML_DIM:(h + 1) * ML_DIM]
        vext = jnp.concatenate([v[:, h * ML_DIM:(h + 1) * ML_DIM], ones_col], axis=1)
        qk = lax.dot_general(qh, kh, (((1,), (1,)), ((), ())), preferred_element_type=F32)
        w = (qk * jnp.exp(log_d - mj)).astype(BF16)
        state = s_ref[h]
        tot = (jnp.exp(m_inter - mj) * jnp.dot(qh, state.astype(BF16), preferred_element_type=F32)
               + jnp.dot(w, vext, preferred_element_type=F32))
        den = tot[:, ML_DIM:ML_DIM + 1]
        h_ref[0, :, h * ML_DIM:(h + 1) * ML_DIM] = (
            tot[:, :ML_DIM] / jnp.maximum(jnp.abs(den), jnp.exp(-mj)))
        b_last = b_col[last:last + 1, :]
        logw_col = b_last - b_col + li_col
        logw_row = b_last - b_row + li_row
        m_new = jnp.maximum(b_last + m_prev, jnp.max(logw_row, axis=1, keepdims=True))
        kw = (kh.astype(F32) * jnp.exp(logw_col - m_new)).astype(BF16)
        s_ref[h] = (jnp.exp(b_last + m_prev - m_new) * state
                    + lax.dot_general(kw, vext, (((0,), (0,)), ((), ())), preferred_element_type=F32))
        m_ref[h:h + 1, :] = jnp.broadcast_to(m_new, (1, LANES))


def _mlstm_kernel(qf_ref, kf_ref, vf_ref, gtf_ref, gff_ref, qb_ref, kb_ref, vb_ref, gtb_ref, gfb_ref,
                  s0_ref, m0_ref, hf_ref, hb_ref, sfin_ref, mfin_ref,
                  sf_ref, sb_ref, mf_ref, mb_ref):
    j = pl.program_id(1)

    @pl.when(j == 0)
    def _():
        sf_ref[...] = s0_ref[0]
        mf_ref[...] = m0_ref[0]
        sb_ref[...] = jnp.zeros(sb_ref.shape, F32)
        mb_ref[...] = jnp.zeros(mb_ref.shape, F32)

    _mlstm_chunk(False, qf_ref[0], kf_ref[0], vf_ref[0], gtf_ref[0], gff_ref[0], sf_ref, mf_ref, hf_ref)
    _mlstm_chunk(True, qb_ref[0], kb_ref[0], vb_ref[0], gtb_ref[0], gfb_ref[0], sb_ref, mb_ref, hb_ref)

    @pl.when(j == pl.num_programs(1) - 1)
    def _():
        sfin_ref[0] = sf_ref[...]
        mfin_ref[0] = mf_ref[...]


def _mlstm(qm, km, vm, g_tm, g_fm, s0, m0):
    b, s, _ = qm.shape
    nc = s // CHUNK
    fwd = lambda width: pl.BlockSpec((1, CHUNK, width), lambda bi, ci: (bi, ci, 0))
    bwd = lambda width: pl.BlockSpec((1, CHUNK, width), lambda bi, ci: (bi, nc - 1 - ci, 0))
    gf_fwd = pl.BlockSpec((1, N_GATES, CHUNK), lambda bi, ci: (bi, 0, ci))
    gf_bwd = pl.BlockSpec((1, N_GATES, CHUNK), lambda bi, ci: (bi, 0, nc - 1 - ci))
    st = pl.BlockSpec((1, ML_HEADS, ML_DIM, 2 * LANES), lambda bi, ci: (bi, 0, 0, 0))
    mx = pl.BlockSpec((1, SUBLANES, LANES), lambda bi, ci: (bi, 0, 0))
    return pl.pallas_call(
        _mlstm_kernel,
        grid=(b, nc),
        in_specs=[fwd(ML_WIDTH)] * 3 + [fwd(N_GATES), gf_fwd] + [bwd(ML_WIDTH)] * 3 + [bwd(N_GATES), gf_bwd]
        + [st, mx],
        out_specs=[fwd(ML_WIDTH), bwd(ML_WIDTH), st, mx],
        out_shape=[jax.ShapeDtypeStruct((b, s, ML_WIDTH), F32)] * 2
        + [jax.ShapeDtypeStruct((b, ML_HEADS, ML_DIM, 2 * LANES), F32),
           jax.ShapeDtypeStruct((b, SUBLANES, LANES), F32)],
        scratch_shapes=[pltpu.VMEM((ML_HEADS, ML_DIM, 2 * LANES), F32)] * 2
        + [pltpu.VMEM((SUBLANES, LANES), F32)] * 2,
        compiler_params=_cparams(("parallel", "arbitrary")),
        name="mlstm",
    )(qm, km, vm, g_tm, g_fm, qm, km, vm, g_tm, g_fm, s0, m0)


def _top_values(x, n, with_rank=False):
    vals = []
    cur = x
    rank = jnp.full(x.shape, float(n), F32) if with_rank else None
    for i in range(n):
        mx = jnp.max(cur, axis=0, keepdims=True)
        vals.append(mx)
        hit = cur == mx
        if with_rank:
            rank = jnp.where(hit, float(i), rank)
        if i + 1 < n:
            cur = jnp.where(hit, NEG_INF, cur)
    return (vals, rank) if with_rank else vals


def _sorting_network(n):
    pairs = []
    p = 1
    while p < n:
        k = p
        while k >= 1:
            for j in range(k % p, n - k, 2 * k):
                for i in range(min(k, n - j - k)):
                    if (i + j) // (2 * p) == (i + j + k) // (2 * p):
                        pairs.append((i + j, i + j + k))
            k //= 2
        p *= 2
    return pairs


def _top_values_sorted(x, n):
    groups = [x[SUBLANES * g:SUBLANES * (g + 1)] for g in range(x.shape[0] // SUBLANES)]
    size = 1
    while size < len(groups):
        size *= 2
    groups += [jnp.full_like(groups[0], NEG_INF)] * (size - len(groups))
    for i, j in _sorting_network(size):
        groups[i], groups[j] = jnp.maximum(groups[i], groups[j]), jnp.minimum(groups[i], groups[j])
    vals = []
    for r in range(n):
        mx = jnp.max(groups[0], axis=0, keepdims=True)
        vals.append(mx)
        if r + 1 < n:
            hit = groups[0] == mx
            live = min(len(groups), n - 1 - r)
            for d in range(live):
                below = groups[d + 1] if d + 1 < len(groups) else NEG_INF
                groups[d] = jnp.where(hit, below, groups[d])
    return vals


def _bf16_pair_bits(x):
    u = lax.bitcast_convert_type(x.astype(BF16).astype(F32), jnp.uint32)
    return u | (u >> 16)


_CAND_PAIRS = [(a, b) for a in range(PEER_TOPK + 1) for b in range(PEER_TOPK + 1)
               if (a + 1) * (b + 1) <= PEER_TOPK + 1]
_CAND_ROWS = -(-len(_CAND_PAIRS) // SUBLANES) * SUBLANES


def _mix_kernel(tt, x_ref, att_ref, hf_ref, hb_ref, om_ref, gmh_ref, wout_ref, gffn_ref, wpqt_ref,
                k1_ref, k2_ref, h1_ref, xn2_ref, r2_ref, e2_ref, n1_ref, e1_ref, mix_ref, cand_ref):
    mix_ref[:, 0:ATT_WIDTH] = att_ref[0]
    for h in range(ML_HEADS):
        sl = slice(h * ML_DIM, (h + 1) * ML_DIM)
        hs = hf_ref[0, :, sl] + hb_ref[0, :, sl]
        ms = jnp.mean(hs * hs, axis=-1, keepdims=True)
        hn = hs * lax.rsqrt(ms + EPS) * gmh_ref[...]
        gate = 1.0 / (1.0 + jnp.exp(-om_ref[0, :, sl].astype(F32)))
        mix_ref[:, ATT_WIDTH + h * ML_DIM:ATT_WIDTH + (h + 1) * ML_DIM] = (hn * gate).astype(BF16)
    h1 = x_ref[0] + jnp.dot(mix_ref[...], wout_ref[...], preferred_element_type=F32)
    h1_ref[0] = h1
    ms = jnp.mean(h1 * h1, axis=-1, keepdims=True)
    xn2 = (h1 * lax.rsqrt(ms + EPS) * gffn_ref[...]).astype(BF16)
    xn2_ref[0] = xn2
    qt = lax.dot_general(wpqt_ref[...], xn2, (((1,), (1,)), ((), ())), preferred_element_type=F32)
    half = PEER_QDIM // 2
    cand_ref[...] = jnp.full(cand_ref.shape, NEG_INF, F32)
    for h in range(PEER_HEADS):
        q1 = qt[h * PEER_QDIM:h * PEER_QDIM + half].astype(BF16)
        q2 = qt[h * PEER_QDIM + half:(h + 1) * PEER_QDIM].astype(BF16)
        s1 = jnp.dot(k1_ref[h], q1, preferred_element_type=F32)
        s2 = jnp.dot(k2_ref[h], q2, preferred_element_type=F32)
        for sl in range(tt // LANES):
            lanes = slice(sl * LANES, (sl + 1) * LANES)
            a1 = s1[:, lanes]
            a2 = s2[:, lanes]
            v1 = _top_values_sorted(a1, PEER_TOPK + 1)
            v2, rank2 = _top_values(a2, PEER_TOPK + 1, with_rank=True)
            for idx, (a, b) in enumerate(_CAND_PAIRS):
                cand_ref[idx:idx + 1, lanes] = v1[a] + v2[b]
            cs = _top_values_sorted(cand_ref[:, lanes], PEER_TOPK + 1)
            z = jnp.zeros_like(cs[0])
            for i in range(PEER_TOPK):
                z = z + jnp.exp(cs[i] - cs[0])
            thr = 0.5 * (cs[PEER_TOPK - 1] + cs[PEER_TOPK])
            thr1 = thr - a1
            n1 = jnp.zeros_like(a1)
            for b in range(PEER_TOPK // 2):
                n1 = n1 + jnp.where(v2[b] > thr1, 1.0, 0.0)
            n_best = jnp.zeros_like(thr)
            for b in range(PEER_TOPK // 2, PEER_TOPK):
                n_best = n_best + jnp.where(v2[b] > thr - v1[0], 1.0, 0.0)
            n1 = n1 + jnp.where(a1 == v1[0], n_best, 0.0)
            r2_ref[0, h, :, lanes] = rank2.astype(BF16)
            n1_ref[0, h, :, lanes] = _bf16_pair_bits(n1)
            e1_ref[0, h, :, lanes] = _bf16_pair_bits(jnp.exp(a1 - v1[0]) * (0.5 / z))
            e2_ref[0, h, :, lanes] = jnp.exp(a2 - v2[0]).astype(BF16)


def _mix(x, att, hf, hb, om, w):
    b, s, d = x.shape
    tt = MIX_TILE
    nt = s // tt
    tspec = lambda width: pl.BlockSpec((1, tt, width), lambda bi, ti: (bi, ti, 0))
    sspec = pl.BlockSpec((1, PEER_HEADS, PEER_NKEYS, tt), lambda bi, ti: (bi, 0, 0, ti))
    return pl.pallas_call(
        functools.partial(_mix_kernel, tt),
        grid=(b, nt),
        in_specs=[tspec(d), tspec(ATT_WIDTH), tspec(ML_WIDTH), tspec(ML_WIDTH), tspec(ML_WIDTH),
                  _const_spec((1, ML_DIM)), _const_spec((d, d)), _const_spec((1, d)),
                  _const_spec((PEER_HEADS * PEER_QDIM, d)),
                  _const_spec((PEER_HEADS, PEER_NKEYS, PEER_QDIM // 2)),
                  _const_spec((PEER_HEADS, PEER_NKEYS, PEER_QDIM // 2))],
        out_specs=[tspec(d), tspec(d), sspec, sspec, sspec, sspec],
        out_shape=[jax.ShapeDtypeStruct((b, s, d), F32), jax.ShapeDtypeStruct((b, s, d), BF16),
                   jax.ShapeDtypeStruct((b, PEER_HEADS, PEER_NKEYS, s), BF16),
                   jax.ShapeDtypeStruct((b, PEER_HEADS, PEER_NKEYS, s), BF16),
                   jax.ShapeDtypeStruct((b, PEER_HEADS, PEER_NKEYS, s), jnp.uint32),
                   jax.ShapeDtypeStruct((b, PEER_HEADS, PEER_NKEYS, s), jnp.uint32)],
        scratch_shapes=[pltpu.VMEM((tt, d), BF16), pltpu.VMEM((_CAND_ROWS, tt), F32)],
        compiler_params=_cparams(("parallel", "parallel")),
        name="mix_route",
    )(x, att, hf, hb, om, w["g_mh"], w["w_out"], w["g_ffn"], w["w_pqt"], w["sub_k1"], w["sub_k2"])


def _gelu_x2(x):
    return x + x * lax.erf(x * (2.0 ** -0.5))


def _peer_kernel(tt, ng, xn2_ref, h1_ref, r2_ref, e2_ref, n1_ref, e1_ref, u_ref, vt_ref, y_ref,
                 a0_ref, a1_ref, w_ref, acc_ref):
    s = pl.program_id(1)
    rows = PEER_GROUP // PEER_NKEYS

    @pl.when(s == 0)
    def _():
        a0_ref[...] = jnp.zeros(a0_ref.shape, F32)
        a1_ref[...] = jnp.zeros(a1_ref.shape, F32)

    @pl.when((s < 1) | ((s + ng - 1) % ng == 0))
    def _():
        acc_ref[...] = jnp.zeros(acc_ref.shape, F32)

    g2 = (s + ng - 1) % ng

    def row(ref, h, c, lanes):
        words = jnp.broadcast_to(ref[0, h, pl.ds(c, 1), lanes], (SUBLANES, PEER_PASS_LANES))
        tile = pltpu.bitcast(words, BF16)
        return jnp.concatenate([tile] * (PEER_NKEYS // (2 * SUBLANES)), axis=0)

    def stages(a_cur, a_oth):
        zero = jnp.zeros((PEER_NKEYS, PEER_PASS_LANES), BF16)
        tok_half = tt // 2
        for kb in range(PEER_GROUP // PEER_VBLOCK):
            for cc in range(kb * PEER_VBLOCK // PEER_NKEYS, (kb + 1) * PEER_VBLOCK // PEER_NKEYS):
                c = g2 * rows + cc
                ex = slice(cc * PEER_NKEYS, (cc + 1) * PEER_NKEYS)
                for lb in range(tt // PEER_PASS_LANES):
                    lanes = pl.ds(lb * PEER_PASS_LANES, PEER_PASS_LANES)
                    wsum = zero
                    for h in range(PEER_HEADS):
                        sel = r2_ref[0, h, :, lanes] < row(n1_ref, h, c, lanes)
                        wsum = wsum + jnp.where(sel, e2_ref[0, h, :, lanes], zero) * row(e1_ref, h, c, lanes)
                    w_ref[ex, lanes] = wsum * _gelu_x2(a_oth[ex, lanes]).astype(BF16)
            if kb < 2:
                tk = slice(kb * tok_half, (kb + 1) * tok_half)
                a_cur[:, tk] = lax.dot_general(u_ref[...], xn2_ref[0, tk, :], (((1,), (1,)), ((), ())),
                                               preferred_element_type=F32)
            blk = slice(kb * PEER_VBLOCK, (kb + 1) * PEER_VBLOCK)
            acc_ref[...] += jnp.dot(vt_ref[0, :, blk], w_ref[blk, :], preferred_element_type=F32)

    @pl.when(s % 2 == 0)
    def _():
        stages(a0_ref, a1_ref)

    @pl.when(s % 2 == 1)
    def _():
        stages(a1_ref, a0_ref)

    @pl.when((s >= 1) & ((s + ng - 1) % ng == ng - 1))
    def _():
        y_ref[0] = h1_ref[0] + acc_ref[...].T


def _peer(xn2, h1, r2, e2, n1, e1, w):
    b, s, d = h1.shape
    tt = PEER_TILE
    nt = s // tt
    ng = (PEER_NKEYS * PEER_NKEYS) // PEER_GROUP
    tile = lambda si, lag: jnp.clip((si - lag) // ng, 0, nt - 1)
    tspec = lambda lag: pl.BlockSpec((1, tt, d), lambda bi, si: (bi, tile(si, lag), 0))
    sspec = pl.BlockSpec((1, PEER_HEADS, PEER_NKEYS, tt), lambda bi, si: (bi, 0, 0, tile(si, 1)))
    return pl.pallas_call(
        functools.partial(_peer_kernel, tt, ng),
        grid=(b, nt * ng + 1),
        in_specs=[tspec(0), tspec(1), sspec, sspec, sspec, sspec,
                  pl.BlockSpec((PEER_GROUP, d), lambda bi, si: (si % ng, 0)),
                  pl.BlockSpec((1, d, PEER_GROUP), lambda bi, si: ((si + ng - 1) % ng, 0, 0))],
        out_specs=tspec(1),
        out_shape=jax.ShapeDtypeStruct((b, s, d), F32),
        scratch_shapes=[pltpu.VMEM((PEER_GROUP, tt), F32)] * 2 + [pltpu.VMEM((PEER_GROUP, tt), BF16)]
        + [pltpu.VMEM((d, tt), F32)],
        compiler_params=_cparams(("parallel", "arbitrary")),
        name="peer_dense",
    )(xn2, h1, r2, e2, n1, e1, w["peer_u"], w["peer_vt"])


def _prep_weights(g_mix, w_in, b_gates, conv_w, conv_b, g_qn, g_kn, lam_q1, lam_k1, lam_q2, lam_k2,
                  g_sub, g_mh, w_out, g_ffn, w_pq, sub_k1, sub_k2, peer_u, peer_v):
    half = ROT_DIM // 2
    inv = ROPE_THETA ** (-jnp.arange(half, dtype=F32) * 2.0 / ROT_DIM)
    w_g = w_in[0][:, N_MAIN:]
    return {
        "g_mix": g_mix[0][None, :],
        "w_main": w_in[0][:, :N_MAIN].astype(BF16),
        "w_g": w_g.astype(BF16),
        "w_gt": w_g.T.astype(BF16),
        "b_g": b_gates[0][None, :],
        "b_gt": b_gates[0][:, None],
        "g_qk": jnp.concatenate([jnp.tile(g_qn[0], ATT_WIDTH // ATT_QK_DIM),
                                 jnp.tile(g_kn[0], ATT_WIDTH // ATT_QK_DIM)])[None, :],
        "inv_lane": jnp.tile(inv, LANES // half)[None, :],
        "conv_w": conv_w[0][:, 0, :],
        "conv_b": conv_b[0][None, :],
        "lam4": jnp.stack([lam_q1[0], lam_k1[0], lam_q2[0], lam_k2[0]]),
        "g_sub": g_sub[0][None, :],
        "g_mh": g_mh[0][None, :],
        "w_out": w_out[0].astype(BF16),
        "g_ffn": g_ffn[0][None, :],
        "w_pqt": w_pq[0].T.astype(BF16),
        "sub_k1": sub_k1[0].astype(BF16),
        "sub_k2": sub_k2[0].astype(BF16),
        "peer_u": peer_u[0].astype(BF16),
        "peer_vt": peer_v[0].astype(BF16).reshape(-1, PEER_GROUP, D_MODEL).transpose(0, 2, 1),
    }


def _trunk(x, meta, w):
    b, s, d = x.shape
    tt = PROJ_TILE
    assert s % tt == 0 and s % (2 * ATT_TK) == 0 and s % MIX_TILE == 0 and s % PEER_TILE == 0
    nt = s // tt
    xr = x.reshape(b, nt, tt, d)
    meta_tail = jnp.broadcast_to(meta[N_META - HALO:][None, None], (b, 1, HALO, d))
    prev = jnp.concatenate([meta_tail, xr[:, :-1, tt - HALO:, :]], axis=1)
    nxt = jnp.concatenate([xr[:, 1:, :HALO, :], jnp.zeros((b, 1, HALO, d), x.dtype)], axis=1)
    q, k, v, qm, km, vm, om, g_tm, g_fm = _proj(x, prev, nxt, N_META, tt, w)

    meta_b = jnp.broadcast_to(meta[None], (b, N_META, d))
    m_prev = jnp.zeros((b, 1, HALO, d), x.dtype)
    m_next = x[:, None, :HALO, :]
    _, k_m, v_m, qm_m, km_m, vm_m, _, gtm_m, gfm_m = _proj(meta_b, m_prev, m_next, 0, N_META, w)

    pad = CHUNK - N_META
    att = _attn(q, k, v, jnp.pad(k_m, ((0, 0), (0, pad), (0, 0))), jnp.pad(v_m, ((0, 0), (0, pad), (0, 0))),
                w["lam4"], w["g_sub"])

    front = lambda a: jnp.pad(a, ((0, 0), (pad, 0), (0, 0)))
    is_input_gate = (jnp.arange(N_GATES) // ML_HEADS) % 2 == 0
    pad_gate = jnp.where(is_input_gate, NEG_INF, 0.0).astype(F32)
    gtm_p = jnp.concatenate([jnp.broadcast_to(pad_gate[None, None, :], (b, pad, N_GATES)), gtm_m], axis=1)
    gfm_p = jnp.concatenate([jnp.broadcast_to(pad_gate[None, :, None], (b, N_GATES, pad)), gfm_m], axis=2)
    s_zero = jnp.zeros((b, ML_HEADS, ML_DIM, 2 * LANES), F32)
    m_zero = jnp.zeros((b, SUBLANES, LANES), F32)
    _, _, s0, m0 = _mlstm(front(qm_m), front(km_m), front(vm_m), gtm_p, gfm_p, s_zero, m_zero)
    hf, hb, _, _ = _mlstm(qm, km, vm, g_tm, g_fm, s0, m0)

    h1, xn2, r2, e2, n1, e1 = _mix(x, att, hf, hb, om, w)
    return _peer(xn2, h1, r2, e2, n1, e1, w)


def kernel(x_prompt, x_sample, meta, g_mix, w_in, b_gates, conv_w, conv_b, g_qn, g_kn, lam_q1, lam_k1,
           lam_q2, lam_k2, g_sub, g_mh, w_out, g_ffn, w_pq, sub_k1, sub_k2, peer_u, peer_v):
    w = _prep_weights(g_mix, w_in, b_gates, conv_w, conv_b, g_qn, g_kn, lam_q1, lam_k1, lam_q2, lam_k2,
                      g_sub, g_mh, w_out, g_ffn, w_pq, sub_k1, sub_k2, peer_u, peer_v)
    return (_trunk(x_prompt, meta, w), _trunk(x_sample, meta, w))
```

```python
import functools
import math

import jax
import jax.numpy as jnp
from jax import lax
from jax.experimental import pallas as pl
from jax.experimental.pallas import tpu as pltpu

F32 = jnp.float32
BF16 = jnp.bfloat16

D_MODEL = 1024
N_META = 16
ATT_HEADS = 4
ATT_QK_DIM = 64
ATT_V_DIM = 128
ATT_WIDTH = ATT_HEADS * ATT_V_DIM
ML_HEADS = 4
ML_DIM = 128
ML_WIDTH = ML_HEADS * ML_DIM
N_GATES = 4 * ML_HEADS
N_MAIN = 3 * ATT_WIDTH + 4 * ML_WIDTH
ROT_DIM = ATT_QK_DIM // 4
ROPE_THETA = 500000.0
CHUNK = 128
PEER_HEADS = 8
PEER_NKEYS = 128
PEER_QDIM = 256
PEER_TOPK = 16
EPS = 1e-6
LAM_INIT = 0.8 - 0.6 * math.exp(-0.3 * 0)
Q_SCALE = ATT_QK_DIM ** -0.5 * math.log2(math.e)

LANES = 128
SUBLANES = 8
HALO = SUBLANES
VMEM_LIMIT = 56 * 1024 * 1024

PROJ_TILE = 512
ATT_TQ = 512
ATT_TK = 1024
MIX_TILE = 512
PEER_TILE = 512
PEER_GROUP = 2048
PEER_PASS_LANES = 256
PEER_VBLOCK = 256
NEG_INF = float("-inf")


def _cparams(sem):
    return pltpu.CompilerParams(dimension_semantics=sem, vmem_limit_bytes=VMEM_LIMIT)


def _const_spec(shape):
    nd = len(shape)
    return pl.BlockSpec(shape, lambda *_: (0,) * nd)


def _log_sigmoid(x):
    return -(jnp.maximum(-x, 0.0) + jnp.log1p(jnp.exp(-jnp.abs(x))))


def _proj_kernel(pos_base, tt,
                 x_ref, prev_ref, next_ref, gmix_ref, wmain_ref, wg_ref, wgt_ref, bg_ref, bgt_ref,
                 gqk_ref, inv_ref, convw_ref, convb_ref,
                 q_ref, k_ref, v_ref, qm_ref, km_ref, vm_ref, om_ref, gtm_ref, gfm_ref,
                 xe_ref, zc_ref):
    i = pl.program_id(1)
    xe_ref[0:HALO, :] = prev_ref[0, 0]
    xe_ref[HALO:HALO + tt, :] = x_ref[0]
    xe_ref[HALO + tt:2 * HALO + tt, :] = next_ref[0, 0]
    xe = xe_ref[...]
    ms = jnp.mean(xe * xe, axis=-1, keepdims=True)
    xn_ext = (xe * lax.rsqrt(ms + EPS) * gmix_ref[...]).astype(BF16)
    xn = xn_ext[HALO:HALO + tt]

    def mm(a, lo, hi):
        return jnp.dot(a, wmain_ref[:, lo:hi], preferred_element_type=F32)

    zqk = mm(xn, 0, 2 * ATT_WIDTH)
    sq = (zqk * zqk).astype(BF16)
    r = lax.broadcasted_iota(jnp.int32, (2 * LANES, 2 * LANES), 0) // ATT_QK_DIM
    c = lax.broadcasted_iota(jnp.int32, (2 * LANES, 2 * LANES), 1) // ATT_QK_DIM
    seg_ones = (r == c).astype(BF16)
    lane = lax.broadcasted_iota(jnp.int32, (tt, LANES), 1)
    l64 = lane % ATT_QK_DIM
    row = lax.broadcasted_iota(jnp.int32, (tt, LANES), 0)
    pos = (pos_base + i * tt + row).astype(F32)
    ang = pos * inv_ref[...]
    cos_a = jnp.cos(ang)
    sin_a = jnp.sin(ang)
    half = ROT_DIM // 2
    cos_m = jnp.where(l64 < ROT_DIM, cos_a, 1.0)
    sin_lo = jnp.where(l64 < half, -sin_a, 0.0)
    sin_hi = jnp.where((l64 >= half) & (l64 < ROT_DIM), sin_a, 0.0)
    for cb in range(4):
        lo = cb * 2 * LANES
        ss = jnp.dot(sq[:, lo:lo + 2 * LANES], seg_ones, preferred_element_type=F32)
        rn = lax.rsqrt(ss * (1.0 / ATT_QK_DIM) + EPS)
        qn = zqk[:, lo:lo + 2 * LANES] * rn * gqk_ref[:, lo:lo + 2 * LANES]
        for sb in range(2):
            xs = qn[:, sb * LANES:(sb + 1) * LANES]
            rot = (xs * cos_m + pltpu.roll(xs, LANES - half, 1) * sin_lo
                   + pltpu.roll(xs, half, 1) * sin_hi)
            col = lo + sb * LANES
            if col < ATT_WIDTH:
                q_ref[0, :, col:col + LANES] = (rot * Q_SCALE).astype(BF16)
            else:
                k_ref[0, :, col - ATT_WIDTH:col - ATT_WIDTH + LANES] = rot.astype(BF16)

    va = mm(xn, 2 * ATT_WIDTH, 3 * ATT_WIDTH).astype(BF16)
    for h in range(ATT_HEADS):
        v_ref[0, :, 2 * h * LANES:(2 * h + 1) * LANES] = va[:, h * ATT_V_DIM:(h + 1) * ATT_V_DIM]
        v_ref[0, :, (2 * h + 1) * LANES:(2 * h + 2) * LANES] = jnp.ones((tt, LANES), BF16)

    base = 3 * ATT_WIDTH
    zc_ref[...] = mm(xn_ext, base, base + 2 * ML_WIDTH)
    y = (convw_ref[0:1, :] * zc_ref[pl.ds(HALO - 1, tt), :]
         + convw_ref[1:2, :] * zc_ref[pl.ds(HALO, tt), :]
         + convw_ref[2:3, :] * zc_ref[pl.ds(HALO + 1, tt), :]
         + convb_ref[...])
    act = y * (1.0 / (1.0 + jnp.exp(-y)))
    qm_ref[0] = act[:, :ML_WIDTH].astype(BF16)
    km_ref[0] = (act[:, ML_WIDTH:] * (ML_DIM ** -0.5)).astype(BF16)
    vm_ref[0] = mm(xn, base + 2 * ML_WIDTH, base + 3 * ML_WIDTH).astype(BF16)
    om_ref[0] = mm(xn, base + 3 * ML_WIDTH, base + 4 * ML_WIDTH).astype(BF16)

    g_tm = jnp.dot(xn, wg_ref[...], preferred_element_type=F32) + bg_ref[...]
    g_fm = lax.dot_general(wgt_ref[...], xn, (((1,), (1,)), ((), ())),
                           preferred_element_type=F32) + bgt_ref[...]
    col_tm = lax.broadcasted_iota(jnp.int32, (tt, N_GATES), 1)
    row_fm = lax.broadcasted_iota(jnp.int32, (N_GATES, tt), 0)
    gtm_ref[0] = jnp.where((col_tm // ML_HEADS) % 2 == 1, _log_sigmoid(g_tm), g_tm)
    gfm_ref[0] = jnp.where((row_fm // ML_HEADS) % 2 == 1, _log_sigmoid(g_fm), g_fm)


def _proj(x, prev, nxt, pos_base, tt, w):
    b, s, d = x.shape
    nt = s // tt
    tok = lambda width, dt: jax.ShapeDtypeStruct((b, s, width), dt)
    out_shape = ([tok(ATT_WIDTH, BF16)] * 2 + [tok(2 * ATT_WIDTH, BF16)] + [tok(ML_WIDTH, BF16)] * 4
                 + [tok(N_GATES, F32), jax.ShapeDtypeStruct((b, N_GATES, s), F32)])
    tspec = lambda width: pl.BlockSpec((1, tt, width), lambda bi, ti: (bi, ti, 0))
    hspec = pl.BlockSpec((1, 1, HALO, d), lambda bi, ti: (bi, ti, 0, 0))
    return pl.pallas_call(
        functools.partial(_proj_kernel, pos_base, tt),
        grid=(b, nt),
        in_specs=[tspec(d), hspec, hspec, _const_spec((1, d)), _const_spec((d, N_MAIN)),
                  _const_spec((d, N_GATES)), _const_spec((N_GATES, d)), _const_spec((1, N_GATES)),
                  _const_spec((N_GATES, 1)), _const_spec((1, 2 * ATT_WIDTH)), _const_spec((1, LANES)),
                  _const_spec((3, 2 * ML_WIDTH)), _const_spec((1, 2 * ML_WIDTH))],
        out_specs=[tspec(ATT_WIDTH)] * 2 + [tspec(2 * ATT_WIDTH)] + [tspec(ML_WIDTH)] * 4
        + [tspec(N_GATES), pl.BlockSpec((1, N_GATES, tt), lambda bi, ti: (bi, 0, ti))],
        out_shape=out_shape,
        scratch_shapes=[pltpu.VMEM((tt + 2 * HALO, d), F32), pltpu.VMEM((tt + 2 * HALO, 2 * ML_WIDTH), F32)],
        compiler_params=_cparams(("parallel", "parallel")),
        name=f"proj_t{tt}",
    )(x, prev, nxt, w["g_mix"], w["w_main"], w["w_g"], w["w_gt"], w["b_g"], w["b_gt"],
      w["g_qk"], w["inv_lane"], w["conv_w"], w["conv_b"])


def _attn_kernel(tq, tk, nk,
                 q_ref, k_ref, v_ref, kt_ref, vt_ref, lam_ref, gsub_ref, o_ref,
                 qq_ref, m_ref, acc_ref, sa_ref, sb_ref):
    q = q_ref[0]
    lane = lax.broadcasted_iota(jnp.int32, (tq, LANES), 1)
    zero = jnp.zeros_like(q)
    qq_ref[0:tq, :] = jnp.where(lane < ATT_QK_DIM, q, zero)
    qq_ref[tq:2 * tq, :] = jnp.where(lane >= ATT_QK_DIM, q, zero)
    m_ref[...] = jnp.full(m_ref.shape, NEG_INF, F32)
    acc_ref[...] = jnp.zeros(acc_ref.shape, F32)

    def scores(kb):
        return lax.dot_general(qq_ref[...], kb, (((1,), (1,)), ((), ())), preferred_element_type=F32)

    def update(s, vb):
        m_prev = m_ref[...]
        m_new = jnp.maximum(m_prev, jnp.max(s, axis=1, keepdims=True))
        alpha = jnp.exp2(m_prev - m_new)
        p = jnp.exp2(s - m_new[:, 0:1]).astype(BF16)
        pv = jnp.dot(p, vb, preferred_element_type=F32)
        acc_ref[:, 0:LANES] = alpha * acc_ref[:, 0:LANES] + pv[:, 0:LANES]
        acc_ref[:, LANES:2 * LANES] = alpha * acc_ref[:, LANES:2 * LANES] + pv[:, LANES:2 * LANES]
        m_ref[...] = m_new

    def kblock(j):
        return k_ref[0, pl.ds(pl.multiple_of(j * tk, tk), tk), :]

    def vblock(j):
        return v_ref[0, pl.ds(pl.multiple_of(j * tk, tk), tk), :]

    sa_ref[...] = scores(kblock(0))

    def body(i, carry):
        j = 2 * i
        sb_ref[...] = scores(kblock(j + 1))
        update(sa_ref[...], vblock(j))
        sa_ref[...] = scores(kblock(j + 2))
        update(sb_ref[...], vblock(j + 1))
        return carry

    lax.fori_loop(0, nk // 2 - 1, body, 0)
    sb_ref[...] = scores(kblock(nk - 1))
    update(sa_ref[...], vblock(nk - 2))
    col = lax.broadcasted_iota(jnp.int32, (2 * tq, LANES), 1)
    s_meta = jnp.where(col < N_META, scores(kt_ref[0]), NEG_INF)
    update(sb_ref[...], vblock(nk - 1))
    update(s_meta, vt_ref[0])

    o1 = acc_ref[0:tq, 0:LANES] / acc_ref[0:tq, LANES:2 * LANES]
    o2 = acc_ref[tq:2 * tq, 0:LANES] / acc_ref[tq:2 * tq, LANES:2 * LANES]
    lam = (jnp.exp(jnp.sum(lam_ref[0:1, :] * lam_ref[1:2, :], axis=1, keepdims=True))
           - jnp.exp(jnp.sum(lam_ref[2:3, :] * lam_ref[3:4, :], axis=1, keepdims=True)) + LAM_INIT)
    o = o1 - lam * o2
    ms = jnp.mean(o * o, axis=-1, keepdims=True)
    o_ref[0] = (o * lax.rsqrt(ms + EPS) * gsub_ref[...] * (1.0 - LAM_INIT)).astype(BF16)


def _attn(q, k, v, kt, vt, lam4, g_sub):
    b, s, _ = q.shape
    tq, tk = ATT_TQ, ATT_TK
    kspec = lambda rows: pl.BlockSpec((1, rows, LANES), lambda bi, hi, qi: (bi, 0, hi))
    vspec = lambda rows: pl.BlockSpec((1, rows, 2 * LANES), lambda bi, hi, qi: (bi, 0, hi))
    return pl.pallas_call(
        functools.partial(_attn_kernel, tq, tk, s // tk),
        grid=(b, ATT_HEADS, s // tq),
        in_specs=[pl.BlockSpec((1, tq, LANES), lambda bi, hi, qi: (bi, qi, hi)),
                  kspec(s), vspec(s), kspec(LANES), vspec(LANES),
                  _const_spec((4, ATT_QK_DIM)), _const_spec((1, ATT_V_DIM))],
        out_specs=pl.BlockSpec((1, tq, LANES), lambda bi, hi, qi: (bi, qi, hi)),
        out_shape=jax.ShapeDtypeStruct((b, s, ATT_WIDTH), BF16),
        scratch_shapes=[pltpu.VMEM((2 * tq, LANES), BF16), pltpu.VMEM((2 * tq, LANES), F32),
                        pltpu.VMEM((2 * tq, 2 * LANES), F32),
                        pltpu.VMEM((2 * tq, tk), F32), pltpu.VMEM((2 * tq, tk), F32)],
        compiler_params=_cparams(("parallel", "parallel", "parallel")),
        name="diff_attn",
    )(q, k, v, kt, vt, lam4, g_sub)


def _mlstm_chunk(reverse, q, k, v, g_tm, g_fm, s_ref, m_ref, h_ref):
    r = lax.broadcasted_iota(jnp.int32, (CHUNK, CHUNK), 0)
    c = lax.broadcasted_iota(jnp.int32, (CHUNK, CHUNK), 1)
    if reverse:
        causal = c >= r
        last = 0
        gbase = 2 * ML_HEADS
    else:
        causal = c <= r
        last = CHUNK - 1
        gbase = 0
    b_cols = jnp.dot(causal.astype(F32), g_tm, preferred_element_type=F32,
                     precision=lax.Precision.HIGHEST)
    b_rows = jnp.dot(g_fm, (r <= c if not reverse else r >= c).astype(F32),
                     preferred_element_type=F32, precision=lax.Precision.HIGHEST)
    ones_col = (c == 0).astype(BF16)
    for h in range(ML_HEADS):
        ic, fc = gbase + h, gbase + ML_HEADS + h
        b_col = b_cols[:, fc:fc + 1]
        b_row = b_rows[fc:fc + 1, :]
        li_row = g_fm[ic:ic + 1, :]
        li_col = g_tm[:, ic:ic + 1]
        m_prev = m_ref[h:h + 1, 0:1]
        log_d = jnp.where(causal, b_col - b_row + li_row, NEG_INF)
        m_inter = b_col + m_prev
        mj = jnp.maximum(m_inter, jnp.max(log_d, axis=1, keepdims=True))
        qh = q[:, h * ML_DIM:(h + 1) * ML_DIM]
        kh = k[:, h * ML_DIM:(h + 1) * ML_DIM]
        vext = jnp.concatenate([v[:, h * ML_DIM:(h + 1) * ML_DIM], ones_col], axis=1)
        qk = lax.dot_general(qh, kh, (((1,), (1,)), ((), ())), preferred_element_type=F32)
        w = (qk * jnp.exp(log_d - mj)).astype(BF16)
        state = s_ref[h]
        tot = (jnp.exp(m_inter - mj) * jnp.dot(qh, state.astype(BF16), preferred_element_type=F32)
               + jnp.dot(w, vext, preferred_element_type=F32))
        den = tot[:, ML_DIM:ML_DIM + 1]
        h_ref[0, :, h * ML_DIM:(h + 1) * ML_DIM] = (
            tot[:, :ML_DIM] / jnp.maximum(jnp.abs(den), jnp.exp(-mj)))
        b_last = b_col[last:last + 1, :]
        logw_col = b_last - b_col + li_col
        logw_row = b_last - b_row + li_row
        m_new = jnp.maximum(b_last + m_prev, jnp.max(logw_row, axis=1, keepdims=True))
        kw = (kh.astype(F32) * jnp.exp(logw_col - m_new)).astype(BF16)
        s_ref[h] = (jnp.exp(b_last + m_prev - m_new) * state
                    + lax.dot_general(kw, vext, (((0,), (0,)), ((), ())), preferred_element_type=F32))
        m_ref[h:h + 1, :] = jnp.broadcast_to(m_new, (1, LANES))


def _mlstm_kernel(qf_ref, kf_ref, vf_ref, gtf_ref, gff_ref, qb_ref, kb_ref, vb_ref, gtb_ref, gfb_ref,
                  s0_ref, m0_ref, hf_ref, hb_ref, sfin_ref, mfin_ref,
                  sf_ref, sb_ref, mf_ref, mb_ref):
    j = pl.program_id(1)

    @pl.when(j == 0)
    def _():
        sf_ref[...] = s0_ref[0]
        mf_ref[...] = m0_ref[0]
        sb_ref[...] = jnp.zeros(sb_ref.shape, F32)
        mb_ref[...] = jnp.zeros(mb_ref.shape, F32)

    _mlstm_chunk(False, qf_ref[0], kf_ref[0], vf_ref[0], gtf_ref[0], gff_ref[0], sf_ref, mf_ref, hf_ref)
    _mlstm_chunk(True, qb_ref[0], kb_ref[0], vb_ref[0], gtb_ref[0], gfb_ref[0], sb_ref, mb_ref, hb_ref)

    @pl.when(j == pl.num_programs(1) - 1)
    def _():
        sfin_ref[0] = sf_ref[...]
        mfin_ref[0] = mf_ref[...]


def _mlstm(qm, km, vm, g_tm, g_fm, s0, m0):
    b, s, _ = qm.shape
    nc = s // CHUNK
    fwd = lambda width: pl.BlockSpec((1, CHUNK, width), lambda bi, ci: (bi, ci, 0))
    bwd = lambda width: pl.BlockSpec((1, CHUNK, width), lambda bi, ci: (bi, nc - 1 - ci, 0))
    gf_fwd = pl.BlockSpec((1, N_GATES, CHUNK), lambda bi, ci: (bi, 0, ci))
    gf_bwd = pl.BlockSpec((1, N_GATES, CHUNK), lambda bi, ci: (bi, 0, nc - 1 - ci))
    st = pl.BlockSpec((1, ML_HEADS, ML_DIM, 2 * LANES), lambda bi, ci: (bi, 0, 0, 0))
    mx = pl.BlockSpec((1, SUBLANES, LANES), lambda bi, ci: (bi, 0, 0))
    return pl.pallas_call(
        _mlstm_kernel,
        grid=(b, nc),
        in_specs=[fwd(ML_WIDTH)] * 3 + [fwd(N_GATES), gf_fwd] + [bwd(ML_WIDTH)] * 3 + [bwd(N_GATES), gf_bwd]
        + [st, mx],
        out_specs=[fwd(ML_WIDTH), bwd(ML_WIDTH), st, mx],
        out_shape=[jax.ShapeDtypeStruct((b, s, ML_WIDTH), F32)] * 2
        + [jax.ShapeDtypeStruct((b, ML_HEADS, ML_DIM, 2 * LANES), F32),
           jax.ShapeDtypeStruct((b, SUBLANES, LANES), F32)],
        scratch_shapes=[pltpu.VMEM((ML_HEADS, ML_DIM, 2 * LANES), F32)] * 2
        + [pltpu.VMEM((SUBLANES, LANES), F32)] * 2,
        compiler_params=_cparams(("parallel", "arbitrary")),
        name="mlstm",
    )(qm, km, vm, g_tm, g_fm, qm, km, vm, g_tm, g_fm, s0, m0)


def _sorting_network(n):
    pairs = []
    p = 1
    while p < n:
        k = p
        while k >= 1:
            for j in range(k % p, n - k, 2 * k):
                for i in range(min(k, n - j - k)):
                    if (i + j) // (2 * p) == (i + j + k) // (2 * p):
                        pairs.append((i + j, i + j + k))
            k //= 2
        p *= 2
    return pairs


def _top_values_sorted(x, n):
    groups = [x[SUBLANES * g:SUBLANES * (g + 1)] for g in range(x.shape[0] // SUBLANES)]
    size = 1
    while size < len(groups):
        size *= 2
    groups += [jnp.full_like(groups[0], NEG_INF)] * (size - len(groups))
    for i, j in _sorting_network(size):
        groups[i], groups[j] = jnp.maximum(groups[i], groups[j]), jnp.minimum(groups[i], groups[j])
    vals = []
    for r in range(n):
        mx = jnp.max(groups[0], axis=0, keepdims=True)
        vals.append(mx)
        if r + 1 < n:
            hit = groups[0] == mx
            live = min(len(groups), n - 1 - r)
            for d in range(live):
                below = groups[d + 1] if d + 1 < len(groups) else NEG_INF
                groups[d] = jnp.where(hit, below, groups[d])
    return vals


def _count_greater(vals, x):
    n = len(vals)

    def pivot(decided, lo, size):
        if not decided:
            return vals[lo + size // 2 - 1]
        half = size // 2
        return jnp.where(decided[0], pivot(decided[1:], lo + half, half), pivot(decided[1:], lo, half))

    decided = []
    count = jnp.zeros_like(x)
    size = n
    while size >= 2:
        above = pivot(decided, 0, n) > x
        decided.append(above)
        count = count + jnp.where(above, float(size // 2), 0.0)
        size //= 2
    return jnp.where(vals[n - 1] > x, float(n), count)


def _bf16_pair_bits(x):
    u = lax.bitcast_convert_type(x.astype(BF16).astype(F32), jnp.uint32)
    return u | (u >> 16)


_CAND_PAIRS = [(a, b) for a in range(PEER_TOPK + 1) for b in range(PEER_TOPK + 1)
               if (a + 1) * (b + 1) <= PEER_TOPK + 1]
_CAND_ROWS = -(-len(_CAND_PAIRS) // SUBLANES) * SUBLANES


def _mix_kernel(tt, x_ref, att_ref, hf_ref, hb_ref, om_ref, gmh_ref, wout_ref, gffn_ref, wpqt_ref,
                k1_ref, k2_ref, h1_ref, xn2_ref, r2_ref, e2_ref, n1_ref, e1_ref, mix_ref, cand_ref):
    mix_ref[:, 0:ATT_WIDTH] = att_ref[0]
    for h in range(ML_HEADS):
        sl = slice(h * ML_DIM, (h + 1) * ML_DIM)
        hs = hf_ref[0, :, sl] + hb_ref[0, :, sl]
        ms = jnp.mean(hs * hs, axis=-1, keepdims=True)
        hn = hs * lax.rsqrt(ms + EPS) * gmh_ref[...]
        gate = 1.0 / (1.0 + jnp.exp(-om_ref[0, :, sl].astype(F32)))
        mix_ref[:, ATT_WIDTH + h * ML_DIM:ATT_WIDTH + (h + 1) * ML_DIM] = (hn * gate).astype(BF16)
    h1 = x_ref[0] + jnp.dot(mix_ref[...], wout_ref[...], preferred_element_type=F32)
    h1_ref[0] = h1
    ms = jnp.mean(h1 * h1, axis=-1, keepdims=True)
    xn2 = (h1 * lax.rsqrt(ms + EPS) * gffn_ref[...]).astype(BF16)
    xn2_ref[0] = xn2
    qt = lax.dot_general(wpqt_ref[...], xn2, (((1,), (1,)), ((), ())), preferred_element_type=F32)
    half = PEER_QDIM // 2
    cand_ref[...] = jnp.full(cand_ref.shape, NEG_INF, F32)
    for h in range(PEER_HEADS):
        q1 = qt[h * PEER_QDIM:h * PEER_QDIM + half].astype(BF16)
        q2 = qt[h * PEER_QDIM + half:(h + 1) * PEER_QDIM].astype(BF16)
        s1 = jnp.dot(k1_ref[h], q1, preferred_element_type=F32)
        s2 = jnp.dot(k2_ref[h], q2, preferred_element_type=F32)
        for sl in range(tt // LANES):
            lanes = slice(sl * LANES, (sl + 1) * LANES)
            a1 = s1[:, lanes]
            a2 = s2[:, lanes]
            v1 = _top_values_sorted(a1, PEER_TOPK + 1)
            v2 = _top_values_sorted(a2, PEER_TOPK + 1)
            rank2 = _count_greater(v2[:PEER_TOPK], a2)
            for idx, (a, b) in enumerate(_CAND_PAIRS):
                cand_ref[idx:idx + 1, lanes] = v1[a] + v2[b]
            cs = _top_values_sorted(cand_ref[:, lanes], PEER_TOPK + 1)
            z = jnp.zeros_like(cs[0])
            for i in range(PEER_TOPK):
                z = z + jnp.exp(cs[i] - cs[0])
            thr = 0.5 * (cs[PEER_TOPK - 1] + cs[PEER_TOPK])
            n1 = _count_greater(v2[:PEER_TOPK // 2], thr - a1)
            n_best = jnp.zeros_like(thr)
            for b in range(PEER_TOPK // 2, PEER_TOPK):
                n_best = n_best + jnp.where(v2[b] > thr - v1[0], 1.0, 0.0)
            n1 = n1 + jnp.where(a1 == v1[0], n_best, 0.0)
            r2_ref[0, h, :, lanes] = rank2.astype(BF16)
            n1_ref[0, h, :, lanes] = _bf16_pair_bits(n1)
            e1_ref[0, h, :, lanes] = _bf16_pair_bits(jnp.exp(a1 - v1[0]) * (0.5 / z))
            e2_ref[0, h, :, lanes] = jnp.exp(a2 - v2[0]).astype(BF16)


def _mix(x, att, hf, hb, om, w):
    b, s, d = x.shape
    tt = MIX_TILE
    nt = s // tt
    tspec = lambda width: pl.BlockSpec((1, tt, width), lambda bi, ti: (bi, ti, 0))
    sspec = pl.BlockSpec((1, PEER_HEADS, PEER_NKEYS, tt), lambda bi, ti: (bi, 0, 0, ti))
    return pl.pallas_call(
        functools.partial(_mix_kernel, tt),
        grid=(b, nt),
        in_specs=[tspec(d), tspec(ATT_WIDTH), tspec(ML_WIDTH), tspec(ML_WIDTH), tspec(ML_WIDTH),
                  _const_spec((1, ML_DIM)), _const_spec((d, d)), _const_spec((1, d)),
                  _const_spec((PEER_HEADS * PEER_QDIM, d)),
                  _const_spec((PEER_HEADS, PEER_NKEYS, PEER_QDIM // 2)),
                  _const_spec((PEER_HEADS, PEER_NKEYS, PEER_QDIM // 2))],
        out_specs=[tspec(d), tspec(d), sspec, sspec, sspec, sspec],
        out_shape=[jax.ShapeDtypeStruct((b, s, d), F32), jax.ShapeDtypeStruct((b, s, d), BF16),
                   jax.ShapeDtypeStruct((b, PEER_HEADS, PEER_NKEYS, s), BF16),
                   jax.ShapeDtypeStruct((b, PEER_HEADS, PEER_NKEYS, s), BF16),
                   jax.ShapeDtypeStruct((b, PEER_HEADS, PEER_NKEYS, s), jnp.uint32),
                   jax.ShapeDtypeStruct((b, PEER_HEADS, PEER_NKEYS, s), jnp.uint32)],
        scratch_shapes=[pltpu.VMEM((tt, d), BF16), pltpu.VMEM((_CAND_ROWS, tt), F32)],
        compiler_params=_cparams(("parallel", "parallel")),
        name="mix_route",
    )(x, att, hf, hb, om, w["g_mh"], w["w_out"], w["g_ffn"], w["w_pqt"], w["sub_k1"], w["sub_k2"])


def _gelu_x2(x):
    return x + x * lax.erf(x * (2.0 ** -0.5))


def _peer_kernel(tt, ng, xn2_ref, h1_ref, r2_ref, e2_ref, n1_ref, e1_ref, u_ref, vt_ref, y_ref,
                 a0_ref, a1_ref, w_ref, acc_ref):
    s = pl.program_id(1)
    rows = PEER_GROUP // PEER_NKEYS

    @pl.when(s == 0)
    def _():
        a0_ref[...] = jnp.zeros(a0_ref.shape, F32)
        a1_ref[...] = jnp.zeros(a1_ref.shape, F32)

    @pl.when((s < 1) | ((s + ng - 1) % ng == 0))
    def _():
        acc_ref[...] = jnp.zeros(acc_ref.shape, F32)

    g2 = (s + ng - 1) % ng

    def row(ref, h, c, lanes):
        words = jnp.broadcast_to(ref[0, h, pl.ds(c, 1), lanes], (SUBLANES, PEER_PASS_LANES))
        tile = pltpu.bitcast(words, BF16)
        return jnp.concatenate([tile] * (PEER_NKEYS // (2 * SUBLANES)), axis=0)

    def stages(a_cur, a_oth):
        zero = jnp.zeros((PEER_NKEYS, PEER_PASS_LANES), BF16)
        tok_half = tt // 2
        for kb in range(PEER_GROUP // PEER_VBLOCK):
            for cc in range(kb * PEER_VBLOCK // PEER_NKEYS, (kb + 1) * PEER_VBLOCK // PEER_NKEYS):
                c = g2 * rows + cc
                ex = slice(cc * PEER_NKEYS, (cc + 1) * PEER_NKEYS)
                for lb in range(tt // PEER_PASS_LANES):
                    lanes = pl.ds(lb * PEER_PASS_LANES, PEER_PASS_LANES)
                    wsum = zero
                    for h in range(PEER_HEADS):
                        sel = r2_ref[0, h, :, lanes] < row(n1_ref, h, c, lanes)
                        wsum = wsum + jnp.where(sel, e2_ref[0, h, :, lanes], zero) * row(e1_ref, h, c, lanes)
                    w_ref[ex, lanes] = wsum * _gelu_x2(a_oth[ex, lanes]).astype(BF16)
            if kb < 2:
                tk = slice(kb * tok_half, (kb + 1) * tok_half)
                a_cur[:, tk] = lax.dot_general(u_ref[...], xn2_ref[0, tk, :], (((1,), (1,)), ((), ())),
                                               preferred_element_type=F32)
            blk = slice(kb * PEER_VBLOCK, (kb + 1) * PEER_VBLOCK)
            acc_ref[...] += jnp.dot(vt_ref[0, :, blk], w_ref[blk, :], preferred_element_type=F32)

    @pl.when(s % 2 == 0)
    def _():
        stages(a0_ref, a1_ref)

    @pl.when(s % 2 == 1)
    def _():
        stages(a1_ref, a0_ref)

    @pl.when((s >= 1) & ((s + ng - 1) % ng == ng - 1))
    def _():
        y_ref[0] = h1_ref[0] + acc_ref[...].T


def _peer(xn2, h1, r2, e2, n1, e1, w):
    b, s, d = h1.shape
    tt = PEER_TILE
    nt = s // tt
    ng = (PEER_NKEYS * PEER_NKEYS) // PEER_GROUP
    tile = lambda si, lag: jnp.clip((si - lag) // ng, 0, nt - 1)
    tspec = lambda lag: pl.BlockSpec((1, tt, d), lambda bi, si: (bi, tile(si, lag), 0))
    sspec = pl.BlockSpec((1, PEER_HEADS, PEER_NKEYS, tt), lambda bi, si: (bi, 0, 0, tile(si, 1)))
    return pl.pallas_call(
        functools.partial(_peer_kernel, tt, ng),
        grid=(b, nt * ng + 1),
        in_specs=[tspec(0), tspec(1), sspec, sspec, sspec, sspec,
                  pl.BlockSpec((PEER_GROUP, d), lambda bi, si: (si % ng, 0)),
                  pl.BlockSpec((1, d, PEER_GROUP), lambda bi, si: ((si + ng - 1) % ng, 0, 0))],
        out_specs=tspec(1),
        out_shape=jax.ShapeDtypeStruct((b, s, d), F32),
        scratch_shapes=[pltpu.VMEM((PEER_GROUP, tt), F32)] * 2 + [pltpu.VMEM((PEER_GROUP, tt), BF16)]
        + [pltpu.VMEM((d, tt), F32)],
        compiler_params=_cparams(("parallel", "arbitrary")),
        name="peer_dense",
    )(xn2, h1, r2, e2, n1, e1, w["peer_u"], w["peer_vt"])


def _prep_weights(g_mix, w_in, b_gates, conv_w, conv_b, g_qn, g_kn, lam_q1, lam_k1, lam_q2, lam_k2,
                  g_sub, g_mh, w_out, g_ffn, w_pq, sub_k1, sub_k2, peer_u, peer_v):
    half = ROT_DIM // 2
    inv = ROPE_THETA ** (-jnp.arange(half, dtype=F32) * 2.0 / ROT_DIM)
    w_g = w_in[0][:, N_MAIN:]
    return {
        "g_mix": g_mix[0][None, :],
        "w_main": w_in[0][:, :N_MAIN].astype(BF16),
        "w_g": w_g.astype(BF16),
        "w_gt": w_g.T.astype(BF16),
        "b_g": b_gates[0][None, :],
        "b_gt": b_gates[0][:, None],
        "g_qk": jnp.concatenate([jnp.tile(g_qn[0], ATT_WIDTH // ATT_QK_DIM),
                                 jnp.tile(g_kn[0], ATT_WIDTH // ATT_QK_DIM)])[None, :],
        "inv_lane": jnp.tile(inv, LANES // half)[None, :],
        "conv_w": conv_w[0][:, 0, :],
        "conv_b": conv_b[0][None, :],
        "lam4": jnp.stack([lam_q1[0], lam_k1[0], lam_q2[0], lam_k2[0]]),
        "g_sub": g_sub[0][None, :],
        "g_mh": g_mh[0][None, :],
        "w_out": w_out[0].astype(BF16),
        "g_ffn": g_ffn[0][None, :],
        "w_pqt": w_pq[0].T.astype(BF16),
        "sub_k1": sub_k1[0].astype(BF16),
        "sub_k2": sub_k2[0].astype(BF16),
        "peer_u": peer_u[0].astype(BF16),
        "peer_vt": peer_v[0].astype(BF16).reshape(-1, PEER_GROUP, D_MODEL).transpose(0, 2, 1),
    }


def _trunk(x, meta, w):
    b, s, d = x.shape
    tt = PROJ_TILE
    assert s % tt == 0 and s % (2 * ATT_TK) == 0 and s % MIX_TILE == 0 and s % PEER_TILE == 0
    nt = s // tt
    xr = x.reshape(b, nt, tt, d)
    meta_tail = jnp.broadcast_to(meta[N_META - HALO:][None, None], (b, 1, HALO, d))
    prev = jnp.concatenate([meta_tail, xr[:, :-1, tt - HALO:, :]], axis=1)
    nxt = jnp.concatenate([xr[:, 1:, :HALO, :], jnp.zeros((b, 1, HALO, d), x.dtype)], axis=1)
    q, k, v, qm, km, vm, om, g_tm, g_fm = _proj(x, prev, nxt, N_META, tt, w)

    meta_b = jnp.broadcast_to(meta[None], (b, N_META, d))
    m_prev = jnp.zeros((b, 1, HALO, d), x.dtype)
    m_next = x[:, None, :HALO, :]
    _, k_m, v_m, qm_m, km_m, vm_m, _, gtm_m, gfm_m = _proj(meta_b, m_prev, m_next, 0, N_META, w)

    pad = CHUNK - N_META
    att = _attn(q, k, v, jnp.pad(k_m, ((0, 0), (0, pad), (0, 0))), jnp.pad(v_m, ((0, 0), (0, pad), (0, 0))),
                w["lam4"], w["g_sub"])

    front = lambda a: jnp.pad(a, ((0, 0), (pad, 0), (0, 0)))
    is_input_gate = (jnp.arange(N_GATES) // ML_HEADS) % 2 == 0
    pad_gate = jnp.where(is_input_gate, NEG_INF, 0.0).astype(F32)
    gtm_p = jnp.concatenate([jnp.broadcast_to(pad_gate[None, None, :], (b, pad, N_GATES)), gtm_m], axis=1)
    gfm_p = jnp.concatenate([jnp.broadcast_to(pad_gate[None, :, None], (b, N_GATES, pad)), gfm_m], axis=2)
    s_zero = jnp.zeros((b, ML_HEADS, ML_DIM, 2 * LANES), F32)
    m_zero = jnp.zeros((b, SUBLANES, LANES), F32)
    _, _, s0, m0 = _mlstm(front(qm_m), front(km_m), front(vm_m), gtm_p, gfm_p, s_zero, m_zero)
    hf, hb, _, _ = _mlstm(qm, km, vm, g_tm, g_fm, s0, m0)

    h1, xn2, r2, e2, n1, e1 = _mix(x, att, hf, hb, om, w)
    return _peer(xn2, h1, r2, e2, n1, e1, w)


def kernel(x_prompt, x_sample, meta, g_mix, w_in, b_gates, conv_w, conv_b, g_qn, g_kn, lam_q1, lam_k1,
           lam_q2, lam_k2, g_sub, g_mh, w_out, g_ffn, w_pq, sub_k1, sub_k2, peer_u, peer_v):
    w = _prep_weights(g_mix, w_in, b_gates, conv_w, conv_b, g_qn, g_kn, lam_q1, lam_k1, lam_q2, lam_k2,
                      g_sub, g_mh, w_out, g_ffn, w_pq, sub_k1, sub_k2, peer_u, peer_v)
    return (_trunk(x_prompt, meta, w), _trunk(x_sample, meta, w))
```

```python
import functools
import math

import jax
import jax.numpy as jnp
from jax import lax
from jax.experimental import pallas as pl
from jax.experimental.pallas import tpu as pltpu

F32 = jnp.float32
BF16 = jnp.bfloat16

D_MODEL = 1024
N_META = 16
ATT_HEADS = 4
ATT_QK_DIM = 64
ATT_V_DIM = 128
ATT_WIDTH = ATT_HEADS * ATT_V_DIM
ML_HEADS = 4
ML_DIM = 128
ML_WIDTH = ML_HEADS * ML_DIM
N_GATES = 4 * ML_HEADS
N_MAIN = 3 * ATT_WIDTH + 4 * ML_WIDTH
ROT_DIM = ATT_QK_DIM // 4
ROPE_THETA = 500000.0
CHUNK = 128
PEER_HEADS = 8
PEER_NKEYS = 128
PEER_QDIM = 256
PEER_TOPK = 16
EPS = 1e-6
LAM_INIT = 0.8 - 0.6 * math.exp(-0.3 * 0)
Q_SCALE = ATT_QK_DIM ** -0.5 * math.log2(math.e)

LANES = 128
SUBLANES = 8
HALO = SUBLANES
VMEM_LIMIT = 56 * 1024 * 1024

PROJ_TILE = 512
ATT_TQ = 512
ATT_TK = 1024
MIX_TILE = 512
PEER_TILE = 512
PEER_GROUP = 2048
PEER_PASS_LANES = 256
PEER_VBLOCK = 256
NEG_INF = float("-inf")


def _cparams(sem):
    return pltpu.CompilerParams(dimension_semantics=sem, vmem_limit_bytes=VMEM_LIMIT)


def _const_spec(shape):
    nd = len(shape)
    return pl.BlockSpec(shape, lambda *_: (0,) * nd)


def _log_sigmoid(x):
    return -(jnp.maximum(-x, 0.0) + jnp.log1p(jnp.exp(-jnp.abs(x))))


def _proj_kernel(pos_base, tt,
                 x_ref, prev_ref, next_ref, gmix_ref, wmain_ref, wg_ref, wgt_ref, bg_ref, bgt_ref,
                 gqk_ref, inv_ref, convw_ref, convb_ref,
                 q_ref, k_ref, v_ref, qm_ref, km_ref, vm_ref, om_ref, gtm_ref, gfm_ref,
                 xe_ref, zc_ref):
    i = pl.program_id(1)
    xe_ref[0:HALO, :] = prev_ref[0, 0]
    xe_ref[HALO:HALO + tt, :] = x_ref[0]
    xe_ref[HALO + tt:2 * HALO + tt, :] = next_ref[0, 0]
    xe = xe_ref[...]
    ms = jnp.mean(xe * xe, axis=-1, keepdims=True)
    xn_ext = (xe * lax.rsqrt(ms + EPS) * gmix_ref[...]).astype(BF16)
    xn = xn_ext[HALO:HALO + tt]

    def mm(a, lo, hi):
        return jnp.dot(a, wmain_ref[:, lo:hi], preferred_element_type=F32)

    zqk = mm(xn, 0, 2 * ATT_WIDTH)
    sq = (zqk * zqk).astype(BF16)
    r = lax.broadcasted_iota(jnp.int32, (2 * LANES, 2 * LANES), 0) // ATT_QK_DIM
    c = lax.broadcasted_iota(jnp.int32, (2 * LANES, 2 * LANES), 1) // ATT_QK_DIM
    seg_ones = (r == c).astype(BF16)
    lane = lax.broadcasted_iota(jnp.int32, (tt, LANES), 1)
    l64 = lane % ATT_QK_DIM
    row = lax.broadcasted_iota(jnp.int32, (tt, LANES), 0)
    pos = (pos_base + i * tt + row).astype(F32)
    ang = pos * inv_ref[...]
    cos_a = jnp.cos(ang)
    sin_a = jnp.sin(ang)
    half = ROT_DIM // 2
    cos_m = jnp.where(l64 < ROT_DIM, cos_a, 1.0)
    sin_lo = jnp.where(l64 < half, -sin_a, 0.0)
    sin_hi = jnp.where((l64 >= half) & (l64 < ROT_DIM), sin_a, 0.0)
    for cb in range(4):
        lo = cb * 2 * LANES
        ss = jnp.dot(sq[:, lo:lo + 2 * LANES], seg_ones, preferred_element_type=F32)
        rn = lax.rsqrt(ss * (1.0 / ATT_QK_DIM) + EPS)
        qn = zqk[:, lo:lo + 2 * LANES] * rn * gqk_ref[:, lo:lo + 2 * LANES]
        for sb in range(2):
            xs = qn[:, sb * LANES:(sb + 1) * LANES]
            rot = (xs * cos_m + pltpu.roll(xs, LANES - half, 1) * sin_lo
                   + pltpu.roll(xs, half, 1) * sin_hi)
            col = lo + sb * LANES
            if col < ATT_WIDTH:
                q_ref[0, :, col:col + LANES] = (rot * Q_SCALE).astype(BF16)
            else:
                k_ref[0, :, col - ATT_WIDTH:col - ATT_WIDTH + LANES] = rot.astype(BF16)

    va = mm(xn, 2 * ATT_WIDTH, 3 * ATT_WIDTH).astype(BF16)
    for h in range(ATT_HEADS):
        v_ref[0, :, 2 * h * LANES:(2 * h + 1) * LANES] = va[:, h * ATT_V_DIM:(h + 1) * ATT_V_DIM]
        v_ref[0, :, (2 * h + 1) * LANES:(2 * h + 2) * LANES] = jnp.ones((tt, LANES), BF16)

    base = 3 * ATT_WIDTH
    zc_ref[...] = mm(xn_ext, base, base + 2 * ML_WIDTH)
    y = (convw_ref[0:1, :] * zc_ref[pl.ds(HALO - 1, tt), :]
         + convw_ref[1:2, :] * zc_ref[pl.ds(HALO, tt), :]
         + convw_ref[2:3, :] * zc_ref[pl.ds(HALO + 1, tt), :]
         + convb_ref[...])
    act = y * (1.0 / (1.0 + jnp.exp(-y)))
    qm_ref[0] = act[:, :ML_WIDTH].astype(BF16)
    km_ref[0] = (act[:, ML_WIDTH:] * (ML_DIM ** -0.5)).astype(BF16)
    vm_ref[0] = mm(xn, base + 2 * ML_WIDTH, base + 3 * ML_WIDTH).astype(BF16)
    om_ref[0] = mm(xn, base + 3 * ML_WIDTH, base + 4 * ML_WIDTH).astype(BF16)

    g_tm = jnp.dot(xn, wg_ref[...], preferred_element_type=F32) + bg_ref[...]
    g_fm = lax.dot_general(wgt_ref[...], xn, (((1,), (1,)), ((), ())),
                           preferred_element_type=F32) + bgt_ref[...]
    col_tm = lax.broadcasted_iota(jnp.int32, (tt, N_GATES), 1)
    row_fm = lax.broadcasted_iota(jnp.int32, (N_GATES, tt), 0)
    gtm_ref[0] = jnp.where((col_tm // ML_HEADS) % 2 == 1, _log_sigmoid(g_tm), g_tm)
    gfm_ref[0] = jnp.where((row_fm // ML_HEADS) % 2 == 1, _log_sigmoid(g_fm), g_fm)


def _proj(x, prev, nxt, pos_base, tt, w):
    b, s, d = x.shape
    nt = s // tt
    tok = lambda width, dt: jax.ShapeDtypeStruct((b, s, width), dt)
    out_shape = ([tok(ATT_WIDTH, BF16)] * 2 + [tok(2 * ATT_WIDTH, BF16)] + [tok(ML_WIDTH, BF16)] * 4
                 + [tok(N_GATES, F32), jax.ShapeDtypeStruct((b, N_GATES, s), F32)])
    tspec = lambda width: pl.BlockSpec((1, tt, width), lambda bi, ti: (bi, ti, 0))
    hspec = pl.BlockSpec((1, 1, HALO, d), lambda bi, ti: (bi, ti, 0, 0))
    return pl.pallas_call(
        functools.partial(_proj_kernel, pos_base, tt),
        grid=(b, nt),
        in_specs=[tspec(d), hspec, hspec, _const_spec((1, d)), _const_spec((d, N_MAIN)),
                  _const_spec((d, N_GATES)), _const_spec((N_GATES, d)), _const_spec((1, N_GATES)),
                  _const_spec((N_GATES, 1)), _const_spec((1, 2 * ATT_WIDTH)), _const_spec((1, LANES)),
                  _const_spec((3, 2 * ML_WIDTH)), _const_spec((1, 2 * ML_WIDTH))],
        out_specs=[tspec(ATT_WIDTH)] * 2 + [tspec(2 * ATT_WIDTH)] + [tspec(ML_WIDTH)] * 4
        + [tspec(N_GATES), pl.BlockSpec((1, N_GATES, tt), lambda bi, ti: (bi, 0, ti))],
        out_shape=out_shape,
        scratch_shapes=[pltpu.VMEM((tt + 2 * HALO, d), F32), pltpu.VMEM((tt + 2 * HALO, 2 * ML_WIDTH), F32)],
        compiler_params=_cparams(("parallel", "parallel")),
        name=f"proj_t{tt}",
    )(x, prev, nxt, w["g_mix"], w["w_main"], w["w_g"], w["w_gt"], w["b_g"], w["b_gt"],
      w["g_qk"], w["inv_lane"], w["conv_w"], w["conv_b"])


def _attn_kernel(tq, tk, nk,
                 q_ref, k_ref, v_ref, kt_ref, vt_ref, lam_ref, gsub_ref, o_ref,
                 qq_ref, m_ref, acc_ref, sa_ref, sb_ref):
    q = q_ref[0]
    lane = lax.broadcasted_iota(jnp.int32, (tq, LANES), 1)
    zero = jnp.zeros_like(q)
    qq_ref[0:tq, :] = jnp.where(lane < ATT_QK_DIM, q, zero)
    qq_ref[tq:2 * tq, :] = jnp.where(lane >= ATT_QK_DIM, q, zero)
    m_ref[...] = jnp.full(m_ref.shape, NEG_INF, F32)
    acc_ref[...] = jnp.zeros(acc_ref.shape, F32)

    def scores(kb):
        return lax.dot_general(qq_ref[...], kb, (((1,), (1,)), ((), ())), preferred_element_type=F32)

    def update(s, vb):
        m_prev = m_ref[...]
        m_new = jnp.maximum(m_prev, jnp.max(s, axis=1, keepdims=True))
        alpha = jnp.exp2(m_prev - m_new)
        p = jnp.exp2(s - m_new[:, 0:1]).astype(BF16)
        pv = jnp.dot(p, vb, preferred_element_type=F32)
        acc_ref[:, 0:LANES] = alpha * acc_ref[:, 0:LANES] + pv[:, 0:LANES]
        acc_ref[:, LANES:2 * LANES] = alpha * acc_ref[:, LANES:2 * LANES] + pv[:, LANES:2 * LANES]
        m_ref[...] = m_new

    def kblock(j):
        return k_ref[0, pl.ds(pl.multiple_of(j * tk, tk), tk), :]

    def vblock(j):
        return v_ref[0, pl.ds(pl.multiple_of(j * tk, tk), tk), :]

    sa_ref[...] = scores(kblock(0))

    def body(i, carry):
        j = 2 * i
        sb_ref[...] = scores(kblock(j + 1))
        update(sa_ref[...], vblock(j))
        sa_ref[...] = scores(kblock(j + 2))
        update(sb_ref[...], vblock(j + 1))
        return carry

    lax.fori_loop(0, nk // 2 - 1, body, 0)
    sb_ref[...] = scores(kblock(nk - 1))
    update(sa_ref[...], vblock(nk - 2))
    col = lax.broadcasted_iota(jnp.int32, (2 * tq, LANES), 1)
    s_meta = jnp.where(col < N_META, scores(kt_ref[0]), NEG_INF)
    update(sb_ref[...], vblock(nk - 1))
    update(s_meta, vt_ref[0])

    o1 = acc_ref[0:tq, 0:LANES] / acc_ref[0:tq, LANES:2 * LANES]
    o2 = acc_ref[tq:2 * tq, 0:LANES] / acc_ref[tq:2 * tq, LANES:2 * LANES]
    lam = (jnp.exp(jnp.sum(lam_ref[0:1, :] * lam_ref[1:2, :], axis=1, keepdims=True))
           - jnp.exp(jnp.sum(lam_ref[2:3, :] * lam_ref[3:4, :], axis=1, keepdims=True)) + LAM_INIT)
    o = o1 - lam * o2
    ms = jnp.mean(o * o, axis=-1, keepdims=True)
    o_ref[0] = (o * lax.rsqrt(ms + EPS) * gsub_ref[...] * (1.0 - LAM_INIT)).astype(BF16)


def _attn(q, k, v, kt, vt, lam4, g_sub):
    b, s, _ = q.shape
    tq, tk = ATT_TQ, ATT_TK
    kspec = lambda rows: pl.BlockSpec((1, rows, LANES), lambda bi, hi, qi: (bi, 0, hi))
    vspec = lambda rows: pl.BlockSpec((1, rows, 2 * LANES), lambda bi, hi, qi: (bi, 0, hi))
    return pl.pallas_call(
        functools.partial(_attn_kernel, tq, tk, s // tk),
        grid=(b, ATT_HEADS, s // tq),
        in_specs=[pl.BlockSpec((1, tq, LANES), lambda bi, hi, qi: (bi, qi, hi)),
                  kspec(s), vspec(s), kspec(LANES), vspec(LANES),
                  _const_spec((4, ATT_QK_DIM)), _const_spec((1, ATT_V_DIM))],
        out_specs=pl.BlockSpec((1, tq, LANES), lambda bi, hi, qi: (bi, qi, hi)),
        out_shape=jax.ShapeDtypeStruct((b, s, ATT_WIDTH), BF16),
        scratch_shapes=[pltpu.VMEM((2 * tq, LANES), BF16), pltpu.VMEM((2 * tq, LANES), F32),
                        pltpu.VMEM((2 * tq, 2 * LANES), F32),
                        pltpu.VMEM((2 * tq, tk), F32), pltpu.VMEM((2 * tq, tk), F32)],
        compiler_params=_cparams(("parallel", "parallel", "parallel")),
        name="diff_attn",
    )(q, k, v, kt, vt, lam4, g_sub)


def _mlstm_chunks(dirs):
    r = lax.broadcasted_iota(jnp.int32, (CHUNK, CHUNK), 0)
    c = lax.broadcasted_iota(jnp.int32, (CHUNK, CHUNK), 1)
    ones_col = (c == 0).astype(BF16)
    chains = []
    for reverse, q, k, v, g_tm, g_fm, s_ref, m_ref, h_ref in dirs:
        causal = (c >= r) if reverse else (c <= r)
        last = 0 if reverse else CHUNK - 1
        gbase = 2 * ML_HEADS if reverse else 0
        b_cols = jnp.dot(causal.astype(F32), g_tm, preferred_element_type=F32,
                         precision=lax.Precision.HIGHEST)
        b_rows = jnp.dot(g_fm, ((r >= c) if reverse else (r <= c)).astype(F32),
                         preferred_element_type=F32, precision=lax.Precision.HIGHEST)
        for h in range(ML_HEADS):
            fc = gbase + ML_HEADS + h
            sl = slice(h * ML_DIM, (h + 1) * ML_DIM)
            chains.append(dict(
                causal=causal, last=last, h=h, sl=sl, s_ref=s_ref, m_ref=m_ref, h_ref=h_ref,
                q=q[:, sl], k=k[:, sl], v=v[:, sl],
                b_col=b_cols[:, fc:fc + 1], b_row=b_rows[fc:fc + 1, :],
                li_row=g_fm[gbase + h:gbase + h + 1, :], li_col=g_tm[:, gbase + h:gbase + h + 1]))
    n = range(len(chains))
    ch = chains
    m_prev = [ch[i]["m_ref"][ch[i]["h"]:ch[i]["h"] + 1, 0:1] for i in n]
    log_d = [jnp.where(ch[i]["causal"], ch[i]["b_col"] - ch[i]["b_row"] + ch[i]["li_row"], NEG_INF) for i in n]
    m_inter = [ch[i]["b_col"] + m_prev[i] for i in n]
    mj = [jnp.maximum(m_inter[i], jnp.max(log_d[i], axis=1, keepdims=True)) for i in n]
    vext = [jnp.concatenate([ch[i]["v"], ones_col], axis=1) for i in n]
    qk = [lax.dot_general(ch[i]["q"], ch[i]["k"], (((1,), (1,)), ((), ())), preferred_element_type=F32)
          for i in n]
    state = [ch[i]["s_ref"][ch[i]["h"]] for i in n]
    inter = [jnp.dot(ch[i]["q"], state[i].astype(BF16), preferred_element_type=F32) for i in n]
    b_last = [ch[i]["b_col"][ch[i]["last"]:ch[i]["last"] + 1, :] for i in n]
    m_new = [jnp.maximum(b_last[i] + m_prev[i],
                         jnp.max(b_last[i] - ch[i]["b_row"] + ch[i]["li_row"], axis=1, keepdims=True)) for i in n]
    w = [(qk[i] * jnp.exp(log_d[i] - mj[i])).astype(BF16) for i in n]
    kw = [(ch[i]["k"].astype(F32)
           * jnp.exp(b_last[i] - ch[i]["b_col"] + ch[i]["li_col"] - m_new[i])).astype(BF16) for i in n]
    tot = [jnp.exp(m_inter[i] - mj[i]) * inter[i] + jnp.dot(w[i], vext[i], preferred_element_type=F32)
           for i in n]
    upd = [lax.dot_general(kw[i], vext[i], (((0,), (0,)), ((), ())), preferred_element_type=F32) for i in n]
    for i in n:
        den = tot[i][:, ML_DIM:ML_DIM + 1]
        ch[i]["h_ref"][0, :, ch[i]["sl"]] = tot[i][:, :ML_DIM] / jnp.maximum(jnp.abs(den), jnp.exp(-mj[i]))
    for i in n:
        h = ch[i]["h"]
        ch[i]["s_ref"][h] = jnp.exp(b_last[i] + m_prev[i] - m_new[i]) * state[i] + upd[i]
        ch[i]["m_ref"][h:h + 1, :] = jnp.broadcast_to(m_new[i], (1, LANES))


def _mlstm_kernel(qf_ref, kf_ref, vf_ref, gtf_ref, gff_ref, qb_ref, kb_ref, vb_ref, gtb_ref, gfb_ref,
                  s0_ref, m0_ref, hf_ref, hb_ref, sfin_ref, mfin_ref,
                  sf_ref, sb_ref, mf_ref, mb_ref):
    j = pl.program_id(1)

    @pl.when(j == 0)
    def _():
        sf_ref[...] = s0_ref[0]
        mf_ref[...] = m0_ref[0]
        sb_ref[...] = jnp.zeros(sb_ref.shape, F32)
        mb_ref[...] = jnp.zeros(mb_ref.shape, F32)

    _mlstm_chunks([(False, qf_ref[0], kf_ref[0], vf_ref[0], gtf_ref[0], gff_ref[0], sf_ref, mf_ref, hf_ref)])
    _mlstm_chunks([(True, qb_ref[0], kb_ref[0], vb_ref[0], gtb_ref[0], gfb_ref[0], sb_ref, mb_ref, hb_ref)])

    @pl.when(j == pl.num_programs(1) - 1)
    def _():
        sfin_ref[0] = sf_ref[...]
        mfin_ref[0] = mf_ref[...]


def _mlstm(qm, km, vm, g_tm, g_fm, s0, m0):
    b, s, _ = qm.shape
    nc = s // CHUNK
    fwd = lambda width: pl.BlockSpec((1, CHUNK, width), lambda bi, ci: (bi, ci, 0))
    bwd = lambda width: pl.BlockSpec((1, CHUNK, width), lambda bi, ci: (bi, nc - 1 - ci, 0))
    gf_fwd = pl.BlockSpec((1, N_GATES, CHUNK), lambda bi, ci: (bi, 0, ci))
    gf_bwd = pl.BlockSpec((1, N_GATES, CHUNK), lambda bi, ci: (bi, 0, nc - 1 - ci))
    st = pl.BlockSpec((1, ML_HEADS, ML_DIM, 2 * LANES), lambda bi, ci: (bi, 0, 0, 0))
    mx = pl.BlockSpec((1, SUBLANES, LANES), lambda bi, ci: (bi, 0, 0))
    return pl.pallas_call(
        _mlstm_kernel,
        grid=(b, nc),
        in_specs=[fwd(ML_WIDTH)] * 3 + [fwd(N_GATES), gf_fwd] + [bwd(ML_WIDTH)] * 3 + [bwd(N_GATES), gf_bwd]
        + [st, mx],
        out_specs=[fwd(ML_WIDTH), bwd(ML_WIDTH), st, mx],
        out_shape=[jax.ShapeDtypeStruct((b, s, ML_WIDTH), F32)] * 2
        + [jax.ShapeDtypeStruct((b, ML_HEADS, ML_DIM, 2 * LANES), F32),
           jax.ShapeDtypeStruct((b, SUBLANES, LANES), F32)],
        scratch_shapes=[pltpu.VMEM((ML_HEADS, ML_DIM, 2 * LANES), F32)] * 2
        + [pltpu.VMEM((SUBLANES, LANES), F32)] * 2,
        compiler_params=_cparams(("parallel", "arbitrary")),
        name="mlstm",
    )(qm, km, vm, g_tm, g_fm, qm, km, vm, g_tm, g_fm, s0, m0)


def _sorting_network(n):
    pairs = []
    p = 1
    while p < n:
        k = p
        while k >= 1:
            for j in range(k % p, n - k, 2 * k):
                for i in range(min(k, n - j - k)):
                    if (i + j) // (2 * p) == (i + j + k) // (2 * p):
                        pairs.append((i + j, i + j + k))
            k //= 2
        p *= 2
    return pairs


def _top_values_sorted(x, n):
    groups = [x[SUBLANES * g:SUBLANES * (g + 1)] for g in range(x.shape[0] // SUBLANES)]
    size = 1
    while size < len(groups):
        size *= 2
    groups += [jnp.full_like(groups[0], NEG_INF)] * (size - len(groups))
    for i, j in _sorting_network(size):
        groups[i], groups[j] = jnp.maximum(groups[i], groups[j]), jnp.minimum(groups[i], groups[j])
    vals = []
    for r in range(n):
        mx = jnp.max(groups[0], axis=0, keepdims=True)
        vals.append(mx)
        if r + 1 < n:
            hit = groups[0] == mx
            live = min(len(groups), n - 1 - r)
            for d in range(live):
                below = groups[d + 1] if d + 1 < len(groups) else NEG_INF
                groups[d] = jnp.where(hit, below, groups[d])
    return vals


def _count_greater(vals, x):
    n = len(vals)

    def pivot(decided, lo, size):
        if not decided:
            return vals[lo + size // 2 - 1]
        half = size // 2
        return jnp.where(decided[0], pivot(decided[1:], lo + half, half), pivot(decided[1:], lo, half))

    decided = []
    count = jnp.zeros_like(x)
    size = n
    while size >= 2:
        above = pivot(decided, 0, n) > x
        decided.append(above)
        count = count + jnp.where(above, float(size // 2), 0.0)
        size //= 2
    return jnp.where(vals[n - 1] > x, float(n), count)


def _bf16_pair_bits(x):
    u = lax.bitcast_convert_type(x.astype(BF16).astype(F32), jnp.uint32)
    return u | (u >> 16)


_CAND_PAIRS = [(a, b) for a in range(PEER_TOPK + 1) for b in range(PEER_TOPK + 1)
               if (a + 1) * (b + 1) <= PEER_TOPK + 1]
_CAND_ROWS = -(-len(_CAND_PAIRS) // SUBLANES) * SUBLANES


def _mix_kernel(tt, x_ref, att_ref, hf_ref, hb_ref, om_ref, gmh_ref, wout_ref, gffn_ref, wpqt_ref,
                k1_ref, k2_ref, h1_ref, xn2_ref, r2_ref, e2_ref, n1_ref, e1_ref, mix_ref, cand_ref):
    mix_ref[:, 0:ATT_WIDTH] = att_ref[0]
    for h in range(ML_HEADS):
        sl = slice(h * ML_DIM, (h + 1) * ML_DIM)
        hs = hf_ref[0, :, sl] + hb_ref[0, :, sl]
        ms = jnp.mean(hs * hs, axis=-1, keepdims=True)
        hn = hs * lax.rsqrt(ms + EPS) * gmh_ref[...]
        gate = 1.0 / (1.0 + jnp.exp(-om_ref[0, :, sl].astype(F32)))
        mix_ref[:, ATT_WIDTH + h * ML_DIM:ATT_WIDTH + (h + 1) * ML_DIM] = (hn * gate).astype(BF16)
    h1 = x_ref[0] + jnp.dot(mix_ref[...], wout_ref[...], preferred_element_type=F32)
    h1_ref[0] = h1
    ms = jnp.mean(h1 * h1, axis=-1, keepdims=True)
    xn2 = (h1 * lax.rsqrt(ms + EPS) * gffn_ref[...]).astype(BF16)
    xn2_ref[0] = xn2
    qt = lax.dot_general(wpqt_ref[...], xn2, (((1,), (1,)), ((), ())), preferred_element_type=F32)
    half = PEER_QDIM // 2
    cand_ref[...] = jnp.full(cand_ref.shape, NEG_INF, F32)
    for h in range(PEER_HEADS):
        q1 = qt[h * PEER_QDIM:h * PEER_QDIM + half].astype(BF16)
        q2 = qt[h * PEER_QDIM + half:(h + 1) * PEER_QDIM].astype(BF16)
        s1 = jnp.dot(k1_ref[h], q1, preferred_element_type=F32)
        s2 = jnp.dot(k2_ref[h], q2, preferred_element_type=F32)
        for sl in range(tt // LANES):
            lanes = slice(sl * LANES, (sl + 1) * LANES)
            a1 = s1[:, lanes]
            a2 = s2[:, lanes]
            v1 = _top_values_sorted(a1, PEER_TOPK + 1)
            v2 = _top_values_sorted(a2, PEER_TOPK + 1)
            rank2 = _count_greater(v2[:PEER_TOPK], a2)
            for idx, (a, b) in enumerate(_CAND_PAIRS):
                cand_ref[idx:idx + 1, lanes] = v1[a] + v2[b]
            cs = _top_values_sorted(cand_ref[:, lanes], PEER_TOPK + 1)
            z = jnp.zeros_like(cs[0])
            for i in range(PEER_TOPK):
                z = z + jnp.exp(cs[i] - cs[0])
            thr = 0.5 * (cs[PEER_TOPK - 1] + cs[PEER_TOPK])
            n1 = _count_greater(v2[:PEER_TOPK // 2], thr - a1)
            n_best = jnp.zeros_like(thr)
            for b in range(PEER_TOPK // 2, PEER_TOPK):
                n_best = n_best + jnp.where(v2[b] > thr - v1[0], 1.0, 0.0)
            n1 = n1 + jnp.where(a1 == v1[0], n_best, 0.0)
            r2_ref[0, h, :, lanes] = rank2.astype(BF16)
            n1_ref[0, h, :, lanes] = _bf16_pair_bits(n1)
            e1_ref[0, h, :, lanes] = _bf16_pair_bits(jnp.exp(a1 - v1[0]) * (0.5 / z))
            e2_ref[0, h, :, lanes] = jnp.exp(a2 - v2[0]).astype(BF16)


def _mix(x, att, hf, hb, om, w):
    b, s, d = x.shape
    tt = MIX_TILE
    nt = s // tt
    tspec = lambda width: pl.BlockSpec((1, tt, width), lambda bi, ti: (bi, ti, 0))
    sspec = pl.BlockSpec((1, PEER_HEADS, PEER_NKEYS, tt), lambda bi, ti: (bi, 0, 0, ti))
    return pl.pallas_call(
        functools.partial(_mix_kernel, tt),
        grid=(b, nt),
        in_specs=[tspec(d), tspec(ATT_WIDTH), tspec(ML_WIDTH), tspec(ML_WIDTH), tspec(ML_WIDTH),
                  _const_spec((1, ML_DIM)), _const_spec((d, d)), _const_spec((1, d)),
                  _const_spec((PEER_HEADS * PEER_QDIM, d)),
                  _const_spec((PEER_HEADS, PEER_NKEYS, PEER_QDIM // 2)),
                  _const_spec((PEER_HEADS, PEER_NKEYS, PEER_QDIM // 2))],
        out_specs=[tspec(d), tspec(d), sspec, sspec, sspec, sspec],
        out_shape=[jax.ShapeDtypeStruct((b, s, d), F32), jax.ShapeDtypeStruct((b, s, d), BF16),
                   jax.ShapeDtypeStruct((b, PEER_HEADS, PEER_NKEYS, s), BF16),
                   jax.ShapeDtypeStruct((b, PEER_HEADS, PEER_NKEYS, s), BF16),
                   jax.ShapeDtypeStruct((b, PEER_HEADS, PEER_NKEYS, s), jnp.uint32),
                   jax.ShapeDtypeStruct((b, PEER_HEADS, PEER_NKEYS, s), jnp.uint32)],
        scratch_shapes=[pltpu.VMEM((tt, d), BF16), pltpu.VMEM((_CAND_ROWS, tt), F32)],
        compiler_params=_cparams(("parallel", "parallel")),
        name="mix_route",
    )(x, att, hf, hb, om, w["g_mh"], w["w_out"], w["g_ffn"], w["w_pqt"], w["sub_k1"], w["sub_k2"])


def _gelu_x2(x):
    return x + x * lax.erf(x * (2.0 ** -0.5))


def _peer_kernel(tt, ng, xn2_ref, h1_ref, r2_ref, e2_ref, n1_ref, e1_ref, u_ref, vt_ref, y_ref,
                 a0_ref, a1_ref, w_ref, acc_ref):
    s = pl.program_id(1)
    rows = PEER_GROUP // PEER_NKEYS

    @pl.when(s == 0)
    def _():
        a0_ref[...] = jnp.zeros(a0_ref.shape, F32)
        a1_ref[...] = jnp.zeros(a1_ref.shape, F32)

    @pl.when((s < 1) | ((s + ng - 1) % ng == 0))
    def _():
        acc_ref[...] = jnp.zeros(acc_ref.shape, F32)

    g2 = (s + ng - 1) % ng

    def row(ref, h, c, lanes):
        words = jnp.broadcast_to(ref[0, h, pl.ds(c, 1), lanes], (SUBLANES, PEER_PASS_LANES))
        tile = pltpu.bitcast(words, BF16)
        return jnp.concatenate([tile] * (PEER_NKEYS // (2 * SUBLANES)), axis=0)

    def stages(a_cur, a_oth):
        zero = jnp.zeros((PEER_NKEYS, PEER_PASS_LANES), BF16)
        tok_half = tt // 2
        for kb in range(PEER_GROUP // PEER_VBLOCK):
            for cc in range(kb * PEER_VBLOCK // PEER_NKEYS, (kb + 1) * PEER_VBLOCK // PEER_NKEYS):
                c = g2 * rows + cc
                ex = slice(cc * PEER_NKEYS, (cc + 1) * PEER_NKEYS)
                for lb in range(tt // PEER_PASS_LANES):
                    lanes = pl.ds(lb * PEER_PASS_LANES, PEER_PASS_LANES)
                    wsum = zero
                    for h in range(PEER_HEADS):
                        sel = r2_ref[0, h, :, lanes] < row(n1_ref, h, c, lanes)
                        wsum = wsum + jnp.where(sel, e2_ref[0, h, :, lanes], zero) * row(e1_ref, h, c, lanes)
                    w_ref[ex, lanes] = wsum * _gelu_x2(a_oth[ex, lanes]).astype(BF16)
            if kb < 2:
                tk = slice(kb * tok_half, (kb + 1) * tok_half)
                a_cur[:, tk] = lax.dot_general(u_ref[...], xn2_ref[0, tk, :], (((1,), (1,)), ((), ())),
                                               preferred_element_type=F32)
            blk = slice(kb * PEER_VBLOCK, (kb + 1) * PEER_VBLOCK)
            acc_ref[...] += jnp.dot(vt_ref[0, :, blk], w_ref[blk, :], preferred_element_type=F32)

    @pl.when(s % 2 == 0)
    def _():
        stages(a0_ref, a1_ref)

    @pl.when(s % 2 == 1)
    def _():
        stages(a1_ref, a0_ref)

    @pl.when((s >= 1) & ((s + ng - 1) % ng == ng - 1))
    def _():
        y_ref[0] = h1_ref[0] + acc_ref[...].T


def _peer(xn2, h1, r2, e2, n1, e1, w):
    b, s, d = h1.shape
    tt = PEER_TILE
    nt = s // tt
    ng = (PEER_NKEYS * PEER_NKEYS) // PEER_GROUP
    tile = lambda si, lag: jnp.clip((si - lag) // ng, 0, nt - 1)
    tspec = lambda lag: pl.BlockSpec((1, tt, d), lambda bi, si: (bi, tile(si, lag), 0))
    sspec = pl.BlockSpec((1, PEER_HEADS, PEER_NKEYS, tt), lambda bi, si: (bi, 0, 0, tile(si, 1)))
    return pl.pallas_call(
        functools.partial(_peer_kernel, tt, ng),
        grid=(b, nt * ng + 1),
        in_specs=[tspec(0), tspec(1), sspec, sspec, sspec, sspec,
                  pl.BlockSpec((PEER_GROUP, d), lambda bi, si: (si % ng, 0)),
                  pl.BlockSpec((1, d, PEER_GROUP), lambda bi, si: ((si + ng - 1) % ng, 0, 0))],
        out_specs=tspec(1),
        out_shape=jax.ShapeDtypeStruct((b, s, d), F32),
        scratch_shapes=[pltpu.VMEM((PEER_GROUP, tt), F32)] * 2 + [pltpu.VMEM((PEER_GROUP, tt), BF16)]
        + [pltpu.VMEM((d, tt), F32)],
        compiler_params=_cparams(("parallel", "arbitrary")),
        name="peer_dense",
    )(xn2, h1, r2, e2, n1, e1, w["peer_u"], w["peer_vt"])


def _prep_weights(g_mix, w_in, b_gates, conv_w, conv_b, g_qn, g_kn, lam_q1, lam_k1, lam_q2, lam_k2,
                  g_sub, g_mh, w_out, g_ffn, w_pq, sub_k1, sub_k2, peer_u, peer_v):
    half = ROT_DIM // 2
    inv = ROPE_THETA ** (-jnp.arange(half, dtype=F32) * 2.0 / ROT_DIM)
    w_g = w_in[0][:, N_MAIN:]
    return {
        "g_mix": g_mix[0][None, :],
        "w_main": w_in[0][:, :N_MAIN].astype(BF16),
        "w_g": w_g.astype(BF16),
        "w_gt": w_g.T.astype(BF16),
        "b_g": b_gates[0][None, :],
        "b_gt": b_gates[0][:, None],
        "g_qk": jnp.concatenate([jnp.tile(g_qn[0], ATT_WIDTH // ATT_QK_DIM),
                                 jnp.tile(g_kn[0], ATT_WIDTH // ATT_QK_DIM)])[None, :],
        "inv_lane": jnp.tile(inv, LANES // half)[None, :],
        "conv_w": conv_w[0][:, 0, :],
        "conv_b": conv_b[0][None, :],
        "lam4": jnp.stack([lam_q1[0], lam_k1[0], lam_q2[0], lam_k2[0]]),
        "g_sub": g_sub[0][None, :],
        "g_mh": g_mh[0][None, :],
        "w_out": w_out[0].astype(BF16),
        "g_ffn": g_ffn[0][None, :],
        "w_pqt": w_pq[0].T.astype(BF16),
        "sub_k1": sub_k1[0].astype(BF16),
        "sub_k2": sub_k2[0].astype(BF16),
        "peer_u": peer_u[0].astype(BF16),
        "peer_vt": peer_v[0].astype(BF16).reshape(-1, PEER_GROUP, D_MODEL).transpose(0, 2, 1),
    }


def _trunk(x, meta, w):
    b, s, d = x.shape
    tt = PROJ_TILE
    assert s % tt == 0 and s % (2 * ATT_TK) == 0 and s % MIX_TILE == 0 and s % PEER_TILE == 0
    nt = s // tt
    xr = x.reshape(b, nt, tt, d)
    meta_tail = jnp.broadcast_to(meta[N_META - HALO:][None, None], (b, 1, HALO, d))
    prev = jnp.concatenate([meta_tail, xr[:, :-1, tt - HALO:, :]], axis=1)
    nxt = jnp.concatenate([xr[:, 1:, :HALO, :], jnp.zeros((b, 1, HALO, d), x.dtype)], axis=1)
    q, k, v, qm, km, vm, om, g_tm, g_fm = _proj(x, prev, nxt, N_META, tt, w)

    meta_b = jnp.broadcast_to(meta[None], (b, N_META, d))
    m_prev = jnp.zeros((b, 1, HALO, d), x.dtype)
    m_next = x[:, None, :HALO, :]
    _, k_m, v_m, qm_m, km_m, vm_m, _, gtm_m, gfm_m = _proj(meta_b, m_prev, m_next, 0, N_META, w)

    pad = CHUNK - N_META
    att = _attn(q, k, v, jnp.pad(k_m, ((0, 0), (0, pad), (0, 0))), jnp.pad(v_m, ((0, 0), (0, pad), (0, 0))),
                w["lam4"], w["g_sub"])

    front = lambda a: jnp.pad(a, ((0, 0), (pad, 0), (0, 0)))
    is_input_gate = (jnp.arange(N_GATES) // ML_HEADS) % 2 == 0
    pad_gate = jnp.where(is_input_gate, NEG_INF, 0.0).astype(F32)
    gtm_p = jnp.concatenate([jnp.broadcast_to(pad_gate[None, None, :], (b, pad, N_GATES)), gtm_m], axis=1)
    gfm_p = jnp.concatenate([jnp.broadcast_to(pad_gate[None, :, None], (b, N_GATES, pad)), gfm_m], axis=2)
    s_zero = jnp.zeros((b, ML_HEADS, ML_DIM, 2 * LANES), F32)
    m_zero = jnp.zeros((b, SUBLANES, LANES), F32)
    _, _, s0, m0 = _mlstm(front(qm_m), front(km_m), front(vm_m), gtm_p, gfm_p, s_zero, m_zero)
    hf, hb, _, _ = _mlstm(qm, km, vm, g_tm, g_fm, s0, m0)

    h1, xn2, r2, e2, n1, e1 = _mix(x, att, hf, hb, om, w)
    return _peer(xn2, h1, r2, e2, n1, e1, w)


def kernel(x_prompt, x_sample, meta, g_mix, w_in, b_gates, conv_w, conv_b, g_qn, g_kn, lam_q1, lam_k1,
           lam_q2, lam_k2, g_sub, g_mh, w_out, g_ffn, w_pq, sub_k1, sub_k2, peer_u, peer_v):
    w = _prep_weights(g_mix, w_in, b_gates, conv_w, conv_b, g_qn, g_kn, lam_q1, lam_k1, lam_q2, lam_k2,
                      g_sub, g_mh, w_out, g_ffn, w_pq, sub_k1, sub_k2, peer_u, peer_v)
    return (_trunk(x_prompt, meta, w), _trunk(x_sample, meta, w))
```

```python
import functools
import math

import jax
import jax.numpy as jnp
from jax import lax
from jax.experimental import pallas as pl
from jax.experimental.pallas import tpu as pltpu

F32 = jnp.float32
BF16 = jnp.bfloat16

D_MODEL = 1024
N_META = 16
ATT_HEADS = 4
ATT_QK_DIM = 64
ATT_V_DIM = 128
ATT_WIDTH = ATT_HEADS * ATT_V_DIM
ML_HEADS = 4
ML_DIM = 128
ML_WIDTH = ML_HEADS * ML_DIM
N_GATES = 4 * ML_HEADS
N_MAIN = 3 * ATT_WIDTH + 4 * ML_WIDTH
ROT_DIM = ATT_QK_DIM // 4
ROPE_THETA = 500000.0
CHUNK = 128
PEER_HEADS = 8
PEER_NKEYS = 128
PEER_QDIM = 256
PEER_TOPK = 16
EPS = 1e-6
LAM_INIT = 0.8 - 0.6 * math.exp(-0.3 * 0)
Q_SCALE = ATT_QK_DIM ** -0.5 * math.log2(math.e)

LANES = 128
SUBLANES = 8
HALO = SUBLANES
VMEM_LIMIT = 56 * 1024 * 1024

PROJ_TILE = 512
ATT_TQ = 512
ATT_TK = 1024
MIX_TILE = 512
ROUTE_SLAB = 128
PEER_TILE = 512
PEER_GROUP = 2048
PEER_PASS_LANES = 256
PEER_VBLOCK = 256
NEG_INF = float("-inf")


def _cparams(sem):
    return pltpu.CompilerParams(dimension_semantics=sem, vmem_limit_bytes=VMEM_LIMIT)


def _const_spec(shape):
    nd = len(shape)
    return pl.BlockSpec(shape, lambda *_: (0,) * nd)


def _log_sigmoid(x):
    return -(jnp.maximum(-x, 0.0) + jnp.log1p(jnp.exp(-jnp.abs(x))))


def _proj_kernel(pos_base, tt,
                 x_ref, prev_ref, next_ref, gmix_ref, wmain_ref, wg_ref, wgt_ref, bg_ref, bgt_ref,
                 gqk_ref, inv_ref, convw_ref, convb_ref,
                 q_ref, k_ref, v_ref, qm_ref, km_ref, vm_ref, om_ref, gtm_ref, gfm_ref,
                 xe_ref, zc_ref):
    i = pl.program_id(1)
    xe_ref[0:HALO, :] = prev_ref[0, 0]
    xe_ref[HALO:HALO + tt, :] = x_ref[0]
    xe_ref[HALO + tt:2 * HALO + tt, :] = next_ref[0, 0]
    xe = xe_ref[...]
    ms = jnp.mean(xe * xe, axis=-1, keepdims=True)
    xn_ext = (xe * lax.rsqrt(ms + EPS) * gmix_ref[...]).astype(BF16)
    xn = xn_ext[HALO:HALO + tt]

    def mm(a, lo, hi):
        return jnp.dot(a, wmain_ref[:, lo:hi], preferred_element_type=F32)

    zqk = mm(xn, 0, 2 * ATT_WIDTH)
    sq = (zqk * zqk).astype(BF16)
    r = lax.broadcasted_iota(jnp.int32, (2 * LANES, 2 * LANES), 0) // ATT_QK_DIM
    c = lax.broadcasted_iota(jnp.int32, (2 * LANES, 2 * LANES), 1) // ATT_QK_DIM
    seg_ones = (r == c).astype(BF16)
    lane = lax.broadcasted_iota(jnp.int32, (tt, LANES), 1)
    l64 = lane % ATT_QK_DIM
    row = lax.broadcasted_iota(jnp.int32, (tt, LANES), 0)
    pos = (pos_base + i * tt + row).astype(F32)
    ang = pos * inv_ref[...]
    cos_a = jnp.cos(ang)
    sin_a = jnp.sin(ang)
    half = ROT_DIM // 2
    cos_m = jnp.where(l64 < ROT_DIM, cos_a, 1.0)
    sin_lo = jnp.where(l64 < half, -sin_a, 0.0)
    sin_hi = jnp.where((l64 >= half) & (l64 < ROT_DIM), sin_a, 0.0)
    for cb in range(4):
        lo = cb * 2 * LANES
        ss = jnp.dot(sq[:, lo:lo + 2 * LANES], seg_ones, preferred_element_type=F32)
        rn = lax.rsqrt(ss * (1.0 / ATT_QK_DIM) + EPS)
        qn = zqk[:, lo:lo + 2 * LANES] * rn * gqk_ref[:, lo:lo + 2 * LANES]
        for sb in range(2):
            xs = qn[:, sb * LANES:(sb + 1) * LANES]
            rot = (xs * cos_m + pltpu.roll(xs, LANES - half, 1) * sin_lo
                   + pltpu.roll(xs, half, 1) * sin_hi)
            col = lo + sb * LANES
            if col < ATT_WIDTH:
                q_ref[0, :, col:col + LANES] = (rot * Q_SCALE).astype(BF16)
            else:
                k_ref[0, :, col - ATT_WIDTH:col - ATT_WIDTH + LANES] = rot.astype(BF16)

    va = mm(xn, 2 * ATT_WIDTH, 3 * ATT_WIDTH).astype(BF16)
    for h in range(ATT_HEADS):
        v_ref[0, :, 2 * h * LANES:(2 * h + 1) * LANES] = va[:, h * ATT_V_DIM:(h + 1) * ATT_V_DIM]
        v_ref[0, :, (2 * h + 1) * LANES:(2 * h + 2) * LANES] = jnp.ones((tt, LANES), BF16)

    base = 3 * ATT_WIDTH
    zc_ref[...] = mm(xn_ext, base, base + 2 * ML_WIDTH)
    y = (convw_ref[0:1, :] * zc_ref[pl.ds(HALO - 1, tt), :]
         + convw_ref[1:2, :] * zc_ref[pl.ds(HALO, tt), :]
         + convw_ref[2:3, :] * zc_ref[pl.ds(HALO + 1, tt), :]
         + convb_ref[...])
    act = y * (1.0 / (1.0 + jnp.exp(-y)))
    qm_ref[0] = act[:, :ML_WIDTH].astype(BF16)
    km_ref[0] = (act[:, ML_WIDTH:] * (ML_DIM ** -0.5)).astype(BF16)
    vm_ref[0] = mm(xn, base + 2 * ML_WIDTH, base + 3 * ML_WIDTH).astype(BF16)
    om_ref[0] = mm(xn, base + 3 * ML_WIDTH, base + 4 * ML_WIDTH).astype(BF16)

    g_tm = jnp.dot(xn, wg_ref[...], preferred_element_type=F32) + bg_ref[...]
    g_fm = lax.dot_general(wgt_ref[...], xn, (((1,), (1,)), ((), ())),
                           preferred_element_type=F32) + bgt_ref[...]
    col_tm = lax.broadcasted_iota(jnp.int32, (tt, N_GATES), 1)
    row_fm = lax.broadcasted_iota(jnp.int32, (N_GATES, tt), 0)
    gtm_ref[0] = jnp.where((col_tm // ML_HEADS) % 2 == 1, _log_sigmoid(g_tm), g_tm)
    gfm_ref[0] = jnp.where((row_fm // ML_HEADS) % 2 == 1, _log_sigmoid(g_fm), g_fm)


def _proj(x, prev, nxt, pos_base, tt, w):
    b, s, d = x.shape
    nt = s // tt
    tok = lambda width, dt: jax.ShapeDtypeStruct((b, s, width), dt)
    out_shape = ([tok(ATT_WIDTH, BF16)] * 2 + [tok(2 * ATT_WIDTH, BF16)] + [tok(ML_WIDTH, BF16)] * 4
                 + [tok(N_GATES, F32), jax.ShapeDtypeStruct((b, N_GATES, s), F32)])
    tspec = lambda width: pl.BlockSpec((1, tt, width), lambda bi, ti: (bi, ti, 0))
    hspec = pl.BlockSpec((1, 1, HALO, d), lambda bi, ti: (bi, ti, 0, 0))
    return pl.pallas_call(
        functools.partial(_proj_kernel, pos_base, tt),
        grid=(b, nt),
        in_specs=[tspec(d), hspec, hspec, _const_spec((1, d)), _const_spec((d, N_MAIN)),
                  _const_spec((d, N_GATES)), _const_spec((N_GATES, d)), _const_spec((1, N_GATES)),
                  _const_spec((N_GATES, 1)), _const_spec((1, 2 * ATT_WIDTH)), _const_spec((1, LANES)),
                  _const_spec((3, 2 * ML_WIDTH)), _const_spec((1, 2 * ML_WIDTH))],
        out_specs=[tspec(ATT_WIDTH)] * 2 + [tspec(2 * ATT_WIDTH)] + [tspec(ML_WIDTH)] * 4
        + [tspec(N_GATES), pl.BlockSpec((1, N_GATES, tt), lambda bi, ti: (bi, 0, ti))],
        out_shape=out_shape,
        scratch_shapes=[pltpu.VMEM((tt + 2 * HALO, d), F32), pltpu.VMEM((tt + 2 * HALO, 2 * ML_WIDTH), F32)],
        compiler_params=_cparams(("parallel", "parallel")),
        name=f"proj_t{tt}",
    )(x, prev, nxt, w["g_mix"], w["w_main"], w["w_g"], w["w_gt"], w["b_g"], w["b_gt"],
      w["g_qk"], w["inv_lane"], w["conv_w"], w["conv_b"])


def _attn_kernel(tq, tk, nk,
                 q_ref, k_ref, v_ref, kt_ref, vt_ref, lam_ref, gsub_ref, o_ref,
                 qq_ref, m_ref, acc_ref, sa_ref, sb_ref):
    q = q_ref[0]
    lane = lax.broadcasted_iota(jnp.int32, (tq, LANES), 1)
    zero = jnp.zeros_like(q)
    qq_ref[0:tq, :] = jnp.where(lane < ATT_QK_DIM, q, zero)
    qq_ref[tq:2 * tq, :] = jnp.where(lane >= ATT_QK_DIM, q, zero)
    m_ref[...] = jnp.full(m_ref.shape, NEG_INF, F32)
    acc_ref[...] = jnp.zeros(acc_ref.shape, F32)

    def scores(kb):
        return lax.dot_general(qq_ref[...], kb, (((1,), (1,)), ((), ())), preferred_element_type=F32)

    def update(s, vb):
        m_prev = m_ref[...]
        m_new = jnp.maximum(m_prev, jnp.max(s, axis=1, keepdims=True))
        alpha = jnp.exp2(m_prev - m_new)
        p = jnp.exp2(s - m_new[:, 0:1]).astype(BF16)
        pv = jnp.dot(p, vb, preferred_element_type=F32)
        acc_ref[:, 0:LANES] = alpha * acc_ref[:, 0:LANES] + pv[:, 0:LANES]
        acc_ref[:, LANES:2 * LANES] = alpha * acc_ref[:, LANES:2 * LANES] + pv[:, LANES:2 * LANES]
        m_ref[...] = m_new

    def kblock(j):
        return k_ref[0, pl.ds(pl.multiple_of(j * tk, tk), tk), :]

    def vblock(j):
        return v_ref[0, pl.ds(pl.multiple_of(j * tk, tk), tk), :]

    sa_ref[...] = scores(kblock(0))

    def body(i, carry):
        j = 2 * i
        sb_ref[...] = scores(kblock(j + 1))
        update(sa_ref[...], vblock(j))
        sa_ref[...] = scores(kblock(j + 2))
        update(sb_ref[...], vblock(j + 1))
        return carry

    lax.fori_loop(0, nk // 2 - 1, body, 0)
    sb_ref[...] = scores(kblock(nk - 1))
    update(sa_ref[...], vblock(nk - 2))
    col = lax.broadcasted_iota(jnp.int32, (2 * tq, LANES), 1)
    s_meta = jnp.where(col < N_META, scores(kt_ref[0]), NEG_INF)
    update(sb_ref[...], vblock(nk - 1))
    update(s_meta, vt_ref[0])

    o1 = acc_ref[0:tq, 0:LANES] / acc_ref[0:tq, LANES:2 * LANES]
    o2 = acc_ref[tq:2 * tq, 0:LANES] / acc_ref[tq:2 * tq, LANES:2 * LANES]
    lam = (jnp.exp(jnp.sum(lam_ref[0:1, :] * lam_ref[1:2, :], axis=1, keepdims=True))
           - jnp.exp(jnp.sum(lam_ref[2:3, :] * lam_ref[3:4, :], axis=1, keepdims=True)) + LAM_INIT)
    o = o1 - lam * o2
    ms = jnp.mean(o * o, axis=-1, keepdims=True)
    o_ref[0] = (o * lax.rsqrt(ms + EPS) * gsub_ref[...] * (1.0 - LAM_INIT)).astype(BF16)


def _att_key_block(s):
    resident = s * 3 * LANES * 2 * 2
    for tk in (2 * ATT_TK, ATT_TK):
        scores = 2 * (2 * ATT_TQ) * tk * 4
        if s % (2 * tk) == 0 and resident + scores <= VMEM_LIMIT * 3 // 5:
            return tk
    raise ValueError(f"sequence length {s} is not a multiple of {2 * ATT_TK}")


def _attn(q, k, v, kt, vt, lam4, g_sub):
    b, s, _ = q.shape
    tq, tk = ATT_TQ, _att_key_block(s)
    kspec = lambda rows: pl.BlockSpec((1, rows, LANES), lambda bi, hi, qi: (bi, 0, hi))
    vspec = lambda rows: pl.BlockSpec((1, rows, 2 * LANES), lambda bi, hi, qi: (bi, 0, hi))
    return pl.pallas_call(
        functools.partial(_attn_kernel, tq, tk, s // tk),
        grid=(b, ATT_HEADS, s // tq),
        in_specs=[pl.BlockSpec((1, tq, LANES), lambda bi, hi, qi: (bi, qi, hi)),
                  kspec(s), vspec(s), kspec(LANES), vspec(LANES),
                  _const_spec((4, ATT_QK_DIM)), _const_spec((1, ATT_V_DIM))],
        out_specs=pl.BlockSpec((1, tq, LANES), lambda bi, hi, qi: (bi, qi, hi)),
        out_shape=jax.ShapeDtypeStruct((b, s, ATT_WIDTH), BF16),
        scratch_shapes=[pltpu.VMEM((2 * tq, LANES), BF16), pltpu.VMEM((2 * tq, LANES), F32),
                        pltpu.VMEM((2 * tq, 2 * LANES), F32),
                        pltpu.VMEM((2 * tq, tk), F32), pltpu.VMEM((2 * tq, tk), F32)],
        compiler_params=_cparams(("parallel", "parallel", "parallel")),
        name="diff_attn",
    )(q, k, v, kt, vt, lam4, g_sub)


def _mlstm_chunks(dirs):
    r = lax.broadcasted_iota(jnp.int32, (CHUNK, CHUNK), 0)
    c = lax.broadcasted_iota(jnp.int32, (CHUNK, CHUNK), 1)
    ones_col = (c == 0).astype(BF16)
    chains = []
    for reverse, q, k, v, g_tm, g_fm, s_ref, m_ref, h_ref in dirs:
        causal = (c >= r) if reverse else (c <= r)
        last = 0 if reverse else CHUNK - 1
        gbase = 2 * ML_HEADS if reverse else 0
        b_cols = jnp.dot(causal.astype(F32), g_tm, preferred_element_type=F32,
                         precision=lax.Precision.HIGHEST)
        b_rows = jnp.dot(g_fm, ((r >= c) if reverse else (r <= c)).astype(F32),
                         preferred_element_type=F32, precision=lax.Precision.HIGHEST)
        for h in range(ML_HEADS):
            fc = gbase + ML_HEADS + h
            sl = slice(h * ML_DIM, (h + 1) * ML_DIM)
            chains.append(dict(
                causal=causal, last=last, h=h, sl=sl, s_ref=s_ref, m_ref=m_ref, h_ref=h_ref,
                q=q[:, sl], k=k[:, sl], v=v[:, sl],
                b_col=b_cols[:, fc:fc + 1], b_row=b_rows[fc:fc + 1, :],
                li_row=g_fm[gbase + h:gbase + h + 1, :], li_col=g_tm[:, gbase + h:gbase + h + 1]))
    n = range(len(chains))
    ch = chains
    m_prev = [ch[i]["m_ref"][ch[i]["h"]:ch[i]["h"] + 1, 0:1] for i in n]
    log_d = [jnp.where(ch[i]["causal"], ch[i]["b_col"] - ch[i]["b_row"] + ch[i]["li_row"], NEG_INF) for i in n]
    m_inter = [ch[i]["b_col"] + m_prev[i] for i in n]
    mj = [jnp.maximum(m_inter[i], jnp.max(log_d[i], axis=1, keepdims=True)) for i in n]
    vext = [jnp.concatenate([ch[i]["v"], ones_col], axis=1) for i in n]
    qk = [lax.dot_general(ch[i]["q"], ch[i]["k"], (((1,), (1,)), ((), ())), preferred_element_type=F32)
          for i in n]
    state = [ch[i]["s_ref"][ch[i]["h"]] for i in n]
    inter = [jnp.dot(ch[i]["q"], state[i].astype(BF16), preferred_element_type=F32) for i in n]
    b_last = [ch[i]["b_col"][ch[i]["last"]:ch[i]["last"] + 1, :] for i in n]
    m_new = [jnp.maximum(b_last[i] + m_prev[i],
                         jnp.max(b_last[i] - ch[i]["b_row"] + ch[i]["li_row"], axis=1, keepdims=True)) for i in n]
    w = [(qk[i] * jnp.exp(log_d[i] - mj[i])).astype(BF16) for i in n]
    kw = [(ch[i]["k"].astype(F32)
           * jnp.exp(b_last[i] - ch[i]["b_col"] + ch[i]["li_col"] - m_new[i])).astype(BF16) for i in n]
    tot = [jnp.exp(m_inter[i] - mj[i]) * inter[i] + jnp.dot(w[i], vext[i], preferred_element_type=F32)
           for i in n]
    upd = [lax.dot_general(kw[i], vext[i], (((0,), (0,)), ((), ())), preferred_element_type=F32) for i in n]
    for i in n:
        den = tot[i][:, ML_DIM:ML_DIM + 1]
        ch[i]["h_ref"][0, :, ch[i]["sl"]] = tot[i][:, :ML_DIM] / jnp.maximum(jnp.abs(den), jnp.exp(-mj[i]))
    for i in n:
        h = ch[i]["h"]
        ch[i]["s_ref"][h] = jnp.exp(b_last[i] + m_prev[i] - m_new[i]) * state[i] + upd[i]
        ch[i]["m_ref"][h:h + 1, :] = jnp.broadcast_to(m_new[i], (1, LANES))


def _mlstm_kernel(qf_ref, kf_ref, vf_ref, gtf_ref, gff_ref, qb_ref, kb_ref, vb_ref, gtb_ref, gfb_ref,
                  s0_ref, m0_ref, hf_ref, hb_ref, sfin_ref, mfin_ref,
                  sf_ref, sb_ref, mf_ref, mb_ref):
    j = pl.program_id(1)

    @pl.when(j == 0)
    def _():
        sf_ref[...] = s0_ref[0]
        mf_ref[...] = m0_ref[0]
        sb_ref[...] = jnp.zeros(sb_ref.shape, F32)
        mb_ref[...] = jnp.zeros(mb_ref.shape, F32)

    _mlstm_chunks([(False, qf_ref[0], kf_ref[0], vf_ref[0], gtf_ref[0], gff_ref[0], sf_ref, mf_ref, hf_ref)])
    _mlstm_chunks([(True, qb_ref[0], kb_ref[0], vb_ref[0], gtb_ref[0], gfb_ref[0], sb_ref, mb_ref, hb_ref)])

    @pl.when(j == pl.num_programs(1) - 1)
    def _():
        sfin_ref[0] = sf_ref[...]
        mfin_ref[0] = mf_ref[...]


def _mlstm(qm, km, vm, g_tm, g_fm, s0, m0):
    b, s, _ = qm.shape
    nc = s // CHUNK
    fwd = lambda width: pl.BlockSpec((1, CHUNK, width), lambda bi, ci: (bi, ci, 0))
    bwd = lambda width: pl.BlockSpec((1, CHUNK, width), lambda bi, ci: (bi, nc - 1 - ci, 0))
    gf_fwd = pl.BlockSpec((1, N_GATES, CHUNK), lambda bi, ci: (bi, 0, ci))
    gf_bwd = pl.BlockSpec((1, N_GATES, CHUNK), lambda bi, ci: (bi, 0, nc - 1 - ci))
    st = pl.BlockSpec((1, ML_HEADS, ML_DIM, 2 * LANES), lambda bi, ci: (bi, 0, 0, 0))
    mx = pl.BlockSpec((1, SUBLANES, LANES), lambda bi, ci: (bi, 0, 0))
    return pl.pallas_call(
        _mlstm_kernel,
        grid=(b, nc),
        in_specs=[fwd(ML_WIDTH)] * 3 + [fwd(N_GATES), gf_fwd] + [bwd(ML_WIDTH)] * 3 + [bwd(N_GATES), gf_bwd]
        + [st, mx],
        out_specs=[fwd(ML_WIDTH), bwd(ML_WIDTH), st, mx],
        out_shape=[jax.ShapeDtypeStruct((b, s, ML_WIDTH), F32)] * 2
        + [jax.ShapeDtypeStruct((b, ML_HEADS, ML_DIM, 2 * LANES), F32),
           jax.ShapeDtypeStruct((b, SUBLANES, LANES), F32)],
        scratch_shapes=[pltpu.VMEM((ML_HEADS, ML_DIM, 2 * LANES), F32)] * 2
        + [pltpu.VMEM((SUBLANES, LANES), F32)] * 2,
        compiler_params=_cparams(("parallel", "arbitrary")),
        name="mlstm",
    )(qm, km, vm, g_tm, g_fm, qm, km, vm, g_tm, g_fm, s0, m0)


def _sorting_network(n):
    pairs = []
    p = 1
    while p < n:
        k = p
        while k >= 1:
            for j in range(k % p, n - k, 2 * k):
                for i in range(min(k, n - j - k)):
                    if (i + j) // (2 * p) == (i + j + k) // (2 * p):
                        pairs.append((i + j, i + j + k))
            k //= 2
        p *= 2
    return pairs


def _top_values_sorted(xs, n):
    sorted_groups = []
    for x in xs:
        groups = [x[SUBLANES * g:SUBLANES * (g + 1)] for g in range(x.shape[0] // SUBLANES)]
        size = 1
        while size < len(groups):
            size *= 2
        groups += [jnp.full_like(groups[0], NEG_INF)] * (size - len(groups))
        for i, j in _sorting_network(size):
            groups[i], groups[j] = jnp.maximum(groups[i], groups[j]), jnp.minimum(groups[i], groups[j])
        sorted_groups.append(groups)
    vals = [[] for _ in xs]
    for r in range(n):
        for groups, out in zip(sorted_groups, vals):
            mx = jnp.max(groups[0], axis=0, keepdims=True)
            out.append(mx)
            if r + 1 < n:
                hit = groups[0] == mx
                live = min(len(groups), n - 1 - r)
                for d in range(live):
                    below = groups[d + 1] if d + 1 < len(groups) else NEG_INF
                    groups[d] = jnp.where(hit, below, groups[d])
    return vals


def _count_greater(vals, x):
    n = len(vals)

    def pivot(decided, lo, size):
        if not decided:
            return vals[lo + size // 2 - 1]
        half = size // 2
        return jnp.where(decided[0], pivot(decided[1:], lo + half, half), pivot(decided[1:], lo, half))

    decided = []
    count = jnp.zeros_like(x)
    size = n
    while size >= 2:
        above = pivot(decided, 0, n) > x
        decided.append(above)
        count = count + jnp.where(above, float(size // 2), 0.0)
        size //= 2
    return jnp.where(vals[n - 1] > x, float(n), count)


def _bf16_pair_bits(x):
    u = lax.bitcast_convert_type(x.astype(BF16).astype(F32), jnp.uint32)
    return u | (u >> 16)


_CAND_PAIRS = [(a, b) for a in range(PEER_TOPK + 1) for b in range(PEER_TOPK + 1)
               if (a + 1) * (b + 1) <= PEER_TOPK + 1]
_CAND_ROWS = -(-len(_CAND_PAIRS) // SUBLANES) * SUBLANES


def _mix_kernel(tt, x_ref, att_ref, hf_ref, hb_ref, om_ref, gmh_ref, wout_ref, gffn_ref, wpqt_ref,
                k1_ref, k2_ref, h1_ref, xn2_ref, r2_ref, e2_ref, n1_ref, e1_ref, mix_ref, cand_ref):
    mix_ref[:, 0:ATT_WIDTH] = att_ref[0]
    for h in range(ML_HEADS):
        sl = slice(h * ML_DIM, (h + 1) * ML_DIM)
        hs = hf_ref[0, :, sl] + hb_ref[0, :, sl]
        ms = jnp.mean(hs * hs, axis=-1, keepdims=True)
        hn = hs * lax.rsqrt(ms + EPS) * gmh_ref[...]
        gate = 1.0 / (1.0 + jnp.exp(-om_ref[0, :, sl].astype(F32)))
        mix_ref[:, ATT_WIDTH + h * ML_DIM:ATT_WIDTH + (h + 1) * ML_DIM] = (hn * gate).astype(BF16)
    h1 = x_ref[0] + jnp.dot(mix_ref[...], wout_ref[...], preferred_element_type=F32)
    h1_ref[0] = h1
    ms = jnp.mean(h1 * h1, axis=-1, keepdims=True)
    xn2 = (h1 * lax.rsqrt(ms + EPS) * gffn_ref[...]).astype(BF16)
    xn2_ref[0] = xn2
    qt = lax.dot_general(wpqt_ref[...], xn2, (((1,), (1,)), ((), ())), preferred_element_type=F32)
    half = PEER_QDIM // 2
    cand_ref[...] = jnp.full(cand_ref.shape, NEG_INF, F32)
    for h in range(PEER_HEADS):
        q1 = qt[h * PEER_QDIM:h * PEER_QDIM + half].astype(BF16)
        q2 = qt[h * PEER_QDIM + half:(h + 1) * PEER_QDIM].astype(BF16)
        s1 = jnp.dot(k1_ref[h], q1, preferred_element_type=F32)
        s2 = jnp.dot(k2_ref[h], q2, preferred_element_type=F32)
        for sl in range(tt // ROUTE_SLAB):
            lanes = slice(sl * ROUTE_SLAB, (sl + 1) * ROUTE_SLAB)
            a1 = s1[:, lanes]
            a2 = s2[:, lanes]
            v1, v2 = _top_values_sorted([a1, a2], PEER_TOPK + 1)
            rank2 = _count_greater(v2[:PEER_TOPK], a2)
            for idx, (a, b) in enumerate(_CAND_PAIRS):
                cand_ref[idx:idx + 1, lanes] = v1[a] + v2[b]
            cs, = _top_values_sorted([cand_ref[:, lanes]], PEER_TOPK + 1)
            z = jnp.zeros_like(cs[0])
            for i in range(PEER_TOPK):
                z = z + jnp.exp(cs[i] - cs[0])
            thr = 0.5 * (cs[PEER_TOPK - 1] + cs[PEER_TOPK])
            n1 = _count_greater(v2[:PEER_TOPK // 2], thr - a1)
            n_best = jnp.zeros_like(thr)
            for b in range(PEER_TOPK // 2, PEER_TOPK):
                n_best = n_best + jnp.where(v2[b] > thr - v1[0], 1.0, 0.0)
            n1 = n1 + jnp.where(a1 == v1[0], n_best, 0.0)
            r2_ref[0, h, :, lanes] = rank2.astype(BF16)
            n1_ref[0, h, :, lanes] = _bf16_pair_bits(n1)
            e1_ref[0, h, :, lanes] = _bf16_pair_bits(jnp.exp(a1 - v1[0]) * (0.5 / z))
            e2_ref[0, h, :, lanes] = jnp.exp(a2 - v2[0]).astype(BF16)


def _mix(x, att, hf, hb, om, w):
    b, s, d = x.shape
    tt = MIX_TILE
    nt = s // tt
    tspec = lambda width: pl.BlockSpec((1, tt, width), lambda bi, ti: (bi, ti, 0))
    sspec = pl.BlockSpec((1, PEER_HEADS, PEER_NKEYS, tt), lambda bi, ti: (bi, 0, 0, ti))
    return pl.pallas_call(
        functools.partial(_mix_kernel, tt),
        grid=(b, nt),
        in_specs=[tspec(d), tspec(ATT_WIDTH), tspec(ML_WIDTH), tspec(ML_WIDTH), tspec(ML_WIDTH),
                  _const_spec((1, ML_DIM)), _const_spec((d, d)), _const_spec((1, d)),
                  _const_spec((PEER_HEADS * PEER_QDIM, d)),
                  _const_spec((PEER_HEADS, PEER_NKEYS, PEER_QDIM // 2)),
                  _const_spec((PEER_HEADS, PEER_NKEYS, PEER_QDIM // 2))],
        out_specs=[tspec(d), tspec(d), sspec, sspec, sspec, sspec],
        out_shape=[jax.ShapeDtypeStruct((b, s, d), F32), jax.ShapeDtypeStruct((b, s, d), BF16),
                   jax.ShapeDtypeStruct((b, PEER_HEADS, PEER_NKEYS, s), BF16),
                   jax.ShapeDtypeStruct((b, PEER_HEADS, PEER_NKEYS, s), BF16),
                   jax.ShapeDtypeStruct((b, PEER_HEADS, PEER_NKEYS, s), jnp.uint32),
                   jax.ShapeDtypeStruct((b, PEER_HEADS, PEER_NKEYS, s), jnp.uint32)],
        scratch_shapes=[pltpu.VMEM((tt, d), BF16), pltpu.VMEM((_CAND_ROWS, tt), F32)],
        compiler_params=_cparams(("parallel", "parallel")),
        name="mix_route",
    )(x, att, hf, hb, om, w["g_mh"], w["w_out"], w["g_ffn"], w["w_pqt"], w["sub_k1"], w["sub_k2"])


def _gelu_x2(x):
    return x + x * lax.erf(x * (2.0 ** -0.5))


def _peer_kernel(tt, ng, xn2_ref, h1_ref, r2_ref, e2_ref, n1_ref, e1_ref, u_ref, vt_ref, y_ref,
                 a0_ref, a1_ref, w_ref, acc_ref):
    s = pl.program_id(1)
    rows = PEER_GROUP // PEER_NKEYS

    @pl.when(s == 0)
    def _():
        a0_ref[...] = jnp.zeros(a0_ref.shape, F32)
        a1_ref[...] = jnp.zeros(a1_ref.shape, F32)

    @pl.when((s < 1) | ((s + ng - 1) % ng == 0))
    def _():
        acc_ref[...] = jnp.zeros(acc_ref.shape, F32)

    g2 = (s + ng - 1) % ng

    def row(ref, h, c, lanes):
        words = jnp.broadcast_to(ref[0, h, pl.ds(c, 1), lanes], (SUBLANES, PEER_PASS_LANES))
        tile = pltpu.bitcast(words, BF16)
        return jnp.concatenate([tile] * (PEER_NKEYS // (2 * SUBLANES)), axis=0)

    def stages(a_cur, a_oth):
        zero = jnp.zeros((PEER_NKEYS, PEER_PASS_LANES), BF16)
        tok_half = tt // 2
        for kb in range(PEER_GROUP // PEER_VBLOCK):
            for cc in range(kb * PEER_VBLOCK // PEER_NKEYS, (kb + 1) * PEER_VBLOCK // PEER_NKEYS):
                c = g2 * rows + cc
                ex = slice(cc * PEER_NKEYS, (cc + 1) * PEER_NKEYS)
                for lb in range(tt // PEER_PASS_LANES):
                    lanes = pl.ds(lb * PEER_PASS_LANES, PEER_PASS_LANES)
                    wsum = zero
                    for h in range(PEER_HEADS):
                        sel = r2_ref[0, h, :, lanes] < row(n1_ref, h, c, lanes)
                        wsum = wsum + jnp.where(sel, e2_ref[0, h, :, lanes], zero) * row(e1_ref, h, c, lanes)
                    w_ref[ex, lanes] = wsum * _gelu_x2(a_oth[ex, lanes]).astype(BF16)
            if kb < 2:
                tk = slice(kb * tok_half, (kb + 1) * tok_half)
                a_cur[:, tk] = lax.dot_general(u_ref[...], xn2_ref[0, tk, :], (((1,), (1,)), ((), ())),
                                               preferred_element_type=F32)
            blk = slice(kb * PEER_VBLOCK, (kb + 1) * PEER_VBLOCK)
            acc_ref[...] += jnp.dot(vt_ref[0, :, blk], w_ref[blk, :], preferred_element_type=F32)

    @pl.when(s % 2 == 0)
    def _():
        stages(a0_ref, a1_ref)

    @pl.when(s % 2 == 1)
    def _():
        stages(a1_ref, a0_ref)

    @pl.when((s >= 1) & ((s + ng - 1) % ng == ng - 1))
    def _():
        y_ref[0] = h1_ref[0] + acc_ref[...].T


def _peer(xn2, h1, r2, e2, n1, e1, w):
    b, s, d = h1.shape
    tt = PEER_TILE
    nt = s // tt
    ng = (PEER_NKEYS * PEER_NKEYS) // PEER_GROUP
    tile = lambda si, lag: jnp.clip((si - lag) // ng, 0, nt - 1)
    tspec = lambda lag: pl.BlockSpec((1, tt, d), lambda bi, si: (bi, tile(si, lag), 0))
    sspec = pl.BlockSpec((1, PEER_HEADS, PEER_NKEYS, tt), lambda bi, si: (bi, 0, 0, tile(si, 1)))
    return pl.pallas_call(
        functools.partial(_peer_kernel, tt, ng),
        grid=(b, nt * ng + 1),
        in_specs=[tspec(0), tspec(1), sspec, sspec, sspec, sspec,
                  pl.BlockSpec((PEER_GROUP, d), lambda bi, si: (si % ng, 0)),
                  pl.BlockSpec((1, d, PEER_GROUP), lambda bi, si: ((si + ng - 1) % ng, 0, 0))],
        out_specs=tspec(1),
        out_shape=jax.ShapeDtypeStruct((b, s, d), F32),
        scratch_shapes=[pltpu.VMEM((PEER_GROUP, tt), F32)] * 2 + [pltpu.VMEM((PEER_GROUP, tt), BF16)]
        + [pltpu.VMEM((d, tt), F32)],
        compiler_params=_cparams(("parallel", "arbitrary")),
        name="peer_dense",
    )(xn2, h1, r2, e2, n1, e1, w["peer_u"], w["peer_vt"])


def _prep_weights(g_mix, w_in, b_gates, conv_w, conv_b, g_qn, g_kn, lam_q1, lam_k1, lam_q2, lam_k2,
                  g_sub, g_mh, w_out, g_ffn, w_pq, sub_k1, sub_k2, peer_u, peer_v):
    half = ROT_DIM // 2
    inv = ROPE_THETA ** (-jnp.arange(half, dtype=F32) * 2.0 / ROT_DIM)
    w_g = w_in[0][:, N_MAIN:]
    return {
        "g_mix": g_mix[0][None, :],
        "w_main": w_in[0][:, :N_MAIN].astype(BF16),
        "w_g": w_g.astype(BF16),
        "w_gt": w_g.T.astype(BF16),
        "b_g": b_gates[0][None, :],
        "b_gt": b_gates[0][:, None],
        "g_qk": jnp.concatenate([jnp.tile(g_qn[0], ATT_WIDTH // ATT_QK_DIM),
                                 jnp.tile(g_kn[0], ATT_WIDTH // ATT_QK_DIM)])[None, :],
        "inv_lane": jnp.tile(inv, LANES // half)[None, :],
        "conv_w": conv_w[0][:, 0, :],
        "conv_b": conv_b[0][None, :],
        "lam4": jnp.stack([lam_q1[0], lam_k1[0], lam_q2[0], lam_k2[0]]),
        "g_sub": g_sub[0][None, :],
        "g_mh": g_mh[0][None, :],
        "w_out": w_out[0].astype(BF16),
        "g_ffn": g_ffn[0][None, :],
        "w_pqt": w_pq[0].T.astype(BF16),
        "sub_k1": sub_k1[0].astype(BF16),
        "sub_k2": sub_k2[0].astype(BF16),
        "peer_u": peer_u[0].astype(BF16),
        "peer_vt": peer_v[0].astype(BF16).reshape(-1, PEER_GROUP, D_MODEL).transpose(0, 2, 1),
    }


def _trunk(x, meta, w):
    b, s, d = x.shape
    tt = PROJ_TILE
    assert s % tt == 0 and s % (2 * ATT_TK) == 0 and s % MIX_TILE == 0 and s % PEER_TILE == 0
    nt = s // tt
    xr = x.reshape(b, nt, tt, d)
    meta_tail = jnp.broadcast_to(meta[N_META - HALO:][None, None], (b, 1, HALO, d))
    prev = jnp.concatenate([meta_tail, xr[:, :-1, tt - HALO:, :]], axis=1)
    nxt = jnp.concatenate([xr[:, 1:, :HALO, :], jnp.zeros((b, 1, HALO, d), x.dtype)], axis=1)
    q, k, v, qm, km, vm, om, g_tm, g_fm = _proj(x, prev, nxt, N_META, tt, w)

    meta_b = jnp.broadcast_to(meta[None], (b, N_META, d))
    m_prev = jnp.zeros((b, 1, HALO, d), x.dtype)
    m_next = x[:, None, :HALO, :]
    _, k_m, v_m, qm_m, km_m, vm_m, _, gtm_m, gfm_m = _proj(meta_b, m_prev, m_next, 0, N_META, w)

    pad = CHUNK - N_META
    att = _attn(q, k, v, jnp.pad(k_m, ((0, 0), (0, pad), (0, 0))), jnp.pad(v_m, ((0, 0), (0, pad), (0, 0))),
                w["lam4"], w["g_sub"])

    front = lambda a: jnp.pad(a, ((0, 0), (pad, 0), (0, 0)))
    is_input_gate = (jnp.arange(N_GATES) // ML_HEADS) % 2 == 0
    pad_gate = jnp.where(is_input_gate, NEG_INF, 0.0).astype(F32)
    gtm_p = jnp.concatenate([jnp.broadcast_to(pad_gate[None, None, :], (b, pad, N_GATES)), gtm_m], axis=1)
    gfm_p = jnp.concatenate([jnp.broadcast_to(pad_gate[None, :, None], (b, N_GATES, pad)), gfm_m], axis=2)
    s_zero = jnp.zeros((b, ML_HEADS, ML_DIM, 2 * LANES), F32)
    m_zero = jnp.zeros((b, SUBLANES, LANES), F32)
    _, _, s0, m0 = _mlstm(front(qm_m), front(km_m), front(vm_m), gtm_p, gfm_p, s_zero, m_zero)
    hf, hb, _, _ = _mlstm(qm, km, vm, g_tm, g_fm, s0, m0)

    h1, xn2, r2, e2, n1, e1 = _mix(x, att, hf, hb, om, w)
    return _peer(xn2, h1, r2, e2, n1, e1, w)


def kernel(x_prompt, x_sample, meta, g_mix, w_in, b_gates, conv_w, conv_b, g_qn, g_kn, lam_q1, lam_k1,
           lam_q2, lam_k2, g_sub, g_mh, w_out, g_ffn, w_pq, sub_k1, sub_k2, peer_u, peer_v):
    w = _prep_weights(g_mix, w_in, b_gates, conv_w, conv_b, g_qn, g_kn, lam_q1, lam_k1, lam_q2, lam_k2,
                      g_sub, g_mh, w_out, g_ffn, w_pq, sub_k1, sub_k2, peer_u, peer_v)
    return (_trunk(x_prompt, meta, w), _trunk(x_sample, meta, w))
```

```python
import functools
import math

import jax
import jax.numpy as jnp
from jax import lax
from jax.experimental import pallas as pl
from jax.experimental.pallas import tpu as pltpu

F32 = jnp.float32
BF16 = jnp.bfloat16

D_MODEL = 1024
N_META = 16
ATT_HEADS = 4
ATT_QK_DIM = 64
ATT_V_DIM = 128
ATT_WIDTH = ATT_HEADS * ATT_V_DIM
ML_HEADS = 4
ML_DIM = 128
ML_WIDTH = ML_HEADS * ML_DIM
N_GATES = 4 * ML_HEADS
N_MAIN = 3 * ATT_WIDTH + 4 * ML_WIDTH
ROT_DIM = ATT_QK_DIM // 4
ROPE_THETA = 500000.0
CHUNK = 128
PEER_HEADS = 8
PEER_NKEYS = 128
PEER_QDIM = 256
PEER_TOPK = 16
EPS = 1e-6
LAM_INIT = 0.8 - 0.6 * math.exp(-0.3 * 0)
Q_SCALE = ATT_QK_DIM ** -0.5 * math.log2(math.e)

LANES = 128
SUBLANES = 8
HALO = SUBLANES
VMEM_LIMIT = 56 * 1024 * 1024

PROJ_TILE = 512
ATT_TQ = 512
ATT_TK = 1024
MIX_TILE = 512
ROUTE_SLAB = 128
PEER_TILE = 512
PEER_GROUP = 2048
PEER_PASS_LANES = 256
PEER_VBLOCK = 256
NEG_INF = float("-inf")


def _cparams(sem):
    return pltpu.CompilerParams(dimension_semantics=sem, vmem_limit_bytes=VMEM_LIMIT)


def _const_spec(shape):
    nd = len(shape)
    return pl.BlockSpec(shape, lambda *_: (0,) * nd)


def _log_sigmoid(x):
    return -(jnp.maximum(-x, 0.0) + jnp.log1p(jnp.exp(-jnp.abs(x))))


def _proj_kernel(pos_base, tt,
                 x_ref, prev_ref, next_ref, gmix_ref, wmain_ref, wg_ref, wgt_ref, bg_ref, bgt_ref,
                 gqk_ref, inv_ref, convw_ref, convb_ref,
                 q_ref, k_ref, v_ref, qm_ref, km_ref, vm_ref, om_ref, gtm_ref, gfm_ref,
                 xe_ref, zc_ref):
    i = pl.program_id(1)
    xe_ref[0:HALO, :] = prev_ref[0, 0]
    xe_ref[HALO:HALO + tt, :] = x_ref[0]
    xe_ref[HALO + tt:2 * HALO + tt, :] = next_ref[0, 0]
    xe = xe_ref[...]
    ms = jnp.mean(xe * xe, axis=-1, keepdims=True)
    xn_ext = (xe * lax.rsqrt(ms + EPS) * gmix_ref[...]).astype(BF16)
    xn = xn_ext[HALO:HALO + tt]

    def mm(a, lo, hi):
        return jnp.dot(a, wmain_ref[:, lo:hi], preferred_element_type=F32)

    zqk = mm(xn, 0, 2 * ATT_WIDTH)
    sq = (zqk * zqk).astype(BF16)
    r = lax.broadcasted_iota(jnp.int32, (2 * LANES, 2 * LANES), 0) // ATT_QK_DIM
    c = lax.broadcasted_iota(jnp.int32, (2 * LANES, 2 * LANES), 1) // ATT_QK_DIM
    seg_ones = (r == c).astype(BF16)
    lane = lax.broadcasted_iota(jnp.int32, (tt, LANES), 1)
    l64 = lane % ATT_QK_DIM
    row = lax.broadcasted_iota(jnp.int32, (tt, LANES), 0)
    pos = (pos_base + i * tt + row).astype(F32)
    ang = pos * inv_ref[...]
    cos_a = jnp.cos(ang)
    sin_a = jnp.sin(ang)
    half = ROT_DIM // 2
    cos_m = jnp.where(l64 < ROT_DIM, cos_a, 1.0)
    sin_lo = jnp.where(l64 < half, -sin_a, 0.0)
    sin_hi = jnp.where((l64 >= half) & (l64 < ROT_DIM), sin_a, 0.0)
    for cb in range(4):
        lo = cb * 2 * LANES
        ss = jnp.dot(sq[:, lo:lo + 2 * LANES], seg_ones, preferred_element_type=F32)
        rn = lax.rsqrt(ss * (1.0 / ATT_QK_DIM) + EPS)
        qn = zqk[:, lo:lo + 2 * LANES] * rn * gqk_ref[:, lo:lo + 2 * LANES]
        for sb in range(2):
            xs = qn[:, sb * LANES:(sb + 1) * LANES]
            rot = (xs * cos_m + pltpu.roll(xs, LANES - half, 1) * sin_lo
                   + pltpu.roll(xs, half, 1) * sin_hi)
            col = lo + sb * LANES
            if col < ATT_WIDTH:
                q_ref[0, :, col:col + LANES] = (rot * Q_SCALE).astype(BF16)
            else:
                k_ref[0, :, col - ATT_WIDTH:col - ATT_WIDTH + LANES] = rot.astype(BF16)

    va = mm(xn, 2 * ATT_WIDTH, 3 * ATT_WIDTH).astype(BF16)
    for h in range(ATT_HEADS):
        v_ref[0, :, 2 * h * LANES:(2 * h + 1) * LANES] = va[:, h * ATT_V_DIM:(h + 1) * ATT_V_DIM]
        v_ref[0, :, (2 * h + 1) * LANES:(2 * h + 2) * LANES] = jnp.ones((tt, LANES), BF16)

    base = 3 * ATT_WIDTH
    zc_ref[...] = mm(xn_ext, base, base + 2 * ML_WIDTH)
    y = (convw_ref[0:1, :] * zc_ref[pl.ds(HALO - 1, tt), :]
         + convw_ref[1:2, :] * zc_ref[pl.ds(HALO, tt), :]
         + convw_ref[2:3, :] * zc_ref[pl.ds(HALO + 1, tt), :]
         + convb_ref[...])
    act = y * (1.0 / (1.0 + jnp.exp(-y)))
    qm_ref[0] = act[:, :ML_WIDTH].astype(BF16)
    km_ref[0] = (act[:, ML_WIDTH:] * (ML_DIM ** -0.5)).astype(BF16)
    vm_ref[0] = mm(xn, base + 2 * ML_WIDTH, base + 3 * ML_WIDTH).astype(BF16)
    om_ref[0] = mm(xn, base + 3 * ML_WIDTH, base + 4 * ML_WIDTH).astype(BF16)

    g_tm = jnp.dot(xn, wg_ref[...], preferred_element_type=F32) + bg_ref[...]
    g_fm = lax.dot_general(wgt_ref[...], xn, (((1,), (1,)), ((), ())),
                           preferred_element_type=F32) + bgt_ref[...]
    col_tm = lax.broadcasted_iota(jnp.int32, (tt, N_GATES), 1)
    row_fm = lax.broadcasted_iota(jnp.int32, (N_GATES, tt), 0)
    gtm_ref[0] = jnp.where((col_tm // ML_HEADS) % 2 == 1, _log_sigmoid(g_tm), g_tm)
    gfm_ref[0] = jnp.where((row_fm // ML_HEADS) % 2 == 1, _log_sigmoid(g_fm), g_fm)


def _proj(x, prev, nxt, pos_base, tt, w):
    b, s, d = x.shape
    nt = s // tt
    tok = lambda width, dt: jax.ShapeDtypeStruct((b, s, width), dt)
    out_shape = ([tok(ATT_WIDTH, BF16)] * 2 + [tok(2 * ATT_WIDTH, BF16)] + [tok(ML_WIDTH, BF16)] * 4
                 + [tok(N_GATES, F32), jax.ShapeDtypeStruct((b, N_GATES, s), F32)])
    tspec = lambda width: pl.BlockSpec((1, tt, width), lambda bi, ti: (bi, ti, 0))
    hspec = pl.BlockSpec((1, 1, HALO, d), lambda bi, ti: (bi, ti, 0, 0))
    return pl.pallas_call(
        functools.partial(_proj_kernel, pos_base, tt),
        grid=(b, nt),
        in_specs=[tspec(d), hspec, hspec, _const_spec((1, d)), _const_spec((d, N_MAIN)),
                  _const_spec((d, N_GATES)), _const_spec((N_GATES, d)), _const_spec((1, N_GATES)),
                  _const_spec((N_GATES, 1)), _const_spec((1, 2 * ATT_WIDTH)), _const_spec((1, LANES)),
                  _const_spec((3, 2 * ML_WIDTH)), _const_spec((1, 2 * ML_WIDTH))],
        out_specs=[tspec(ATT_WIDTH)] * 2 + [tspec(2 * ATT_WIDTH)] + [tspec(ML_WIDTH)] * 4
        + [tspec(N_GATES), pl.BlockSpec((1, N_GATES, tt), lambda bi, ti: (bi, 0, ti))],
        out_shape=out_shape,
        scratch_shapes=[pltpu.VMEM((tt + 2 * HALO, d), F32), pltpu.VMEM((tt + 2 * HALO, 2 * ML_WIDTH), F32)],
        compiler_params=_cparams(("parallel", "parallel")),
        name=f"proj_t{tt}",
    )(x, prev, nxt, w["g_mix"], w["w_main"], w["w_g"], w["w_gt"], w["b_g"], w["b_gt"],
      w["g_qk"], w["inv_lane"], w["conv_w"], w["conv_b"])


def _attn_kernel(tq, tk, nk,
                 q_ref, k_ref, v_ref, kt_ref, vt_ref, lam_ref, gsub_ref, o_ref,
                 qq_ref, m_ref, acc_ref, sa_ref, sb_ref):
    q = q_ref[0]
    lane = lax.broadcasted_iota(jnp.int32, (tq, LANES), 1)
    zero = jnp.zeros_like(q)
    qq_ref[0:tq, :] = jnp.where(lane < ATT_QK_DIM, q, zero)
    qq_ref[tq:2 * tq, :] = jnp.where(lane >= ATT_QK_DIM, q, zero)
    m_ref[...] = jnp.full(m_ref.shape, NEG_INF, F32)
    acc_ref[...] = jnp.zeros(acc_ref.shape, F32)

    def scores(kb):
        return lax.dot_general(qq_ref[...], kb, (((1,), (1,)), ((), ())), preferred_element_type=F32)

    def update(s, vb):
        m_prev = m_ref[...]
        m_new = jnp.maximum(m_prev, jnp.max(s, axis=1, keepdims=True))
        alpha = jnp.exp2(m_prev - m_new)
        p = jnp.exp2(s - m_new[:, 0:1]).astype(BF16)
        pv = jnp.dot(p, vb, preferred_element_type=F32)
        acc_ref[:, 0:LANES] = alpha * acc_ref[:, 0:LANES] + pv[:, 0:LANES]
        acc_ref[:, LANES:2 * LANES] = alpha * acc_ref[:, LANES:2 * LANES] + pv[:, LANES:2 * LANES]
        m_ref[...] = m_new

    def kblock(j):
        return k_ref[0, pl.ds(pl.multiple_of(j * tk, tk), tk), :]

    def vblock(j):
        return v_ref[0, pl.ds(pl.multiple_of(j * tk, tk), tk), :]

    sa_ref[...] = scores(kblock(0))

    def body(i, carry):
        j = 2 * i
        sb_ref[...] = scores(kblock(j + 1))
        update(sa_ref[...], vblock(j))
        sa_ref[...] = scores(kblock(j + 2))
        update(sb_ref[...], vblock(j + 1))
        return carry

    lax.fori_loop(0, nk // 2 - 1, body, 0)
    sb_ref[...] = scores(kblock(nk - 1))
    update(sa_ref[...], vblock(nk - 2))
    col = lax.broadcasted_iota(jnp.int32, (2 * tq, LANES), 1)
    s_meta = jnp.where(col < N_META, scores(kt_ref[0]), NEG_INF)
    update(sb_ref[...], vblock(nk - 1))
    update(s_meta, vt_ref[0])

    o1 = acc_ref[0:tq, 0:LANES] / acc_ref[0:tq, LANES:2 * LANES]
    o2 = acc_ref[tq:2 * tq, 0:LANES] / acc_ref[tq:2 * tq, LANES:2 * LANES]
    lam = (jnp.exp(jnp.sum(lam_ref[0:1, :] * lam_ref[1:2, :], axis=1, keepdims=True))
           - jnp.exp(jnp.sum(lam_ref[2:3, :] * lam_ref[3:4, :], axis=1, keepdims=True)) + LAM_INIT)
    o = o1 - lam * o2
    ms = jnp.mean(o * o, axis=-1, keepdims=True)
    o_ref[0] = (o * lax.rsqrt(ms + EPS) * gsub_ref[...] * (1.0 - LAM_INIT)).astype(BF16)


def _att_key_block(s):
    resident = s * 3 * LANES * 2 * 2
    for tk in (2 * ATT_TK, ATT_TK):
        scores = 2 * (2 * ATT_TQ) * tk * 4
        if s % (2 * tk) == 0 and resident + scores <= VMEM_LIMIT * 4 // 5:
            return tk
    raise ValueError(f"sequence length {s} is not a multiple of {2 * ATT_TK}")


def _attn(q, k, v, kt, vt, lam4, g_sub):
    b, s, _ = q.shape
    tq, tk = ATT_TQ, _att_key_block(s)
    kspec = lambda rows: pl.BlockSpec((1, rows, LANES), lambda bi, hi, qi: (bi, 0, hi))
    vspec = lambda rows: pl.BlockSpec((1, rows, 2 * LANES), lambda bi, hi, qi: (bi, 0, hi))
    return pl.pallas_call(
        functools.partial(_attn_kernel, tq, tk, s // tk),
        grid=(b, ATT_HEADS, s // tq),
        in_specs=[pl.BlockSpec((1, tq, LANES), lambda bi, hi, qi: (bi, qi, hi)),
                  kspec(s), vspec(s), kspec(LANES), vspec(LANES),
                  _const_spec((4, ATT_QK_DIM)), _const_spec((1, ATT_V_DIM))],
        out_specs=pl.BlockSpec((1, tq, LANES), lambda bi, hi, qi: (bi, qi, hi)),
        out_shape=jax.ShapeDtypeStruct((b, s, ATT_WIDTH), BF16),
        scratch_shapes=[pltpu.VMEM((2 * tq, LANES), BF16), pltpu.VMEM((2 * tq, LANES), F32),
                        pltpu.VMEM((2 * tq, 2 * LANES), F32),
                        pltpu.VMEM((2 * tq, tk), F32), pltpu.VMEM((2 * tq, tk), F32)],
        compiler_params=_cparams(("parallel", "parallel", "parallel")),
        name="diff_attn",
    )(q, k, v, kt, vt, lam4, g_sub)


def _mlstm_chunks(dirs):
    r = lax.broadcasted_iota(jnp.int32, (CHUNK, CHUNK), 0)
    c = lax.broadcasted_iota(jnp.int32, (CHUNK, CHUNK), 1)
    ones_col = (c == 0).astype(BF16)
    chains = []
    for reverse, q, k, v, g_tm, g_fm, s_ref, m_ref, h_ref in dirs:
        causal = (c >= r) if reverse else (c <= r)
        last = 0 if reverse else CHUNK - 1
        gbase = 2 * ML_HEADS if reverse else 0
        b_cols = jnp.dot(causal.astype(F32), g_tm, preferred_element_type=F32,
                         precision=lax.Precision.HIGHEST)
        b_rows = jnp.dot(g_fm, ((r >= c) if reverse else (r <= c)).astype(F32),
                         preferred_element_type=F32, precision=lax.Precision.HIGHEST)
        for h in range(ML_HEADS):
            fc = gbase + ML_HEADS + h
            sl = slice(h * ML_DIM, (h + 1) * ML_DIM)
            chains.append(dict(
                causal=causal, last=last, h=h, sl=sl, s_ref=s_ref, m_ref=m_ref, h_ref=h_ref,
                q=q[:, sl], k=k[:, sl], v=v[:, sl],
                b_col=b_cols[:, fc:fc + 1], b_row=b_rows[fc:fc + 1, :],
                li_row=g_fm[gbase + h:gbase + h + 1, :], li_col=g_tm[:, gbase + h:gbase + h + 1]))
    n = range(len(chains))
    ch = chains
    m_prev = [ch[i]["m_ref"][ch[i]["h"]:ch[i]["h"] + 1, 0:1] for i in n]
    log_d = [jnp.where(ch[i]["causal"], ch[i]["b_col"] - ch[i]["b_row"] + ch[i]["li_row"], NEG_INF) for i in n]
    m_inter = [ch[i]["b_col"] + m_prev[i] for i in n]
    mj = [jnp.maximum(m_inter[i], jnp.max(log_d[i], axis=1, keepdims=True)) for i in n]
    vext = [jnp.concatenate([ch[i]["v"], ones_col], axis=1) for i in n]
    qk = [lax.dot_general(ch[i]["q"], ch[i]["k"], (((1,), (1,)), ((), ())), preferred_element_type=F32)
          for i in n]
    state = [ch[i]["s_ref"][ch[i]["h"]] for i in n]
    inter = [jnp.dot(ch[i]["q"], state[i].astype(BF16), preferred_element_type=F32) for i in n]
    b_last = [ch[i]["b_col"][ch[i]["last"]:ch[i]["last"] + 1, :] for i in n]
    m_new = [jnp.maximum(b_last[i] + m_prev[i],
                         jnp.max(b_last[i] - ch[i]["b_row"] + ch[i]["li_row"], axis=1, keepdims=True)) for i in n]
    w = [(qk[i] * jnp.exp(log_d[i] - mj[i])).astype(BF16) for i in n]
    kw = [(ch[i]["k"].astype(F32)
           * jnp.exp(b_last[i] - ch[i]["b_col"] + ch[i]["li_col"] - m_new[i])).astype(BF16) for i in n]
    tot = [jnp.exp(m_inter[i] - mj[i]) * inter[i] + jnp.dot(w[i], vext[i], preferred_element_type=F32)
           for i in n]
    upd = [lax.dot_general(kw[i], vext[i], (((0,), (0,)), ((), ())), preferred_element_type=F32) for i in n]
    for i in n:
        den = tot[i][:, ML_DIM:ML_DIM + 1]
        ch[i]["h_ref"][0, :, ch[i]["sl"]] = tot[i][:, :ML_DIM] / jnp.maximum(jnp.abs(den), jnp.exp(-mj[i]))
    for i in n:
        h = ch[i]["h"]
        ch[i]["s_ref"][h] = jnp.exp(b_last[i] + m_prev[i] - m_new[i]) * state[i] + upd[i]
        ch[i]["m_ref"][h:h + 1, :] = jnp.broadcast_to(m_new[i], (1, LANES))


def _mlstm_kernel(qf_ref, kf_ref, vf_ref, gtf_ref, gff_ref, qb_ref, kb_ref, vb_ref, gtb_ref, gfb_ref,
                  s0_ref, m0_ref, hf_ref, hb_ref, sfin_ref, mfin_ref,
                  sf_ref, sb_ref, mf_ref, mb_ref):
    j = pl.program_id(1)

    @pl.when(j == 0)
    def _():
        sf_ref[...] = s0_ref[0]
        mf_ref[...] = m0_ref[0]
        sb_ref[...] = jnp.zeros(sb_ref.shape, F32)
        mb_ref[...] = jnp.zeros(mb_ref.shape, F32)

    _mlstm_chunks([(False, qf_ref[0], kf_ref[0], vf_ref[0], gtf_ref[0], gff_ref[0], sf_ref, mf_ref, hf_ref)])
    _mlstm_chunks([(True, qb_ref[0], kb_ref[0], vb_ref[0], gtb_ref[0], gfb_ref[0], sb_ref, mb_ref, hb_ref)])

    @pl.when(j == pl.num_programs(1) - 1)
    def _():
        sfin_ref[0] = sf_ref[...]
        mfin_ref[0] = mf_ref[...]


def _mlstm(qm, km, vm, g_tm, g_fm, s0, m0):
    b, s, _ = qm.shape
    nc = s // CHUNK
    fwd = lambda width: pl.BlockSpec((1, CHUNK, width), lambda bi, ci: (bi, ci, 0))
    bwd = lambda width: pl.BlockSpec((1, CHUNK, width), lambda bi, ci: (bi, nc - 1 - ci, 0))
    gf_fwd = pl.BlockSpec((1, N_GATES, CHUNK), lambda bi, ci: (bi, 0, ci))
    gf_bwd = pl.BlockSpec((1, N_GATES, CHUNK), lambda bi, ci: (bi, 0, nc - 1 - ci))
    st = pl.BlockSpec((1, ML_HEADS, ML_DIM, 2 * LANES), lambda bi, ci: (bi, 0, 0, 0))
    mx = pl.BlockSpec((1, SUBLANES, LANES), lambda bi, ci: (bi, 0, 0))
    return pl.pallas_call(
        _mlstm_kernel,
        grid=(b, nc),
        in_specs=[fwd(ML_WIDTH)] * 3 + [fwd(N_GATES), gf_fwd] + [bwd(ML_WIDTH)] * 3 + [bwd(N_GATES), gf_bwd]
        + [st, mx],
        out_specs=[fwd(ML_WIDTH), bwd(ML_WIDTH), st, mx],
        out_shape=[jax.ShapeDtypeStruct((b, s, ML_WIDTH), F32)] * 2
        + [jax.ShapeDtypeStruct((b, ML_HEADS, ML_DIM, 2 * LANES), F32),
           jax.ShapeDtypeStruct((b, SUBLANES, LANES), F32)],
        scratch_shapes=[pltpu.VMEM((ML_HEADS, ML_DIM, 2 * LANES), F32)] * 2
        + [pltpu.VMEM((SUBLANES, LANES), F32)] * 2,
        compiler_params=_cparams(("parallel", "arbitrary")),
        name="mlstm",
    )(qm, km, vm, g_tm, g_fm, qm, km, vm, g_tm, g_fm, s0, m0)


def _sorting_network(n):
    pairs = []
    p = 1
    while p < n:
        k = p
        while k >= 1:
            for j in range(k % p, n - k, 2 * k):
                for i in range(min(k, n - j - k)):
                    if (i + j) // (2 * p) == (i + j + k) // (2 * p):
                        pairs.append((i + j, i + j + k))
            k //= 2
        p *= 2
    return pairs


def _top_values_sorted(xs, n):
    sorted_groups = []
    for x in xs:
        groups = [x[SUBLANES * g:SUBLANES * (g + 1)] for g in range(x.shape[0] // SUBLANES)]
        size = 1
        while size < len(groups):
            size *= 2
        groups += [jnp.full_like(groups[0], NEG_INF)] * (size - len(groups))
        for i, j in _sorting_network(size):
            groups[i], groups[j] = jnp.maximum(groups[i], groups[j]), jnp.minimum(groups[i], groups[j])
        sorted_groups.append(groups)
    vals = [[] for _ in xs]
    for r in range(n):
        for groups, out in zip(sorted_groups, vals):
            mx = jnp.max(groups[0], axis=0, keepdims=True)
            out.append(mx)
            if r + 1 < n:
                hit = groups[0] == mx
                live = min(len(groups), n - 1 - r)
                for d in range(live):
                    below = groups[d + 1] if d + 1 < len(groups) else NEG_INF
                    groups[d] = jnp.where(hit, below, groups[d])
    return vals


def _count_greater(vals, x):
    n = len(vals)

    def pivot(decided, lo, size):
        if not decided:
            return vals[lo + size // 2 - 1]
        half = size // 2
        return jnp.where(decided[0], pivot(decided[1:], lo + half, half), pivot(decided[1:], lo, half))

    decided = []
    count = jnp.zeros_like(x)
    size = n
    while size >= 2:
        above = pivot(decided, 0, n) > x
        decided.append(above)
        count = count + jnp.where(above, float(size // 2), 0.0)
        size //= 2
    return jnp.where(vals[n - 1] > x, float(n), count)


def _bf16_pair_bits(x):
    u = lax.bitcast_convert_type(x.astype(BF16).astype(F32), jnp.uint32)
    return u | (u >> 16)


_CAND_PAIRS = [(a, b) for a in range(PEER_TOPK + 1) for b in range(PEER_TOPK + 1)
               if (a + 1) * (b + 1) <= PEER_TOPK + 1]
_CAND_ROWS = -(-len(_CAND_PAIRS) // SUBLANES) * SUBLANES


def _mix_kernel(tt, x_ref, att_ref, hf_ref, hb_ref, om_ref, gmh_ref, wout_ref, gffn_ref, wpqt_ref,
                k1_ref, k2_ref, h1_ref, xn2_ref, r2_ref, e2_ref, n1_ref, e1_ref, mix_ref, cand_ref):
    mix_ref[:, 0:ATT_WIDTH] = att_ref[0]
    for h in range(ML_HEADS):
        sl = slice(h * ML_DIM, (h + 1) * ML_DIM)
        hs = hf_ref[0, :, sl] + hb_ref[0, :, sl]
        ms = jnp.mean(hs * hs, axis=-1, keepdims=True)
        hn = hs * lax.rsqrt(ms + EPS) * gmh_ref[...]
        gate = 1.0 / (1.0 + jnp.exp(-om_ref[0, :, sl].astype(F32)))
        mix_ref[:, ATT_WIDTH + h * ML_DIM:ATT_WIDTH + (h + 1) * ML_DIM] = (hn * gate).astype(BF16)
    h1 = x_ref[0] + jnp.dot(mix_ref[...], wout_ref[...], preferred_element_type=F32)
    h1_ref[0] = h1
    ms = jnp.mean(h1 * h1, axis=-1, keepdims=True)
    xn2 = (h1 * lax.rsqrt(ms + EPS) * gffn_ref[...]).astype(BF16)
    xn2_ref[0] = xn2
    qt = lax.dot_general(wpqt_ref[...], xn2, (((1,), (1,)), ((), ())), preferred_element_type=F32)
    half = PEER_QDIM // 2
    cand_ref[...] = jnp.full(cand_ref.shape, NEG_INF, F32)
    for h in range(PEER_HEADS):
        q1 = qt[h * PEER_QDIM:h * PEER_QDIM + half].astype(BF16)
        q2 = qt[h * PEER_QDIM + half:(h + 1) * PEER_QDIM].astype(BF16)
        s1 = jnp.dot(k1_ref[h], q1, preferred_element_type=F32)
        s2 = jnp.dot(k2_ref[h], q2, preferred_element_type=F32)
        for sl in range(tt // ROUTE_SLAB):
            lanes = slice(sl * ROUTE_SLAB, (sl + 1) * ROUTE_SLAB)
            a1 = s1[:, lanes]
            a2 = s2[:, lanes]
            v1, v2 = _top_values_sorted([a1, a2], PEER_TOPK + 1)
            rank2 = _count_greater(v2[:PEER_TOPK], a2)
            for idx, (a, b) in enumerate(_CAND_PAIRS):
                cand_ref[idx:idx + 1, lanes] = v1[a] + v2[b]
            cs, = _top_values_sorted([cand_ref[:, lanes]], PEER_TOPK + 1)
            z = jnp.zeros_like(cs[0])
            for i in range(PEER_TOPK):
                z = z + jnp.exp(cs[i] - cs[0])
            thr = 0.5 * (cs[PEER_TOPK - 1] + cs[PEER_TOPK])
            n1 = _count_greater(v2[:PEER_TOPK // 2], thr - a1)
            n_best = jnp.zeros_like(thr)
            for b in range(PEER_TOPK // 2, PEER_TOPK):
                n_best = n_best + jnp.where(v2[b] > thr - v1[0], 1.0, 0.0)
            n1 = n1 + jnp.where(a1 == v1[0], n_best, 0.0)
            r2_ref[0, h, :, lanes] = rank2.astype(BF16)
            n1_ref[0, h, :, lanes] = _bf16_pair_bits(n1)
            e1_ref[0, h, :, lanes] = _bf16_pair_bits(jnp.exp(a1 - v1[0]) * (0.5 / z))
            e2_ref[0, h, :, lanes] = jnp.exp(a2 - v2[0]).astype(BF16)


def _mix(x, att, hf, hb, om, w):
    b, s, d = x.shape
    tt = MIX_TILE
    nt = s // tt
    tspec = lambda width: pl.BlockSpec((1, tt, width), lambda bi, ti: (bi, ti, 0))
    sspec = pl.BlockSpec((1, PEER_HEADS, PEER_NKEYS, tt), lambda bi, ti: (bi, 0, 0, ti))
    return pl.pallas_call(
        functools.partial(_mix_kernel, tt),
        grid=(b, nt),
        in_specs=[tspec(d), tspec(ATT_WIDTH), tspec(ML_WIDTH), tspec(ML_WIDTH), tspec(ML_WIDTH),
                  _const_spec((1, ML_DIM)), _const_spec((d, d)), _const_spec((1, d)),
                  _const_spec((PEER_HEADS * PEER_QDIM, d)),
                  _const_spec((PEER_HEADS, PEER_NKEYS, PEER_QDIM // 2)),
                  _const_spec((PEER_HEADS, PEER_NKEYS, PEER_QDIM // 2))],
        out_specs=[tspec(d), tspec(d), sspec, sspec, sspec, sspec],
        out_shape=[jax.ShapeDtypeStruct((b, s, d), F32), jax.ShapeDtypeStruct((b, s, d), BF16),
                   jax.ShapeDtypeStruct((b, PEER_HEADS, PEER_NKEYS, s), BF16),
                   jax.ShapeDtypeStruct((b, PEER_HEADS, PEER_NKEYS, s), BF16),
                   jax.ShapeDtypeStruct((b, PEER_HEADS, PEER_NKEYS, s), jnp.uint32),
                   jax.ShapeDtypeStruct((b, PEER_HEADS, PEER_NKEYS, s), jnp.uint32)],
        scratch_shapes=[pltpu.VMEM((tt, d), BF16), pltpu.VMEM((_CAND_ROWS, tt), F32)],
        compiler_params=_cparams(("parallel", "parallel")),
        name="mix_route",
    )(x, att, hf, hb, om, w["g_mh"], w["w_out"], w["g_ffn"], w["w_pqt"], w["sub_k1"], w["sub_k2"])


def _gelu_x2(x):
    return x + x * lax.erf(x * (2.0 ** -0.5))


def _peer_kernel(tt, ng, xn2_ref, h1_ref, r2_ref, e2_ref, n1_ref, e1_ref, u_ref, vt_ref, y_ref,
                 a0_ref, a1_ref, w_ref, acc_ref):
    s = pl.program_id(1)
    rows = PEER_GROUP // PEER_NKEYS

    @pl.when(s == 0)
    def _():
        a0_ref[...] = jnp.zeros(a0_ref.shape, F32)
        a1_ref[...] = jnp.zeros(a1_ref.shape, F32)

    @pl.when((s < 1) | ((s + ng - 1) % ng == 0))
    def _():
        acc_ref[...] = jnp.zeros(acc_ref.shape, F32)

    g2 = (s + ng - 1) % ng

    def row(ref, h, c, lanes):
        words = jnp.broadcast_to(ref[0, h, pl.ds(c, 1), lanes], (SUBLANES, PEER_PASS_LANES))
        tile = pltpu.bitcast(words, BF16)
        return jnp.concatenate([tile] * (PEER_NKEYS // (2 * SUBLANES)), axis=0)

    def stages(a_cur, a_oth):
        zero = jnp.zeros((PEER_NKEYS, PEER_PASS_LANES), BF16)
        tok_half = tt // 2
        for kb in range(PEER_GROUP // PEER_VBLOCK):
            for cc in range(kb * PEER_VBLOCK // PEER_NKEYS, (kb + 1) * PEER_VBLOCK // PEER_NKEYS):
                c = g2 * rows + cc
                ex = slice(cc * PEER_NKEYS, (cc + 1) * PEER_NKEYS)
                for lb in range(tt // PEER_PASS_LANES):
                    lanes = pl.ds(lb * PEER_PASS_LANES, PEER_PASS_LANES)
                    wsum = zero
                    for h in range(PEER_HEADS):
                        sel = r2_ref[0, h, :, lanes] < row(n1_ref, h, c, lanes)
                        wsum = wsum + jnp.where(sel, e2_ref[0, h, :, lanes], zero) * row(e1_ref, h, c, lanes)
                    w_ref[ex, lanes] = wsum * _gelu_x2(a_oth[ex, lanes]).astype(BF16)
            if kb < 2:
                tk = slice(kb * tok_half, (kb + 1) * tok_half)
                a_cur[:, tk] = lax.dot_general(u_ref[...], xn2_ref[0, tk, :], (((1,), (1,)), ((), ())),
                                               preferred_element_type=F32)
            blk = slice(kb * PEER_VBLOCK, (kb + 1) * PEER_VBLOCK)
            acc_ref[...] += jnp.dot(vt_ref[0, :, blk], w_ref[blk, :], preferred_element_type=F32)

    @pl.when(s % 2 == 0)
    def _():
        stages(a0_ref, a1_ref)

    @pl.when(s % 2 == 1)
    def _():
        stages(a1_ref, a0_ref)

    @pl.when((s >= 1) & ((s + ng - 1) % ng == ng - 1))
    def _():
        y_ref[0] = h1_ref[0] + acc_ref[...].T


def _peer(xn2, h1, r2, e2, n1, e1, w):
    b, s, d = h1.shape
    tt = PEER_TILE
    nt = s // tt
    ng = (PEER_NKEYS * PEER_NKEYS) // PEER_GROUP
    tile = lambda si, lag: jnp.clip((si - lag) // ng, 0, nt - 1)
    tspec = lambda lag: pl.BlockSpec((1, tt, d), lambda bi, si: (bi, tile(si, lag), 0))
    sspec = pl.BlockSpec((1, PEER_HEADS, PEER_NKEYS, tt), lambda bi, si: (bi, 0, 0, tile(si, 1)))
    return pl.pallas_call(
        functools.partial(_peer_kernel, tt, ng),
        grid=(b, nt * ng + 1),
        in_specs=[tspec(0), tspec(1), sspec, sspec, sspec, sspec,
                  pl.BlockSpec((PEER_GROUP, d), lambda bi, si: (si % ng, 0)),
                  pl.BlockSpec((1, d, PEER_GROUP), lambda bi, si: ((si + ng - 1) % ng, 0, 0))],
        out_specs=tspec(1),
        out_shape=jax.ShapeDtypeStruct((b, s, d), F32),
        scratch_shapes=[pltpu.VMEM((PEER_GROUP, tt), F32)] * 2 + [pltpu.VMEM((PEER_GROUP, tt), BF16)]
        + [pltpu.VMEM((d, tt), F32)],
        compiler_params=_cparams(("parallel", "arbitrary")),
        name="peer_dense",
    )(xn2, h1, r2, e2, n1, e1, w["peer_u"], w["peer_vt"])


def _prep_weights(g_mix, w_in, b_gates, conv_w, conv_b, g_qn, g_kn, lam_q1, lam_k1, lam_q2, lam_k2,
                  g_sub, g_mh, w_out, g_ffn, w_pq, sub_k1, sub_k2, peer_u, peer_v):
    half = ROT_DIM // 2
    inv = ROPE_THETA ** (-jnp.arange(half, dtype=F32) * 2.0 / ROT_DIM)
    w_g = w_in[0][:, N_MAIN:]
    return {
        "g_mix": g_mix[0][None, :],
        "w_main": w_in[0][:, :N_MAIN].astype(BF16),
        "w_g": w_g.astype(BF16),
        "w_gt": w_g.T.astype(BF16),
        "b_g": b_gates[0][None, :],
        "b_gt": b_gates[0][:, None],
        "g_qk": jnp.concatenate([jnp.tile(g_qn[0], ATT_WIDTH // ATT_QK_DIM),
                                 jnp.tile(g_kn[0], ATT_WIDTH // ATT_QK_DIM)])[None, :],
        "inv_lane": jnp.tile(inv, LANES // half)[None, :],
        "conv_w": conv_w[0][:, 0, :],
        "conv_b": conv_b[0][None, :],
        "lam4": jnp.stack([lam_q1[0], lam_k1[0], lam_q2[0], lam_k2[0]]),
        "g_sub": g_sub[0][None, :],
        "g_mh": g_mh[0][None, :],
        "w_out": w_out[0].astype(BF16),
        "g_ffn": g_ffn[0][None, :],
        "w_pqt": w_pq[0].T.astype(BF16),
        "sub_k1": sub_k1[0].astype(BF16),
        "sub_k2": sub_k2[0].astype(BF16),
        "peer_u": peer_u[0].astype(BF16),
        "peer_vt": peer_v[0].astype(BF16).reshape(-1, PEER_GROUP, D_MODEL).transpose(0, 2, 1),
    }


def _trunk(x, meta, w):
    b, s, d = x.shape
    tt = PROJ_TILE
    assert s % tt == 0 and s % (2 * ATT_TK) == 0 and s % MIX_TILE == 0 and s % PEER_TILE == 0
    nt = s // tt
    xr = x.reshape(b, nt, tt, d)
    meta_tail = jnp.broadcast_to(meta[N_META - HALO:][None, None], (b, 1, HALO, d))
    prev = jnp.concatenate([meta_tail, xr[:, :-1, tt - HALO:, :]], axis=1)
    nxt = jnp.concatenate([xr[:, 1:, :HALO, :], jnp.zeros((b, 1, HALO, d), x.dtype)], axis=1)
    q, k, v, qm, km, vm, om, g_tm, g_fm = _proj(x, prev, nxt, N_META, tt, w)

    meta_b = jnp.broadcast_to(meta[None], (b, N_META, d))
    m_prev = jnp.zeros((b, 1, HALO, d), x.dtype)
    m_next = x[:, None, :HALO, :]
    _, k_m, v_m, qm_m, km_m, vm_m, _, gtm_m, gfm_m = _proj(meta_b, m_prev, m_next, 0, N_META, w)

    pad = CHUNK - N_META
    att = _attn(q, k, v, jnp.pad(k_m, ((0, 0), (0, pad), (0, 0))), jnp.pad(v_m, ((0, 0), (0, pad), (0, 0))),
                w["lam4"], w["g_sub"])

    front = lambda a: jnp.pad(a, ((0, 0), (pad, 0), (0, 0)))
    is_input_gate = (jnp.arange(N_GATES) // ML_HEADS) % 2 == 0
    pad_gate = jnp.where(is_input_gate, NEG_INF, 0.0).astype(F32)
    gtm_p = jnp.concatenate([jnp.broadcast_to(pad_gate[None, None, :], (b, pad, N_GATES)), gtm_m], axis=1)
    gfm_p = jnp.concatenate([jnp.broadcast_to(pad_gate[None, :, None], (b, N_GATES, pad)), gfm_m], axis=2)
    s_zero = jnp.zeros((b, ML_HEADS, ML_DIM, 2 * LANES), F32)
    m_zero = jnp.zeros((b, SUBLANES, LANES), F32)
    _, _, s0, m0 = _mlstm(front(qm_m), front(km_m), front(vm_m), gtm_p, gfm_p, s_zero, m_zero)
    hf, hb, _, _ = _mlstm(qm, km, vm, g_tm, g_fm, s0, m0)

    h1, xn2, r2, e2, n1, e1 = _mix(x, att, hf, hb, om, w)
    return _peer(xn2, h1, r2, e2, n1, e1, w)


def kernel(x_prompt, x_sample, meta, g_mix, w_in, b_gates, conv_w, conv_b, g_qn, g_kn, lam_q1, lam_k1,
           lam_q2, lam_k2, g_sub, g_mh, w_out, g_ffn, w_pq, sub_k1, sub_k2, peer_u, peer_v):
    w = _prep_weights(g_mix, w_in, b_gates, conv_w, conv_b, g_qn, g_kn, lam_q1, lam_k1, lam_q2, lam_k2,
                      g_sub, g_mh, w_out, g_ffn, w_pq, sub_k1, sub_k2, peer_u, peer_v)
    return (_trunk(x_prompt, meta, w), _trunk(x_sample, meta, w))
```

```python
import functools
import math

import jax
import jax.numpy as jnp
from jax import lax
from jax.experimental import pallas as pl
from jax.experimental.pallas import tpu as pltpu

F32 = jnp.float32
BF16 = jnp.bfloat16

D_MODEL = 1024
N_META = 16
ATT_HEADS = 4
ATT_QK_DIM = 64
ATT_V_DIM = 128
ATT_WIDTH = ATT_HEADS * ATT_V_DIM
ML_HEADS = 4
ML_DIM = 128
ML_WIDTH = ML_HEADS * ML_DIM
N_GATES = 4 * ML_HEADS
N_MAIN = 3 * ATT_WIDTH + 4 * ML_WIDTH
ROT_DIM = ATT_QK_DIM // 4
ROPE_THETA = 500000.0
CHUNK = 128
PEER_HEADS = 8
PEER_NKEYS = 128
PEER_QDIM = 256
PEER_TOPK = 16
EPS = 1e-6
LAM_INIT = 0.8 - 0.6 * math.exp(-0.3 * 0)
Q_SCALE = ATT_QK_DIM ** -0.5 * math.log2(math.e)

LANES = 128
SUBLANES = 8
HALO = SUBLANES
VMEM_LIMIT = 56 * 1024 * 1024

PROJ_TILE = 512
ATT_TQ = 512
ATT_TK = 1024
MIX_TILE = 512
ROUTE_SLAB = 128
PEER_TILE = 512
PEER_GROUP = 2048
PEER_PASS_LANES = 256
PEER_VBLOCK = 512
NEG_INF = float("-inf")


def _cparams(sem):
    return pltpu.CompilerParams(dimension_semantics=sem, vmem_limit_bytes=VMEM_LIMIT)


def _const_spec(shape):
    nd = len(shape)
    return pl.BlockSpec(shape, lambda *_: (0,) * nd)


def _log_sigmoid(x):
    return -(jnp.maximum(-x, 0.0) + jnp.log1p(jnp.exp(-jnp.abs(x))))


def _proj_kernel(pos_base, tt,
                 x_ref, prev_ref, next_ref, gmix_ref, wmain_ref, wg_ref, wgt_ref, bg_ref, bgt_ref,
                 gqk_ref, inv_ref, convw_ref, convb_ref,
                 q_ref, k_ref, v_ref, qm_ref, km_ref, vm_ref, om_ref, gtm_ref, gfm_ref,
                 xe_ref, zc_ref):
    i = pl.program_id(1)
    xe_ref[0:HALO, :] = prev_ref[0, 0]
    xe_ref[HALO:HALO + tt, :] = x_ref[0]
    xe_ref[HALO + tt:2 * HALO + tt, :] = next_ref[0, 0]
    xe = xe_ref[...]
    ms = jnp.mean(xe * xe, axis=-1, keepdims=True)
    xn_ext = (xe * lax.rsqrt(ms + EPS) * gmix_ref[...]).astype(BF16)
    xn = xn_ext[HALO:HALO + tt]

    def mm(a, lo, hi):
        return jnp.dot(a, wmain_ref[:, lo:hi], preferred_element_type=F32)

    zqk = mm(xn, 0, 2 * ATT_WIDTH)
    sq = (zqk * zqk).astype(BF16)
    r = lax.broadcasted_iota(jnp.int32, (2 * LANES, 2 * LANES), 0) // ATT_QK_DIM
    c = lax.broadcasted_iota(jnp.int32, (2 * LANES, 2 * LANES), 1) // ATT_QK_DIM
    seg_ones = (r == c).astype(BF16)
    lane = lax.broadcasted_iota(jnp.int32, (tt, LANES), 1)
    l64 = lane % ATT_QK_DIM
    row = lax.broadcasted_iota(jnp.int32, (tt, LANES), 0)
    pos = (pos_base + i * tt + row).astype(F32)
    ang = pos * inv_ref[...]
    cos_a = jnp.cos(ang)
    sin_a = jnp.sin(ang)
    half = ROT_DIM // 2
    cos_m = jnp.where(l64 < ROT_DIM, cos_a, 1.0)
    sin_lo = jnp.where(l64 < half, -sin_a, 0.0)
    sin_hi = jnp.where((l64 >= half) & (l64 < ROT_DIM), sin_a, 0.0)
    for cb in range(4):
        lo = cb * 2 * LANES
        ss = jnp.dot(sq[:, lo:lo + 2 * LANES], seg_ones, preferred_element_type=F32)
        rn = lax.rsqrt(ss * (1.0 / ATT_QK_DIM) + EPS)
        qn = zqk[:, lo:lo + 2 * LANES] * rn * gqk_ref[:, lo:lo + 2 * LANES]
        for sb in range(2):
            xs = qn[:, sb * LANES:(sb + 1) * LANES]
            rot = (xs * cos_m + pltpu.roll(xs, LANES - half, 1) * sin_lo
                   + pltpu.roll(xs, half, 1) * sin_hi)
            col = lo + sb * LANES
            if col < ATT_WIDTH:
                q_ref[0, :, col:col + LANES] = (rot * Q_SCALE).astype(BF16)
            else:
                k_ref[0, :, col - ATT_WIDTH:col - ATT_WIDTH + LANES] = rot.astype(BF16)

    va = mm(xn, 2 * ATT_WIDTH, 3 * ATT_WIDTH).astype(BF16)
    for h in range(ATT_HEADS):
        v_ref[0, :, 2 * h * LANES:(2 * h + 1) * LANES] = va[:, h * ATT_V_DIM:(h + 1) * ATT_V_DIM]
        v_ref[0, :, (2 * h + 1) * LANES:(2 * h + 2) * LANES] = jnp.ones((tt, LANES), BF16)

    base = 3 * ATT_WIDTH
    zc_ref[...] = mm(xn_ext, base, base + 2 * ML_WIDTH)
    y = (convw_ref[0:1, :] * zc_ref[pl.ds(HALO - 1, tt), :]
         + convw_ref[1:2, :] * zc_ref[pl.ds(HALO, tt), :]
         + convw_ref[2:3, :] * zc_ref[pl.ds(HALO + 1, tt), :]
         + convb_ref[...])
    act = y * (1.0 / (1.0 + jnp.exp(-y)))
    qm_ref[0] = act[:, :ML_WIDTH].astype(BF16)
    km_ref[0] = (act[:, ML_WIDTH:] * (ML_DIM ** -0.5)).astype(BF16)
    vm_ref[0] = mm(xn, base + 2 * ML_WIDTH, base + 3 * ML_WIDTH).astype(BF16)
    om_ref[0] = mm(xn, base + 3 * ML_WIDTH, base + 4 * ML_WIDTH).astype(BF16)

    g_tm = jnp.dot(xn, wg_ref[...], preferred_element_type=F32) + bg_ref[...]
    g_fm = lax.dot_general(wgt_ref[...], xn, (((1,), (1,)), ((), ())),
                           preferred_element_type=F32) + bgt_ref[...]
    col_tm = lax.broadcasted_iota(jnp.int32, (tt, N_GATES), 1)
    row_fm = lax.broadcasted_iota(jnp.int32, (N_GATES, tt), 0)
    gtm_ref[0] = jnp.where((col_tm // ML_HEADS) % 2 == 1, _log_sigmoid(g_tm), g_tm)
    gfm_ref[0] = jnp.where((row_fm // ML_HEADS) % 2 == 1, _log_sigmoid(g_fm), g_fm)


def _proj(x, prev, nxt, pos_base, tt, w):
    b, s, d = x.shape
    nt = s // tt
    tok = lambda width, dt: jax.ShapeDtypeStruct((b, s, width), dt)
    out_shape = ([tok(ATT_WIDTH, BF16)] * 2 + [tok(2 * ATT_WIDTH, BF16)] + [tok(ML_WIDTH, BF16)] * 4
                 + [tok(N_GATES, F32), jax.ShapeDtypeStruct((b, N_GATES, s), F32)])
    tspec = lambda width: pl.BlockSpec((1, tt, width), lambda bi, ti: (bi, ti, 0))
    hspec = pl.BlockSpec((1, 1, HALO, d), lambda bi, ti: (bi, ti, 0, 0))
    return pl.pallas_call(
        functools.partial(_proj_kernel, pos_base, tt),
        grid=(b, nt),
        in_specs=[tspec(d), hspec, hspec, _const_spec((1, d)), _const_spec((d, N_MAIN)),
                  _const_spec((d, N_GATES)), _const_spec((N_GATES, d)), _const_spec((1, N_GATES)),
                  _const_spec((N_GATES, 1)), _const_spec((1, 2 * ATT_WIDTH)), _const_spec((1, LANES)),
                  _const_spec((3, 2 * ML_WIDTH)), _const_spec((1, 2 * ML_WIDTH))],
        out_specs=[tspec(ATT_WIDTH)] * 2 + [tspec(2 * ATT_WIDTH)] + [tspec(ML_WIDTH)] * 4
        + [tspec(N_GATES), pl.BlockSpec((1, N_GATES, tt), lambda bi, ti: (bi, 0, ti))],
        out_shape=out_shape,
        scratch_shapes=[pltpu.VMEM((tt + 2 * HALO, d), F32), pltpu.VMEM((tt + 2 * HALO, 2 * ML_WIDTH), F32)],
        compiler_params=_cparams(("parallel", "parallel")),
        name=f"proj_t{tt}",
    )(x, prev, nxt, w["g_mix"], w["w_main"], w["w_g"], w["w_gt"], w["b_g"], w["b_gt"],
      w["g_qk"], w["inv_lane"], w["conv_w"], w["conv_b"])


def _attn_kernel(tq, tk, nk,
                 q_ref, k_ref, v_ref, kt_ref, vt_ref, lam_ref, gsub_ref, o_ref,
                 qq_ref, m_ref, acc_ref, sa_ref, sb_ref):
    q = q_ref[0]
    lane = lax.broadcasted_iota(jnp.int32, (tq, LANES), 1)
    zero = jnp.zeros_like(q)
    qq_ref[0:tq, :] = jnp.where(lane < ATT_QK_DIM, q, zero)
    qq_ref[tq:2 * tq, :] = jnp.where(lane >= ATT_QK_DIM, q, zero)
    m_ref[...] = jnp.full(m_ref.shape, NEG_INF, F32)
    acc_ref[...] = jnp.zeros(acc_ref.shape, F32)

    def scores(kb):
        return lax.dot_general(qq_ref[...], kb, (((1,), (1,)), ((), ())), preferred_element_type=F32)

    def update(s, vb):
        m_prev = m_ref[...]
        m_new = jnp.maximum(m_prev, jnp.max(s, axis=1, keepdims=True))
        alpha = jnp.exp2(m_prev - m_new)
        p = jnp.exp2(s - m_new[:, 0:1]).astype(BF16)
        pv = jnp.dot(p, vb, preferred_element_type=F32)
        acc_ref[:, 0:LANES] = alpha * acc_ref[:, 0:LANES] + pv[:, 0:LANES]
        acc_ref[:, LANES:2 * LANES] = alpha * acc_ref[:, LANES:2 * LANES] + pv[:, LANES:2 * LANES]
        m_ref[...] = m_new

    def kblock(j):
        return k_ref[0, pl.ds(pl.multiple_of(j * tk, tk), tk), :]

    def vblock(j):
        return v_ref[0, pl.ds(pl.multiple_of(j * tk, tk), tk), :]

    sa_ref[...] = scores(kblock(0))

    def body(i, carry):
        j = 2 * i
        sb_ref[...] = scores(kblock(j + 1))
        update(sa_ref[...], vblock(j))
        sa_ref[...] = scores(kblock(j + 2))
        update(sb_ref[...], vblock(j + 1))
        return carry

    lax.fori_loop(0, nk // 2 - 1, body, 0)
    sb_ref[...] = scores(kblock(nk - 1))
    update(sa_ref[...], vblock(nk - 2))
    col = lax.broadcasted_iota(jnp.int32, (2 * tq, LANES), 1)
    s_meta = jnp.where(col < N_META, scores(kt_ref[0]), NEG_INF)
    update(sb_ref[...], vblock(nk - 1))
    update(s_meta, vt_ref[0])

    o1 = acc_ref[0:tq, 0:LANES] / acc_ref[0:tq, LANES:2 * LANES]
    o2 = acc_ref[tq:2 * tq, 0:LANES] / acc_ref[tq:2 * tq, LANES:2 * LANES]
    lam = (jnp.exp(jnp.sum(lam_ref[0:1, :] * lam_ref[1:2, :], axis=1, keepdims=True))
           - jnp.exp(jnp.sum(lam_ref[2:3, :] * lam_ref[3:4, :], axis=1, keepdims=True)) + LAM_INIT)
    o = o1 - lam * o2
    ms = jnp.mean(o * o, axis=-1, keepdims=True)
    o_ref[0] = (o * lax.rsqrt(ms + EPS) * gsub_ref[...] * (1.0 - LAM_INIT)).astype(BF16)


def _att_key_block(s):
    resident = s * 3 * LANES * 2 * 2
    for tk in (2 * ATT_TK, ATT_TK):
        scores = 2 * (2 * ATT_TQ) * tk * 4
        if s % (2 * tk) == 0 and resident + scores <= VMEM_LIMIT * 3 // 5:
            return tk
    raise ValueError(f"sequence length {s} is not a multiple of {2 * ATT_TK}")


def _attn(q, k, v, kt, vt, lam4, g_sub):
    b, s, _ = q.shape
    tq, tk = ATT_TQ, _att_key_block(s)
    kspec = lambda rows: pl.BlockSpec((1, rows, LANES), lambda bi, hi, qi: (bi, 0, hi))
    vspec = lambda rows: pl.BlockSpec((1, rows, 2 * LANES), lambda bi, hi, qi: (bi, 0, hi))
    return pl.pallas_call(
        functools.partial(_attn_kernel, tq, tk, s // tk),
        grid=(b, ATT_HEADS, s // tq),
        in_specs=[pl.BlockSpec((1, tq, LANES), lambda bi, hi, qi: (bi, qi, hi)),
                  kspec(s), vspec(s), kspec(LANES), vspec(LANES),
                  _const_spec((4, ATT_QK_DIM)), _const_spec((1, ATT_V_DIM))],
        out_specs=pl.BlockSpec((1, tq, LANES), lambda bi, hi, qi: (bi, qi, hi)),
        out_shape=jax.ShapeDtypeStruct((b, s, ATT_WIDTH), BF16),
        scratch_shapes=[pltpu.VMEM((2 * tq, LANES), BF16), pltpu.VMEM((2 * tq, LANES), F32),
                        pltpu.VMEM((2 * tq, 2 * LANES), F32),
                        pltpu.VMEM((2 * tq, tk), F32), pltpu.VMEM((2 * tq, tk), F32)],
        compiler_params=_cparams(("parallel", "parallel", "parallel")),
        name="diff_attn",
    )(q, k, v, kt, vt, lam4, g_sub)


def _mlstm_chunks(dirs):
    r = lax.broadcasted_iota(jnp.int32, (CHUNK, CHUNK), 0)
    c = lax.broadcasted_iota(jnp.int32, (CHUNK, CHUNK), 1)
    ones_col = (c == 0).astype(BF16)
    chains = []
    for reverse, q, k, v, g_tm, g_fm, s_ref, m_ref, h_ref in dirs:
        causal = (c >= r) if reverse else (c <= r)
        last = 0 if reverse else CHUNK - 1
        gbase = 2 * ML_HEADS if reverse else 0
        b_cols = jnp.dot(causal.astype(F32), g_tm, preferred_element_type=F32,
                         precision=lax.Precision.HIGHEST)
        b_rows = jnp.dot(g_fm, ((r >= c) if reverse else (r <= c)).astype(F32),
                         preferred_element_type=F32, precision=lax.Precision.HIGHEST)
        for h in range(ML_HEADS):
            fc = gbase + ML_HEADS + h
            sl = slice(h * ML_DIM, (h + 1) * ML_DIM)
            chains.append(dict(
                causal=causal, last=last, h=h, sl=sl, s_ref=s_ref, m_ref=m_ref, h_ref=h_ref,
                q=q[:, sl], k=k[:, sl], v=v[:, sl],
                b_col=b_cols[:, fc:fc + 1], b_row=b_rows[fc:fc + 1, :],
                li_row=g_fm[gbase + h:gbase + h + 1, :], li_col=g_tm[:, gbase + h:gbase + h + 1]))
    n = range(len(chains))
    ch = chains
    m_prev = [ch[i]["m_ref"][ch[i]["h"]:ch[i]["h"] + 1, 0:1] for i in n]
    log_d = [jnp.where(ch[i]["causal"], ch[i]["b_col"] - ch[i]["b_row"] + ch[i]["li_row"], NEG_INF) for i in n]
    m_inter = [ch[i]["b_col"] + m_prev[i] for i in n]
    mj = [jnp.maximum(m_inter[i], jnp.max(log_d[i], axis=1, keepdims=True)) for i in n]
    vext = [jnp.concatenate([ch[i]["v"], ones_col], axis=1) for i in n]
    qk = [lax.dot_general(ch[i]["q"], ch[i]["k"], (((1,), (1,)), ((), ())), preferred_element_type=F32)
          for i in n]
    state = [ch[i]["s_ref"][ch[i]["h"]] for i in n]
    inter = [jnp.dot(ch[i]["q"], state[i].astype(BF16), preferred_element_type=F32) for i in n]
    b_last = [ch[i]["b_col"][ch[i]["last"]:ch[i]["last"] + 1, :] for i in n]
    m_new = [jnp.maximum(b_last[i] + m_prev[i],
                         jnp.max(b_last[i] - ch[i]["b_row"] + ch[i]["li_row"], axis=1, keepdims=True)) for i in n]
    w = [(qk[i] * jnp.exp(log_d[i] - mj[i])).astype(BF16) for i in n]
    kw = [(ch[i]["k"].astype(F32)
           * jnp.exp(b_last[i] - ch[i]["b_col"] + ch[i]["li_col"] - m_new[i])).astype(BF16) for i in n]
    tot = [jnp.exp(m_inter[i] - mj[i]) * inter[i] + jnp.dot(w[i], vext[i], preferred_element_type=F32)
           for i in n]
    upd = [lax.dot_general(kw[i], vext[i], (((0,), (0,)), ((), ())), preferred_element_type=F32) for i in n]
    for i in n:
        den = tot[i][:, ML_DIM:ML_DIM + 1]
        ch[i]["h_ref"][0, :, ch[i]["sl"]] = tot[i][:, :ML_DIM] / jnp.maximum(jnp.abs(den), jnp.exp(-mj[i]))
    for i in n:
        h = ch[i]["h"]
        ch[i]["s_ref"][h] = jnp.exp(b_last[i] + m_prev[i] - m_new[i]) * state[i] + upd[i]
        ch[i]["m_ref"][h:h + 1, :] = jnp.broadcast_to(m_new[i], (1, LANES))


def _mlstm_kernel(qf_ref, kf_ref, vf_ref, gtf_ref, gff_ref, qb_ref, kb_ref, vb_ref, gtb_ref, gfb_ref,
                  s0_ref, m0_ref, hf_ref, hb_ref, sfin_ref, mfin_ref,
                  sf_ref, sb_ref, mf_ref, mb_ref):
    j = pl.program_id(1)

    @pl.when(j == 0)
    def _():
        sf_ref[...] = s0_ref[0]
        mf_ref[...] = m0_ref[0]
        sb_ref[...] = jnp.zeros(sb_ref.shape, F32)
        mb_ref[...] = jnp.zeros(mb_ref.shape, F32)

    _mlstm_chunks([(False, qf_ref[0], kf_ref[0], vf_ref[0], gtf_ref[0], gff_ref[0], sf_ref, mf_ref, hf_ref)])
    _mlstm_chunks([(True, qb_ref[0], kb_ref[0], vb_ref[0], gtb_ref[0], gfb_ref[0], sb_ref, mb_ref, hb_ref)])

    @pl.when(j == pl.num_programs(1) - 1)
    def _():
        sfin_ref[0] = sf_ref[...]
        mfin_ref[0] = mf_ref[...]


def _mlstm(qm, km, vm, g_tm, g_fm, s0, m0):
    b, s, _ = qm.shape
    nc = s // CHUNK
    fwd = lambda width: pl.BlockSpec((1, CHUNK, width), lambda bi, ci: (bi, ci, 0))
    bwd = lambda width: pl.BlockSpec((1, CHUNK, width), lambda bi, ci: (bi, nc - 1 - ci, 0))
    gf_fwd = pl.BlockSpec((1, N_GATES, CHUNK), lambda bi, ci: (bi, 0, ci))
    gf_bwd = pl.BlockSpec((1, N_GATES, CHUNK), lambda bi, ci: (bi, 0, nc - 1 - ci))
    st = pl.BlockSpec((1, ML_HEADS, ML_DIM, 2 * LANES), lambda bi, ci: (bi, 0, 0, 0))
    mx = pl.BlockSpec((1, SUBLANES, LANES), lambda bi, ci: (bi, 0, 0))
    return pl.pallas_call(
        _mlstm_kernel,
        grid=(b, nc),
        in_specs=[fwd(ML_WIDTH)] * 3 + [fwd(N_GATES), gf_fwd] + [bwd(ML_WIDTH)] * 3 + [bwd(N_GATES), gf_bwd]
        + [st, mx],
        out_specs=[fwd(ML_WIDTH), bwd(ML_WIDTH), st, mx],
        out_shape=[jax.ShapeDtypeStruct((b, s, ML_WIDTH), F32)] * 2
        + [jax.ShapeDtypeStruct((b, ML_HEADS, ML_DIM, 2 * LANES), F32),
           jax.ShapeDtypeStruct((b, SUBLANES, LANES), F32)],
        scratch_shapes=[pltpu.VMEM((ML_HEADS, ML_DIM, 2 * LANES), F32)] * 2
        + [pltpu.VMEM((SUBLANES, LANES), F32)] * 2,
        compiler_params=_cparams(("parallel", "arbitrary")),
        name="mlstm",
    )(qm, km, vm, g_tm, g_fm, qm, km, vm, g_tm, g_fm, s0, m0)


def _sorting_network(n):
    pairs = []
    p = 1
    while p < n:
        k = p
        while k >= 1:
            for j in range(k % p, n - k, 2 * k):
                for i in range(min(k, n - j - k)):
                    if (i + j) // (2 * p) == (i + j + k) // (2 * p):
                        pairs.append((i + j, i + j + k))
            k //= 2
        p *= 2
    return pairs


def _top_values_sorted(xs, n):
    sorted_groups = []
    for x in xs:
        groups = [x[SUBLANES * g:SUBLANES * (g + 1)] for g in range(x.shape[0] // SUBLANES)]
        size = 1
        while size < len(groups):
            size *= 2
        groups += [jnp.full_like(groups[0], NEG_INF)] * (size - len(groups))
        for i, j in _sorting_network(size):
            groups[i], groups[j] = jnp.maximum(groups[i], groups[j]), jnp.minimum(groups[i], groups[j])
        sorted_groups.append(groups)
    vals = [[] for _ in xs]
    for r in range(n):
        for groups, out in zip(sorted_groups, vals):
            mx = jnp.max(groups[0], axis=0, keepdims=True)
            out.append(mx)
            if r + 1 < n:
                hit = groups[0] == mx
                live = min(len(groups), n - 1 - r)
                for d in range(live):
                    below = groups[d + 1] if d + 1 < len(groups) else NEG_INF
                    groups[d] = jnp.where(hit, below, groups[d])
    return vals


def _count_greater(vals, x):
    n = len(vals)

    def pivot(decided, lo, size):
        if not decided:
            return vals[lo + size // 2 - 1]
        half = size // 2
        return jnp.where(decided[0], pivot(decided[1:], lo + half, half), pivot(decided[1:], lo, half))

    decided = []
    count = jnp.zeros_like(x)
    size = n
    while size >= 2:
        above = pivot(decided, 0, n) > x
        decided.append(above)
        count = count + jnp.where(above, float(size // 2), 0.0)
        size //= 2
    return jnp.where(vals[n - 1] > x, float(n), count)


def _bf16_pair_bits(x):
    u = lax.bitcast_convert_type(x.astype(BF16).astype(F32), jnp.uint32)
    return u | (u >> 16)


_CAND_PAIRS = [(a, b) for a in range(PEER_TOPK + 1) for b in range(PEER_TOPK + 1)
               if (a + 1) * (b + 1) <= PEER_TOPK + 1]
_CAND_ROWS = -(-len(_CAND_PAIRS) // SUBLANES) * SUBLANES


def _mix_kernel(tt, x_ref, att_ref, hf_ref, hb_ref, om_ref, gmh_ref, wout_ref, gffn_ref, wpqt_ref,
                k1_ref, k2_ref, h1_ref, xn2_ref, r2_ref, e2_ref, n1_ref, e1_ref, mix_ref, cand_ref):
    mix_ref[:, 0:ATT_WIDTH] = att_ref[0]
    for h in range(ML_HEADS):
        sl = slice(h * ML_DIM, (h + 1) * ML_DIM)
        hs = hf_ref[0, :, sl] + hb_ref[0, :, sl]
        ms = jnp.mean(hs * hs, axis=-1, keepdims=True)
        hn = hs * lax.rsqrt(ms + EPS) * gmh_ref[...]
        gate = 1.0 / (1.0 + jnp.exp(-om_ref[0, :, sl].astype(F32)))
        mix_ref[:, ATT_WIDTH + h * ML_DIM:ATT_WIDTH + (h + 1) * ML_DIM] = (hn * gate).astype(BF16)
    h1 = x_ref[0] + jnp.dot(mix_ref[...], wout_ref[...], preferred_element_type=F32)
    h1_ref[0] = h1
    ms = jnp.mean(h1 * h1, axis=-1, keepdims=True)
    xn2 = (h1 * lax.rsqrt(ms + EPS) * gffn_ref[...]).astype(BF16)
    xn2_ref[0] = xn2
    qt = lax.dot_general(wpqt_ref[...], xn2, (((1,), (1,)), ((), ())), preferred_element_type=F32)
    half = PEER_QDIM // 2
    cand_ref[...] = jnp.full(cand_ref.shape, NEG_INF, F32)
    for h in range(PEER_HEADS):
        q1 = qt[h * PEER_QDIM:h * PEER_QDIM + half].astype(BF16)
        q2 = qt[h * PEER_QDIM + half:(h + 1) * PEER_QDIM].astype(BF16)
        s1 = jnp.dot(k1_ref[h], q1, preferred_element_type=F32)
        s2 = jnp.dot(k2_ref[h], q2, preferred_element_type=F32)
        for sl in range(tt // ROUTE_SLAB):
            lanes = slice(sl * ROUTE_SLAB, (sl + 1) * ROUTE_SLAB)
            a1 = s1[:, lanes]
            a2 = s2[:, lanes]
            v1, v2 = _top_values_sorted([a1, a2], PEER_TOPK + 1)
            rank2 = _count_greater(v2[:PEER_TOPK], a2)
            for idx, (a, b) in enumerate(_CAND_PAIRS):
                cand_ref[idx:idx + 1, lanes] = v1[a] + v2[b]
            cs, = _top_values_sorted([cand_ref[:, lanes]], PEER_TOPK + 1)
            z = jnp.zeros_like(cs[0])
            for i in range(PEER_TOPK):
                z = z + jnp.exp(cs[i] - cs[0])
            thr = 0.5 * (cs[PEER_TOPK - 1] + cs[PEER_TOPK])
            n1 = _count_greater(v2[:PEER_TOPK // 2], thr - a1)
            n_best = jnp.zeros_like(thr)
            for b in range(PEER_TOPK // 2, PEER_TOPK):
                n_best = n_best + jnp.where(v2[b] > thr - v1[0], 1.0, 0.0)
            n1 = n1 + jnp.where(a1 == v1[0], n_best, 0.0)
            r2_ref[0, h, :, lanes] = rank2.astype(BF16)
            n1_ref[0, h, :, lanes] = _bf16_pair_bits(n1)
            e1_ref[0, h, :, lanes] = _bf16_pair_bits(jnp.exp(a1 - v1[0]) * (0.5 / z))
            e2_ref[0, h, :, lanes] = jnp.exp(a2 - v2[0]).astype(BF16)


def _mix(x, att, hf, hb, om, w):
    b, s, d = x.shape
    tt = MIX_TILE
    nt = s // tt
    tspec = lambda width: pl.BlockSpec((1, tt, width), lambda bi, ti: (bi, ti, 0))
    sspec = pl.BlockSpec((1, PEER_HEADS, PEER_NKEYS, tt), lambda bi, ti: (bi, 0, 0, ti))
    return pl.pallas_call(
        functools.partial(_mix_kernel, tt),
        grid=(b, nt),
        in_specs=[tspec(d), tspec(ATT_WIDTH), tspec(ML_WIDTH), tspec(ML_WIDTH), tspec(ML_WIDTH),
                  _const_spec((1, ML_DIM)), _const_spec((d, d)), _const_spec((1, d)),
                  _const_spec((PEER_HEADS * PEER_QDIM, d)),
                  _const_spec((PEER_HEADS, PEER_NKEYS, PEER_QDIM // 2)),
                  _const_spec((PEER_HEADS, PEER_NKEYS, PEER_QDIM // 2))],
        out_specs=[tspec(d), tspec(d), sspec, sspec, sspec, sspec],
        out_shape=[jax.ShapeDtypeStruct((b, s, d), F32), jax.ShapeDtypeStruct((b, s, d), BF16),
                   jax.ShapeDtypeStruct((b, PEER_HEADS, PEER_NKEYS, s), BF16),
                   jax.ShapeDtypeStruct((b, PEER_HEADS, PEER_NKEYS, s), BF16),
                   jax.ShapeDtypeStruct((b, PEER_HEADS, PEER_NKEYS, s), jnp.uint32),
                   jax.ShapeDtypeStruct((b, PEER_HEADS, PEER_NKEYS, s), jnp.uint32)],
        scratch_shapes=[pltpu.VMEM((tt, d), BF16), pltpu.VMEM((_CAND_ROWS, tt), F32)],
        compiler_params=_cparams(("parallel", "parallel")),
        name="mix_route",
    )(x, att, hf, hb, om, w["g_mh"], w["w_out"], w["g_ffn"], w["w_pqt"], w["sub_k1"], w["sub_k2"])


def _gelu_x2(x):
    return x + x * lax.erf(x * (2.0 ** -0.5))


def _peer_kernel(tt, ng, xn2_ref, h1_ref, r2_ref, e2_ref, n1_ref, e1_ref, u_ref, vt_ref, y_ref,
                 a0_ref, a1_ref, w_ref, acc_ref):
    s = pl.program_id(1)
    rows = PEER_GROUP // PEER_NKEYS

    @pl.when(s == 0)
    def _():
        a0_ref[...] = jnp.zeros(a0_ref.shape, F32)
        a1_ref[...] = jnp.zeros(a1_ref.shape, F32)

    @pl.when((s < 1) | ((s + ng - 1) % ng == 0))
    def _():
        acc_ref[...] = jnp.zeros(acc_ref.shape, F32)

    g2 = (s + ng - 1) % ng

    def row(ref, h, c, lanes):
        words = jnp.broadcast_to(ref[0, h, pl.ds(c, 1), lanes], (SUBLANES, PEER_PASS_LANES))
        tile = pltpu.bitcast(words, BF16)
        return jnp.concatenate([tile] * (PEER_NKEYS // (2 * SUBLANES)), axis=0)

    def stages(a_cur, a_oth):
        zero = jnp.zeros((PEER_NKEYS, PEER_PASS_LANES), BF16)
        tok_half = tt // 2
        for kb in range(PEER_GROUP // PEER_VBLOCK):
            for cc in range(kb * PEER_VBLOCK // PEER_NKEYS, (kb + 1) * PEER_VBLOCK // PEER_NKEYS):
                c = g2 * rows + cc
                ex = slice(cc * PEER_NKEYS, (cc + 1) * PEER_NKEYS)
                for lb in range(tt // PEER_PASS_LANES):
                    lanes = pl.ds(lb * PEER_PASS_LANES, PEER_PASS_LANES)
                    wsum = zero
                    for h in range(PEER_HEADS):
                        sel = r2_ref[0, h, :, lanes] < row(n1_ref, h, c, lanes)
                        wsum = wsum + jnp.where(sel, e2_ref[0, h, :, lanes], zero) * row(e1_ref, h, c, lanes)
                    w_ref[ex, lanes] = wsum * _gelu_x2(a_oth[ex, lanes]).astype(BF16)
            if kb < 2:
                tk = slice(kb * tok_half, (kb + 1) * tok_half)
                a_cur[:, tk] = lax.dot_general(u_ref[...], xn2_ref[0, tk, :], (((1,), (1,)), ((), ())),
                                               preferred_element_type=F32)
            blk = slice(kb * PEER_VBLOCK, (kb + 1) * PEER_VBLOCK)
            acc_ref[...] += jnp.dot(vt_ref[0, :, blk], w_ref[blk, :], preferred_element_type=F32)

    @pl.when(s % 2 == 0)
    def _():
        stages(a0_ref, a1_ref)

    @pl.when(s % 2 == 1)
    def _():
        stages(a1_ref, a0_ref)

    @pl.when((s >= 1) & ((s + ng - 1) % ng == ng - 1))
    def _():
        y_ref[0] = h1_ref[0] + acc_ref[...].T


def _peer(xn2, h1, r2, e2, n1, e1, w):
    b, s, d = h1.shape
    tt = PEER_TILE
    nt = s // tt
    ng = (PEER_NKEYS * PEER_NKEYS) // PEER_GROUP
    tile = lambda si, lag: jnp.clip((si - lag) // ng, 0, nt - 1)
    tspec = lambda lag: pl.BlockSpec((1, tt, d), lambda bi, si: (bi, tile(si, lag), 0))
    sspec = pl.BlockSpec((1, PEER_HEADS, PEER_NKEYS, tt), lambda bi, si: (bi, 0, 0, tile(si, 1)))
    return pl.pallas_call(
        functools.partial(_peer_kernel, tt, ng),
        grid=(b, nt * ng + 1),
        in_specs=[tspec(0), tspec(1), sspec, sspec, sspec, sspec,
                  pl.BlockSpec((PEER_GROUP, d), lambda bi, si: (si % ng, 0)),
                  pl.BlockSpec((1, d, PEER_GROUP), lambda bi, si: ((si + ng - 1) % ng, 0, 0))],
        out_specs=tspec(1),
        out_shape=jax.ShapeDtypeStruct((b, s, d), F32),
        scratch_shapes=[pltpu.VMEM((PEER_GROUP, tt), F32)] * 2 + [pltpu.VMEM((PEER_GROUP, tt), BF16)]
        + [pltpu.VMEM((d, tt), F32)],
        compiler_params=_cparams(("parallel", "arbitrary")),
        name="peer_dense",
    )(xn2, h1, r2, e2, n1, e1, w["peer_u"], w["peer_vt"])


def _prep_weights(g_mix, w_in, b_gates, conv_w, conv_b, g_qn, g_kn, lam_q1, lam_k1, lam_q2, lam_k2,
                  g_sub, g_mh, w_out, g_ffn, w_pq, sub_k1, sub_k2, peer_u, peer_v):
    half = ROT_DIM // 2
    inv = ROPE_THETA ** (-jnp.arange(half, dtype=F32) * 2.0 / ROT_DIM)
    w_g = w_in[0][:, N_MAIN:]
    return {
        "g_mix": g_mix[0][None, :],
        "w_main": w_in[0][:, :N_MAIN].astype(BF16),
        "w_g": w_g.astype(BF16),
        "w_gt": w_g.T.astype(BF16),
        "b_g": b_gates[0][None, :],
        "b_gt": b_gates[0][:, None],
        "g_qk": jnp.concatenate([jnp.tile(g_qn[0], ATT_WIDTH // ATT_QK_DIM),
                                 jnp.tile(g_kn[0], ATT_WIDTH // ATT_QK_DIM)])[None, :],
        "inv_lane": jnp.tile(inv, LANES // half)[None, :],
        "conv_w": conv_w[0][:, 0, :],
        "conv_b": conv_b[0][None, :],
        "lam4": jnp.stack([lam_q1[0], lam_k1[0], lam_q2[0], lam_k2[0]]),
        "g_sub": g_sub[0][None, :],
        "g_mh": g_mh[0][None, :],
        "w_out": w_out[0].astype(BF16),
        "g_ffn": g_ffn[0][None, :],
        "w_pqt": w_pq[0].T.astype(BF16),
        "sub_k1": sub_k1[0].astype(BF16),
        "sub_k2": sub_k2[0].astype(BF16),
        "peer_u": peer_u[0].astype(BF16),
        "peer_vt": peer_v[0].astype(BF16).reshape(-1, PEER_GROUP, D_MODEL).transpose(0, 2, 1),
    }


def _trunk(x, meta, w):
    b, s, d = x.shape
    tt = PROJ_TILE
    assert s % tt == 0 and s % (2 * ATT_TK) == 0 and s % MIX_TILE == 0 and s % PEER_TILE == 0
    nt = s // tt
    xr = x.reshape(b, nt, tt, d)
    meta_tail = jnp.broadcast_to(meta[N_META - HALO:][None, None], (b, 1, HALO, d))
    prev = jnp.concatenate([meta_tail, xr[:, :-1, tt - HALO:, :]], axis=1)
    nxt = jnp.concatenate([xr[:, 1:, :HALO, :], jnp.zeros((b, 1, HALO, d), x.dtype)], axis=1)
    q, k, v, qm, km, vm, om, g_tm, g_fm = _proj(x, prev, nxt, N_META, tt, w)

    meta_b = jnp.broadcast_to(meta[None], (b, N_META, d))
    m_prev = jnp.zeros((b, 1, HALO, d), x.dtype)
    m_next = x[:, None, :HALO, :]
    _, k_m, v_m, qm_m, km_m, vm_m, _, gtm_m, gfm_m = _proj(meta_b, m_prev, m_next, 0, N_META, w)

    pad = CHUNK - N_META
    att = _attn(q, k, v, jnp.pad(k_m, ((0, 0), (0, pad), (0, 0))), jnp.pad(v_m, ((0, 0), (0, pad), (0, 0))),
                w["lam4"], w["g_sub"])

    front = lambda a: jnp.pad(a, ((0, 0), (pad, 0), (0, 0)))
    is_input_gate = (jnp.arange(N_GATES) // ML_HEADS) % 2 == 0
    pad_gate = jnp.where(is_input_gate, NEG_INF, 0.0).astype(F32)
    gtm_p = jnp.concatenate([jnp.broadcast_to(pad_gate[None, None, :], (b, pad, N_GATES)), gtm_m], axis=1)
    gfm_p = jnp.concatenate([jnp.broadcast_to(pad_gate[None, :, None], (b, N_GATES, pad)), gfm_m], axis=2)
    s_zero = jnp.zeros((b, ML_HEADS, ML_DIM, 2 * LANES), F32)
    m_zero = jnp.zeros((b, SUBLANES, LANES), F32)
    _, _, s0, m0 = _mlstm(front(qm_m), front(km_m), front(vm_m), gtm_p, gfm_p, s_zero, m_zero)
    hf, hb, _, _ = _mlstm(qm, km, vm, g_tm, g_fm, s0, m0)

    h1, xn2, r2, e2, n1, e1 = _mix(x, att, hf, hb, om, w)
    return _peer(xn2, h1, r2, e2, n1, e1, w)


def kernel(x_prompt, x_sample, meta, g_mix, w_in, b_gates, conv_w, conv_b, g_qn, g_kn, lam_q1, lam_k1,
           lam_q2, lam_k2, g_sub, g_mh, w_out, g_ffn, w_pq, sub_k1, sub_k2, peer_u, peer_v):
    w = _prep_weights(g_mix, w_in, b_gates, conv_w, conv_b, g_qn, g_kn, lam_q1, lam_k1, lam_q2, lam_k2,
                      g_sub, g_mh, w_out, g_ffn, w_pq, sub_k1, sub_k2, peer_u, peer_v)
    return (_trunk(x_prompt, meta, w), _trunk(x_sample, meta, w))
```

```python
import functools
import math

import jax
import jax.numpy as jnp
from jax import lax
from jax.experimental import pallas as pl
from jax.experimental.pallas import tpu as pltpu

F32 = jnp.float32
BF16 = jnp.bfloat16

D_MODEL = 1024
N_META = 16
ATT_HEADS = 4
ATT_QK_DIM = 64
ATT_V_DIM = 128
ATT_WIDTH = ATT_HEADS * ATT_V_DIM
ML_HEADS = 4
ML_DIM = 128
ML_WIDTH = ML_HEADS * ML_DIM
N_GATES = 4 * ML_HEADS
N_MAIN = 3 * ATT_WIDTH + 4 * ML_WIDTH
ROT_DIM = ATT_QK_DIM // 4
ROPE_THETA = 500000.0
CHUNK = 128
PEER_HEADS = 8
PEER_NKEYS = 128
PEER_QDIM = 256
PEER_TOPK = 16
EPS = 1e-6
LAM_INIT = 0.8 - 0.6 * math.exp(-0.3 * 0)
Q_SCALE = ATT_QK_DIM ** -0.5 * math.log2(math.e)

LANES = 128
SUBLANES = 8
HALO = SUBLANES
VMEM_LIMIT = 56 * 1024 * 1024

PROJ_TILE = 512
ATT_TQ = 512
ATT_TK = 1024
MIX_TILE = 512
ROUTE_SLAB = 128
PEER_TILE = 512
PEER_GROUP = 2048
PEER_PASS_LANES = 256
PEER_VBLOCK = 1024
NEG_INF = float("-inf")


def _cparams(sem):
    return pltpu.CompilerParams(dimension_semantics=sem, vmem_limit_bytes=VMEM_LIMIT)


def _const_spec(shape):
    nd = len(shape)
    return pl.BlockSpec(shape, lambda *_: (0,) * nd)


def _log_sigmoid(x):
    return -(jnp.maximum(-x, 0.0) + jnp.log1p(jnp.exp(-jnp.abs(x))))


def _proj_kernel(pos_base, tt,
                 x_ref, prev_ref, next_ref, gmix_ref, wmain_ref, wg_ref, wgt_ref, bg_ref, bgt_ref,
                 gqk_ref, inv_ref, convw_ref, convb_ref,
                 q_ref, k_ref, v_ref, qm_ref, km_ref, vm_ref, om_ref, gtm_ref, gfm_ref,
                 xe_ref, zc_ref):
    i = pl.program_id(1)
    xe_ref[0:HALO, :] = prev_ref[0, 0]
    xe_ref[HALO:HALO + tt, :] = x_ref[0]
    xe_ref[HALO + tt:2 * HALO + tt, :] = next_ref[0, 0]
    xe = xe_ref[...]
    ms = jnp.mean(xe * xe, axis=-1, keepdims=True)
    xn_ext = (xe * lax.rsqrt(ms + EPS) * gmix_ref[...]).astype(BF16)
    xn = xn_ext[HALO:HALO + tt]

    def mm(a, lo, hi):
        return jnp.dot(a, wmain_ref[:, lo:hi], preferred_element_type=F32)

    zqk = mm(xn, 0, 2 * ATT_WIDTH)
    sq = (zqk * zqk).astype(BF16)
    r = lax.broadcasted_iota(jnp.int32, (2 * LANES, 2 * LANES), 0) // ATT_QK_DIM
    c = lax.broadcasted_iota(jnp.int32, (2 * LANES, 2 * LANES), 1) // ATT_QK_DIM
    seg_ones = (r == c).astype(BF16)
    lane = lax.broadcasted_iota(jnp.int32, (tt, LANES), 1)
    l64 = lane % ATT_QK_DIM
    row = lax.broadcasted_iota(jnp.int32, (tt, LANES), 0)
    pos = (pos_base + i * tt + row).astype(F32)
    ang = pos * inv_ref[...]
    cos_a = jnp.cos(ang)
    sin_a = jnp.sin(ang)
    half = ROT_DIM // 2
    cos_m = jnp.where(l64 < ROT_DIM, cos_a, 1.0)
    sin_lo = jnp.where(l64 < half, -sin_a, 0.0)
    sin_hi = jnp.where((l64 >= half) & (l64 < ROT_DIM), sin_a, 0.0)
    for cb in range(4):
        lo = cb * 2 * LANES
        ss = jnp.dot(sq[:, lo:lo + 2 * LANES], seg_ones, preferred_element_type=F32)
        rn = lax.rsqrt(ss * (1.0 / ATT_QK_DIM) + EPS)
        qn = zqk[:, lo:lo + 2 * LANES] * rn * gqk_ref[:, lo:lo + 2 * LANES]
        for sb in range(2):
            xs = qn[:, sb * LANES:(sb + 1) * LANES]
            rot = (xs * cos_m + pltpu.roll(xs, LANES - half, 1) * sin_lo
                   + pltpu.roll(xs, half, 1) * sin_hi)
            col = lo + sb * LANES
            if col < ATT_WIDTH:
                q_ref[0, :, col:col + LANES] = (rot * Q_SCALE).astype(BF16)
            else:
                k_ref[0, :, col - ATT_WIDTH:col - ATT_WIDTH + LANES] = rot.astype(BF16)

    va = mm(xn, 2 * ATT_WIDTH, 3 * ATT_WIDTH).astype(BF16)
    for h in range(ATT_HEADS):
        v_ref[0, :, 2 * h * LANES:(2 * h + 1) * LANES] = va[:, h * ATT_V_DIM:(h + 1) * ATT_V_DIM]
        v_ref[0, :, (2 * h + 1) * LANES:(2 * h + 2) * LANES] = jnp.ones((tt, LANES), BF16)

    base = 3 * ATT_WIDTH
    zc_ref[...] = mm(xn_ext, base, base + 2 * ML_WIDTH)
    y = (convw_ref[0:1, :] * zc_ref[pl.ds(HALO - 1, tt), :]
         + convw_ref[1:2, :] * zc_ref[pl.ds(HALO, tt), :]
         + convw_ref[2:3, :] * zc_ref[pl.ds(HALO + 1, tt), :]
         + convb_ref[...])
    act = y * (1.0 / (1.0 + jnp.exp(-y)))
    qm_ref[0] = act[:, :ML_WIDTH].astype(BF16)
    km_ref[0] = (act[:, ML_WIDTH:] * (ML_DIM ** -0.5)).astype(BF16)
    vm_ref[0] = mm(xn, base + 2 * ML_WIDTH, base + 3 * ML_WIDTH).astype(BF16)
    om_ref[0] = mm(xn, base + 3 * ML_WIDTH, base + 4 * ML_WIDTH).astype(BF16)

    g_tm = jnp.dot(xn, wg_ref[...], preferred_element_type=F32) + bg_ref[...]
    g_fm = lax.dot_general(wgt_ref[...], xn, (((1,), (1,)), ((), ())),
                           preferred_element_type=F32) + bgt_ref[...]
    col_tm = lax.broadcasted_iota(jnp.int32, (tt, N_GATES), 1)
    row_fm = lax.broadcasted_iota(jnp.int32, (N_GATES, tt), 0)
    gtm_ref[0] = jnp.where((col_tm // ML_HEADS) % 2 == 1, _log_sigmoid(g_tm), g_tm)
    gfm_ref[0] = jnp.where((row_fm // ML_HEADS) % 2 == 1, _log_sigmoid(g_fm), g_fm)


def _proj(x, prev, nxt, pos_base, tt, w):
    b, s, d = x.shape
    nt = s // tt
    tok = lambda width, dt: jax.ShapeDtypeStruct((b, s, width), dt)
    out_shape = ([tok(ATT_WIDTH, BF16)] * 2 + [tok(2 * ATT_WIDTH, BF16)] + [tok(ML_WIDTH, BF16)] * 4
                 + [tok(N_GATES, F32), jax.ShapeDtypeStruct((b, N_GATES, s), F32)])
    tspec = lambda width: pl.BlockSpec((1, tt, width), lambda bi, ti: (bi, ti, 0))
    hspec = pl.BlockSpec((1, 1, HALO, d), lambda bi, ti: (bi, ti, 0, 0))
    return pl.pallas_call(
        functools.partial(_proj_kernel, pos_base, tt),
        grid=(b, nt),
        in_specs=[tspec(d), hspec, hspec, _const_spec((1, d)), _const_spec((d, N_MAIN)),
                  _const_spec((d, N_GATES)), _const_spec((N_GATES, d)), _const_spec((1, N_GATES)),
                  _const_spec((N_GATES, 1)), _const_spec((1, 2 * ATT_WIDTH)), _const_spec((1, LANES)),
                  _const_spec((3, 2 * ML_WIDTH)), _const_spec((1, 2 * ML_WIDTH))],
        out_specs=[tspec(ATT_WIDTH)] * 2 + [tspec(2 * ATT_WIDTH)] + [tspec(ML_WIDTH)] * 4
        + [tspec(N_GATES), pl.BlockSpec((1, N_GATES, tt), lambda bi, ti: (bi, 0, ti))],
        out_shape=out_shape,
        scratch_shapes=[pltpu.VMEM((tt + 2 * HALO, d), F32), pltpu.VMEM((tt + 2 * HALO, 2 * ML_WIDTH), F32)],
        compiler_params=_cparams(("parallel", "parallel")),
        name=f"proj_t{tt}",
    )(x, prev, nxt, w["g_mix"], w["w_main"], w["w_g"], w["w_gt"], w["b_g"], w["b_gt"],
      w["g_qk"], w["inv_lane"], w["conv_w"], w["conv_b"])


def _attn_kernel(tq, tk, nk,
                 q_ref, k_ref, v_ref, kt_ref, vt_ref, lam_ref, gsub_ref, o_ref,
                 qq_ref, m_ref, acc_ref, sa_ref, sb_ref):
    q = q_ref[0]
    lane = lax.broadcasted_iota(jnp.int32, (tq, LANES), 1)
    zero = jnp.zeros_like(q)
    qq_ref[0:tq, :] = jnp.where(lane < ATT_QK_DIM, q, zero)
    qq_ref[tq:2 * tq, :] = jnp.where(lane >= ATT_QK_DIM, q, zero)
    m_ref[...] = jnp.full(m_ref.shape, NEG_INF, F32)
    acc_ref[...] = jnp.zeros(acc_ref.shape, F32)

    def scores(kb):
        return lax.dot_general(qq_ref[...], kb, (((1,), (1,)), ((), ())), preferred_element_type=F32)

    def update(s, vb):
        m_prev = m_ref[...]
        m_new = jnp.maximum(m_prev, jnp.max(s, axis=1, keepdims=True))
        alpha = jnp.exp2(m_prev - m_new)
        p = jnp.exp2(s - m_new[:, 0:1]).astype(BF16)
        pv = jnp.dot(p, vb, preferred_element_type=F32)
        acc_ref[:, 0:LANES] = alpha * acc_ref[:, 0:LANES] + pv[:, 0:LANES]
        acc_ref[:, LANES:2 * LANES] = alpha * acc_ref[:, LANES:2 * LANES] + pv[:, LANES:2 * LANES]
        m_ref[...] = m_new

    def kblock(j):
        return k_ref[0, pl.ds(pl.multiple_of(j * tk, tk), tk), :]

    def vblock(j):
        return v_ref[0, pl.ds(pl.multiple_of(j * tk, tk), tk), :]

    sa_ref[...] = scores(kblock(0))

    def body(i, carry):
        j = 2 * i
        sb_ref[...] = scores(kblock(j + 1))
        update(sa_ref[...], vblock(j))
        sa_ref[...] = scores(kblock(j + 2))
        update(sb_ref[...], vblock(j + 1))
        return carry

    lax.fori_loop(0, nk // 2 - 1, body, 0)
    sb_ref[...] = scores(kblock(nk - 1))
    update(sa_ref[...], vblock(nk - 2))
    col = lax.broadcasted_iota(jnp.int32, (2 * tq, LANES), 1)
    s_meta = jnp.where(col < N_META, scores(kt_ref[0]), NEG_INF)
    update(sb_ref[...], vblock(nk - 1))
    update(s_meta, vt_ref[0])

    o1 = acc_ref[0:tq, 0:LANES] / acc_ref[0:tq, LANES:2 * LANES]
    o2 = acc_ref[tq:2 * tq, 0:LANES] / acc_ref[tq:2 * tq, LANES:2 * LANES]
    lam = (jnp.exp(jnp.sum(lam_ref[0:1, :] * lam_ref[1:2, :], axis=1, keepdims=True))
           - jnp.exp(jnp.sum(lam_ref[2:3, :] * lam_ref[3:4, :], axis=1, keepdims=True)) + LAM_INIT)
    o = o1 - lam * o2
    ms = jnp.mean(o * o, axis=-1, keepdims=True)
    o_ref[0] = (o * lax.rsqrt(ms + EPS) * gsub_ref[...] * (1.0 - LAM_INIT)).astype(BF16)


def _att_key_block(s):
    resident = s * 3 * LANES * 2 * 2
    for tk in (2 * ATT_TK, ATT_TK):
        scores = 2 * (2 * ATT_TQ) * tk * 4
        if s % (2 * tk) == 0 and resident + scores <= VMEM_LIMIT * 3 // 5:
            return tk
    raise ValueError(f"sequence length {s} is not a multiple of {2 * ATT_TK}")


def _attn(q, k, v, kt, vt, lam4, g_sub):
    b, s, _ = q.shape
    tq, tk = ATT_TQ, _att_key_block(s)
    kspec = lambda rows: pl.BlockSpec((1, rows, LANES), lambda bi, hi, qi: (bi, 0, hi))
    vspec = lambda rows: pl.BlockSpec((1, rows, 2 * LANES), lambda bi, hi, qi: (bi, 0, hi))
    return pl.pallas_call(
        functools.partial(_attn_kernel, tq, tk, s // tk),
        grid=(b, ATT_HEADS, s // tq),
        in_specs=[pl.BlockSpec((1, tq, LANES), lambda bi, hi, qi: (bi, qi, hi)),
                  kspec(s), vspec(s), kspec(LANES), vspec(LANES),
                  _const_spec((4, ATT_QK_DIM)), _const_spec((1, ATT_V_DIM))],
        out_specs=pl.BlockSpec((1, tq, LANES), lambda bi, hi, qi: (bi, qi, hi)),
        out_shape=jax.ShapeDtypeStruct((b, s, ATT_WIDTH), BF16),
        scratch_shapes=[pltpu.VMEM((2 * tq, LANES), BF16), pltpu.VMEM((2 * tq, LANES), F32),
                        pltpu.VMEM((2 * tq, 2 * LANES), F32),
                        pltpu.VMEM((2 * tq, tk), F32), pltpu.VMEM((2 * tq, tk), F32)],
        compiler_params=_cparams(("parallel", "parallel", "parallel")),
        name="diff_attn",
    )(q, k, v, kt, vt, lam4, g_sub)


def _mlstm_chunks(dirs):
    r = lax.broadcasted_iota(jnp.int32, (CHUNK, CHUNK), 0)
    c = lax.broadcasted_iota(jnp.int32, (CHUNK, CHUNK), 1)
    ones_col = (c == 0).astype(BF16)
    chains = []
    for reverse, q, k, v, g_tm, g_fm, s_ref, m_ref, h_ref in dirs:
        causal = (c >= r) if reverse else (c <= r)
        last = 0 if reverse else CHUNK - 1
        gbase = 2 * ML_HEADS if reverse else 0
        b_cols = jnp.dot(causal.astype(F32), g_tm, preferred_element_type=F32,
                         precision=lax.Precision.HIGHEST)
        b_rows = jnp.dot(g_fm, ((r >= c) if reverse else (r <= c)).astype(F32),
                         preferred_element_type=F32, precision=lax.Precision.HIGHEST)
        for h in range(ML_HEADS):
            fc = gbase + ML_HEADS + h
            sl = slice(h * ML_DIM, (h + 1) * ML_DIM)
            chains.append(dict(
                causal=causal, last=last, h=h, sl=sl, s_ref=s_ref, m_ref=m_ref, h_ref=h_ref,
                q=q[:, sl], k=k[:, sl], v=v[:, sl],
                b_col=b_cols[:, fc:fc + 1], b_row=b_rows[fc:fc + 1, :],
                li_row=g_fm[gbase + h:gbase + h + 1, :], li_col=g_tm[:, gbase + h:gbase + h + 1]))
    n = range(len(chains))
    ch = chains
    m_prev = [ch[i]["m_ref"][ch[i]["h"]:ch[i]["h"] + 1, 0:1] for i in n]
    log_d = [jnp.where(ch[i]["causal"], ch[i]["b_col"] - ch[i]["b_row"] + ch[i]["li_row"], NEG_INF) for i in n]
    m_inter = [ch[i]["b_col"] + m_prev[i] for i in n]
    mj = [jnp.maximum(m_inter[i], jnp.max(log_d[i], axis=1, keepdims=True)) for i in n]
    vext = [jnp.concatenate([ch[i]["v"], ones_col], axis=1) for i in n]
    qk = [lax.dot_general(ch[i]["q"], ch[i]["k"], (((1,), (1,)), ((), ())), preferred_element_type=F32)
          for i in n]
    state = [ch[i]["s_ref"][ch[i]["h"]] for i in n]
    inter = [jnp.dot(ch[i]["q"], state[i].astype(BF16), preferred_element_type=F32) for i in n]
    b_last = [ch[i]["b_col"][ch[i]["last"]:ch[i]["last"] + 1, :] for i in n]
    m_new = [jnp.maximum(b_last[i] + m_prev[i],
                         jnp.max(b_last[i] - ch[i]["b_row"] + ch[i]["li_row"], axis=1, keepdims=True)) for i in n]
    w = [(qk[i] * jnp.exp(log_d[i] - mj[i])).astype(BF16) for i in n]
    kw = [(ch[i]["k"].astype(F32)
           * jnp.exp(b_last[i] - ch[i]["b_col"] + ch[i]["li_col"] - m_new[i])).astype(BF16) for i in n]
    tot = [jnp.exp(m_inter[i] - mj[i]) * inter[i] + jnp.dot(w[i], vext[i], preferred_element_type=F32)
           for i in n]
    upd = [lax.dot_general(kw[i], vext[i], (((0,), (0,)), ((), ())), preferred_element_type=F32) for i in n]
    for i in n:
        den = tot[i][:, ML_DIM:ML_DIM + 1]
        ch[i]["h_ref"][0, :, ch[i]["sl"]] = tot[i][:, :ML_DIM] / jnp.maximum(jnp.abs(den), jnp.exp(-mj[i]))
    for i in n:
        h = ch[i]["h"]
        ch[i]["s_ref"][h] = jnp.exp(b_last[i] + m_prev[i] - m_new[i]) * state[i] + upd[i]
        ch[i]["m_ref"][h:h + 1, :] = jnp.broadcast_to(m_new[i], (1, LANES))


def _mlstm_kernel(qf_ref, kf_ref, vf_ref, gtf_ref, gff_ref, qb_ref, kb_ref, vb_ref, gtb_ref, gfb_ref,
                  s0_ref, m0_ref, hf_ref, hb_ref, sfin_ref, mfin_ref,
                  sf_ref, sb_ref, mf_ref, mb_ref):
    j = pl.program_id(1)

    @pl.when(j == 0)
    def _():
        sf_ref[...] = s0_ref[0]
        mf_ref[...] = m0_ref[0]
        sb_ref[...] = jnp.zeros(sb_ref.shape, F32)
        mb_ref[...] = jnp.zeros(mb_ref.shape, F32)

    _mlstm_chunks([(False, qf_ref[0], kf_ref[0], vf_ref[0], gtf_ref[0], gff_ref[0], sf_ref, mf_ref, hf_ref)])
    _mlstm_chunks([(True, qb_ref[0], kb_ref[0], vb_ref[0], gtb_ref[0], gfb_ref[0], sb_ref, mb_ref, hb_ref)])

    @pl.when(j == pl.num_programs(1) - 1)
    def _():
        sfin_ref[0] = sf_ref[...]
        mfin_ref[0] = mf_ref[...]


def _mlstm(qm, km, vm, g_tm, g_fm, s0, m0):
    b, s, _ = qm.shape
    nc = s // CHUNK
    fwd = lambda width: pl.BlockSpec((1, CHUNK, width), lambda bi, ci: (bi, ci, 0))
    bwd = lambda width: pl.BlockSpec((1, CHUNK, width), lambda bi, ci: (bi, nc - 1 - ci, 0))
    gf_fwd = pl.BlockSpec((1, N_GATES, CHUNK), lambda bi, ci: (bi, 0, ci))
    gf_bwd = pl.BlockSpec((1, N_GATES, CHUNK), lambda bi, ci: (bi, 0, nc - 1 - ci))
    st = pl.BlockSpec((1, ML_HEADS, ML_DIM, 2 * LANES), lambda bi, ci: (bi, 0, 0, 0))
    mx = pl.BlockSpec((1, SUBLANES, LANES), lambda bi, ci: (bi, 0, 0))
    return pl.pallas_call(
        _mlstm_kernel,
        grid=(b, nc),
        in_specs=[fwd(ML_WIDTH)] * 3 + [fwd(N_GATES), gf_fwd] + [bwd(ML_WIDTH)] * 3 + [bwd(N_GATES), gf_bwd]
        + [st, mx],
        out_specs=[fwd(ML_WIDTH), bwd(ML_WIDTH), st, mx],
        out_shape=[jax.ShapeDtypeStruct((b, s, ML_WIDTH), F32)] * 2
        + [jax.ShapeDtypeStruct((b, ML_HEADS, ML_DIM, 2 * LANES), F32),
           jax.ShapeDtypeStruct((b, SUBLANES, LANES), F32)],
        scratch_shapes=[pltpu.VMEM((ML_HEADS, ML_DIM, 2 * LANES), F32)] * 2
        + [pltpu.VMEM((SUBLANES, LANES), F32)] * 2,
        compiler_params=_cparams(("parallel", "arbitrary")),
        name="mlstm",
    )(qm, km, vm, g_tm, g_fm, qm, km, vm, g_tm, g_fm, s0, m0)


def _sorting_network(n):
    pairs = []
    p = 1
    while p < n:
        k = p
        while k >= 1:
            for j in range(k % p, n - k, 2 * k):
                for i in range(min(k, n - j - k)):
                    if (i + j) // (2 * p) == (i + j + k) // (2 * p):
                        pairs.append((i + j, i + j + k))
            k //= 2
        p *= 2
    return pairs


def _top_values_sorted(xs, n):
    sorted_groups = []
    for x in xs:
        groups = [x[SUBLANES * g:SUBLANES * (g + 1)] for g in range(x.shape[0] // SUBLANES)]
        size = 1
        while size < len(groups):
            size *= 2
        groups += [jnp.full_like(groups[0], NEG_INF)] * (size - len(groups))
        for i, j in _sorting_network(size):
            groups[i], groups[j] = jnp.maximum(groups[i], groups[j]), jnp.minimum(groups[i], groups[j])
        sorted_groups.append(groups)
    vals = [[] for _ in xs]
    for r in range(n):
        for groups, out in zip(sorted_groups, vals):
            mx = jnp.max(groups[0], axis=0, keepdims=True)
            out.append(mx)
            if r + 1 < n:
                hit = groups[0] == mx
                live = min(len(groups), n - 1 - r)
                for d in range(live):
                    below = groups[d + 1] if d + 1 < len(groups) else NEG_INF
                    groups[d] = jnp.where(hit, below, groups[d])
    return vals


def _count_greater(vals, x):
    n = len(vals)

    def pivot(decided, lo, size):
        if not decided:
            return vals[lo + size // 2 - 1]
        half = size // 2
        return jnp.where(decided[0], pivot(decided[1:], lo + half, half), pivot(decided[1:], lo, half))

    decided = []
    count = jnp.zeros_like(x)
    size = n
    while size >= 2:
        above = pivot(decided, 0, n) > x
        decided.append(above)
        count = count + jnp.where(above, float(size // 2), 0.0)
        size //= 2
    return jnp.where(vals[n - 1] > x, float(n), count)


def _bf16_pair_bits(x):
    u = lax.bitcast_convert_type(x.astype(BF16).astype(F32), jnp.uint32)
    return u | (u >> 16)


_CAND_PAIRS = [(a, b) for a in range(PEER_TOPK + 1) for b in range(PEER_TOPK + 1)
               if (a + 1) * (b + 1) <= PEER_TOPK + 1]
_CAND_ROWS = -(-len(_CAND_PAIRS) // SUBLANES) * SUBLANES


def _mix_kernel(tt, x_ref, att_ref, hf_ref, hb_ref, om_ref, gmh_ref, wout_ref, gffn_ref, wpqt_ref,
                k1_ref, k2_ref, h1_ref, xn2_ref, r2_ref, e2_ref, n1_ref, e1_ref, mix_ref, cand_ref):
    mix_ref[:, 0:ATT_WIDTH] = att_ref[0]
    for h in range(ML_HEADS):
        sl = slice(h * ML_DIM, (h + 1) * ML_DIM)
        hs = hf_ref[0, :, sl] + hb_ref[0, :, sl]
        ms = jnp.mean(hs * hs, axis=-1, keepdims=True)
        hn = hs * lax.rsqrt(ms + EPS) * gmh_ref[...]
        gate = 1.0 / (1.0 + jnp.exp(-om_ref[0, :, sl].astype(F32)))
        mix_ref[:, ATT_WIDTH + h * ML_DIM:ATT_WIDTH + (h + 1) * ML_DIM] = (hn * gate).astype(BF16)
    h1 = x_ref[0] + jnp.dot(mix_ref[...], wout_ref[...], preferred_element_type=F32)
    h1_ref[0] = h1
    ms = jnp.mean(h1 * h1, axis=-1, keepdims=True)
    xn2 = (h1 * lax.rsqrt(ms + EPS) * gffn_ref[...]).astype(BF16)
    xn2_ref[0] = xn2
    qt = lax.dot_general(wpqt_ref[...], xn2, (((1,), (1,)), ((), ())), preferred_element_type=F32)
    half = PEER_QDIM // 2
    cand_ref[...] = jnp.full(cand_ref.shape, NEG_INF, F32)
    for h in range(PEER_HEADS):
        q1 = qt[h * PEER_QDIM:h * PEER_QDIM + half].astype(BF16)
        q2 = qt[h * PEER_QDIM + half:(h + 1) * PEER_QDIM].astype(BF16)
        s1 = jnp.dot(k1_ref[h], q1, preferred_element_type=F32)
        s2 = jnp.dot(k2_ref[h], q2, preferred_element_type=F32)
        for sl in range(tt // ROUTE_SLAB):
            lanes = slice(sl * ROUTE_SLAB, (sl + 1) * ROUTE_SLAB)
            a1 = s1[:, lanes]
            a2 = s2[:, lanes]
            v1, v2 = _top_values_sorted([a1, a2], PEER_TOPK + 1)
            rank2 = _count_greater(v2[:PEER_TOPK], a2)
            for idx, (a, b) in enumerate(_CAND_PAIRS):
                cand_ref[idx:idx + 1, lanes] = v1[a] + v2[b]
            cs, = _top_values_sorted([cand_ref[:, lanes]], PEER_TOPK + 1)
            z = jnp.zeros_like(cs[0])
            for i in range(PEER_TOPK):
                z = z + jnp.exp(cs[i] - cs[0])
            thr = 0.5 * (cs[PEER_TOPK - 1] + cs[PEER_TOPK])
            n1 = _count_greater(v2[:PEER_TOPK // 2], thr - a1)
            n_best = jnp.zeros_like(thr)
            for b in range(PEER_TOPK // 2, PEER_TOPK):
                n_best = n_best + jnp.where(v2[b] > thr - v1[0], 1.0, 0.0)
            n1 = n1 + jnp.where(a1 == v1[0], n_best, 0.0)
            r2_ref[0, h, :, lanes] = rank2.astype(BF16)
            n1_ref[0, h, :, lanes] = _bf16_pair_bits(n1)
            e1_ref[0, h, :, lanes] = _bf16_pair_bits(jnp.exp(a1 - v1[0]) * (0.5 / z))
            e2_ref[0, h, :, lanes] = jnp.exp(a2 - v2[0]).astype(BF16)


def _mix(x, att, hf, hb, om, w):
    b, s, d = x.shape
    tt = MIX_TILE
    nt = s // tt
    tspec = lambda width: pl.BlockSpec((1, tt, width), lambda bi, ti: (bi, ti, 0))
    sspec = pl.BlockSpec((1, PEER_HEADS, PEER_NKEYS, tt), lambda bi, ti: (bi, 0, 0, ti))
    return pl.pallas_call(
        functools.partial(_mix_kernel, tt),
        grid=(b, nt),
        in_specs=[tspec(d), tspec(ATT_WIDTH), tspec(ML_WIDTH), tspec(ML_WIDTH), tspec(ML_WIDTH),
                  _const_spec((1, ML_DIM)), _const_spec((d, d)), _const_spec((1, d)),
                  _const_spec((PEER_HEADS * PEER_QDIM, d)),
                  _const_spec((PEER_HEADS, PEER_NKEYS, PEER_QDIM // 2)),
                  _const_spec((PEER_HEADS, PEER_NKEYS, PEER_QDIM // 2))],
        out_specs=[tspec(d), tspec(d), sspec, sspec, sspec, sspec],
        out_shape=[jax.ShapeDtypeStruct((b, s, d), F32), jax.ShapeDtypeStruct((b, s, d), BF16),
                   jax.ShapeDtypeStruct((b, PEER_HEADS, PEER_NKEYS, s), BF16),
                   jax.ShapeDtypeStruct((b, PEER_HEADS, PEER_NKEYS, s), BF16),
                   jax.ShapeDtypeStruct((b, PEER_HEADS, PEER_NKEYS, s), jnp.uint32),
                   jax.ShapeDtypeStruct((b, PEER_HEADS, PEER_NKEYS, s), jnp.uint32)],
        scratch_shapes=[pltpu.VMEM((tt, d), BF16), pltpu.VMEM((_CAND_ROWS, tt), F32)],
        compiler_params=_cparams(("parallel", "parallel")),
        name="mix_route",
    )(x, att, hf, hb, om, w["g_mh"], w["w_out"], w["g_ffn"], w["w_pqt"], w["sub_k1"], w["sub_k2"])


def _gelu_x2(x):
    return x + x * lax.erf(x * (2.0 ** -0.5))


def _peer_kernel(tt, ng, xn2_ref, h1_ref, r2_ref, e2_ref, n1_ref, e1_ref, u_ref, vt_ref, y_ref,
                 a0_ref, a1_ref, w_ref, acc_ref):
    s = pl.program_id(1)
    rows = PEER_GROUP // PEER_NKEYS

    @pl.when(s == 0)
    def _():
        a0_ref[...] = jnp.zeros(a0_ref.shape, F32)
        a1_ref[...] = jnp.zeros(a1_ref.shape, F32)

    @pl.when((s < 1) | ((s + ng - 1) % ng == 0))
    def _():
        acc_ref[...] = jnp.zeros(acc_ref.shape, F32)

    g2 = (s + ng - 1) % ng

    def row(ref, h, c, lanes):
        words = jnp.broadcast_to(ref[0, h, pl.ds(c, 1), lanes], (SUBLANES, PEER_PASS_LANES))
        tile = pltpu.bitcast(words, BF16)
        return jnp.concatenate([tile] * (PEER_NKEYS // (2 * SUBLANES)), axis=0)

    def stages(a_cur, a_oth):
        zero = jnp.zeros((PEER_NKEYS, PEER_PASS_LANES), BF16)
        tok_half = tt // 2
        for kb in range(PEER_GROUP // PEER_VBLOCK):
            for cc in range(kb * PEER_VBLOCK // PEER_NKEYS, (kb + 1) * PEER_VBLOCK // PEER_NKEYS):
                c = g2 * rows + cc
                ex = slice(cc * PEER_NKEYS, (cc + 1) * PEER_NKEYS)
                for lb in range(tt // PEER_PASS_LANES):
                    lanes = pl.ds(lb * PEER_PASS_LANES, PEER_PASS_LANES)
                    wsum = zero
                    for h in range(PEER_HEADS):
                        sel = r2_ref[0, h, :, lanes] < row(n1_ref, h, c, lanes)
                        wsum = wsum + jnp.where(sel, e2_ref[0, h, :, lanes], zero) * row(e1_ref, h, c, lanes)
                    w_ref[ex, lanes] = wsum * _gelu_x2(a_oth[ex, lanes]).astype(BF16)
            if kb < 2:
                tk = slice(kb * tok_half, (kb + 1) * tok_half)
                a_cur[:, tk] = lax.dot_general(u_ref[...], xn2_ref[0, tk, :], (((1,), (1,)), ((), ())),
                                               preferred_element_type=F32)
            blk = slice(kb * PEER_VBLOCK, (kb + 1) * PEER_VBLOCK)
            acc_ref[...] += jnp.dot(vt_ref[0, :, blk], w_ref[blk, :], preferred_element_type=F32)

    @pl.when(s % 2 == 0)
    def _():
        stages(a0_ref, a1_ref)

    @pl.when(s % 2 == 1)
    def _():
        stages(a1_ref, a0_ref)

    @pl.when((s >= 1) & ((s + ng - 1) % ng == ng - 1))
    def _():
        y_ref[0] = h1_ref[0] + acc_ref[...].T


def _peer(xn2, h1, r2, e2, n1, e1, w):
    b, s, d = h1.shape
    tt = PEER_TILE
    nt = s // tt
    ng = (PEER_NKEYS * PEER_NKEYS) // PEER_GROUP
    tile = lambda si, lag: jnp.clip((si - lag) // ng, 0, nt - 1)
    tspec = lambda lag: pl.BlockSpec((1, tt, d), lambda bi, si: (bi, tile(si, lag), 0))
    sspec = pl.BlockSpec((1, PEER_HEADS, PEER_NKEYS, tt), lambda bi, si: (bi, 0, 0, tile(si, 1)))
    return pl.pallas_call(
        functools.partial(_peer_kernel, tt, ng),
        grid=(b, nt * ng + 1),
        in_specs=[tspec(0), tspec(1), sspec, sspec, sspec, sspec,
                  pl.BlockSpec((PEER_GROUP, d), lambda bi, si: (si % ng, 0)),
                  pl.BlockSpec((1, d, PEER_GROUP), lambda bi, si: ((si + ng - 1) % ng, 0, 0))],
        out_specs=tspec(1),
        out_shape=jax.ShapeDtypeStruct((b, s, d), F32),
        scratch_shapes=[pltpu.VMEM((PEER_GROUP, tt), F32)] * 2 + [pltpu.VMEM((PEER_GROUP, tt), BF16)]
        + [pltpu.VMEM((d, tt), F32)],
        compiler_params=_cparams(("parallel", "arbitrary")),
        name="peer_dense",
    )(xn2, h1, r2, e2, n1, e1, w["peer_u"], w["peer_vt"])


def _prep_weights(g_mix, w_in, b_gates, conv_w, conv_b, g_qn, g_kn, lam_q1, lam_k1, lam_q2, lam_k2,
                  g_sub, g_mh, w_out, g_ffn, w_pq, sub_k1, sub_k2, peer_u, peer_v):
    half = ROT_DIM // 2
    inv = ROPE_THETA ** (-jnp.arange(half, dtype=F32) * 2.0 / ROT_DIM)
    w_g = w_in[0][:, N_MAIN:]
    return {
        "g_mix": g_mix[0][None, :],
        "w_main": w_in[0][:, :N_MAIN].astype(BF16),
        "w_g": w_g.astype(BF16),
        "w_gt": w_g.T.astype(BF16),
        "b_g": b_gates[0][None, :],
        "b_gt": b_gates[0][:, None],
        "g_qk": jnp.concatenate([jnp.tile(g_qn[0], ATT_WIDTH // ATT_QK_DIM),
                                 jnp.tile(g_kn[0], ATT_WIDTH // ATT_QK_DIM)])[None, :],
        "inv_lane": jnp.tile(inv, LANES // half)[None, :],
        "conv_w": conv_w[0][:, 0, :],
        "conv_b": conv_b[0][None, :],
        "lam4": jnp.stack([lam_q1[0], lam_k1[0], lam_q2[0], lam_k2[0]]),
        "g_sub": g_sub[0][None, :],
        "g_mh": g_mh[0][None, :],
        "w_out": w_out[0].astype(BF16),
        "g_ffn": g_ffn[0][None, :],
        "w_pqt": w_pq[0].T.astype(BF16),
        "sub_k1": sub_k1[0].astype(BF16),
        "sub_k2": sub_k2[0].astype(BF16),
        "peer_u": peer_u[0].astype(BF16),
        "peer_vt": peer_v[0].astype(BF16).reshape(-1, PEER_GROUP, D_MODEL).transpose(0, 2, 1),
    }


def _trunk(x, meta, w):
    b, s, d = x.shape
    tt = PROJ_TILE
    assert s % tt == 0 and s % (2 * ATT_TK) == 0 and s % MIX_TILE == 0 and s % PEER_TILE == 0
    nt = s // tt
    xr = x.reshape(b, nt, tt, d)
    meta_tail = jnp.broadcast_to(meta[N_META - HALO:][None, None], (b, 1, HALO, d))
    prev = jnp.concatenate([meta_tail, xr[:, :-1, tt - HALO:, :]], axis=1)
    nxt = jnp.concatenate([xr[:, 1:, :HALO, :], jnp.zeros((b, 1, HALO, d), x.dtype)], axis=1)
    q, k, v, qm, km, vm, om, g_tm, g_fm = _proj(x, prev, nxt, N_META, tt, w)

    meta_b = jnp.broadcast_to(meta[None], (b, N_META, d))
    m_prev = jnp.zeros((b, 1, HALO, d), x.dtype)
    m_next = x[:, None, :HALO, :]
    _, k_m, v_m, qm_m, km_m, vm_m, _, gtm_m, gfm_m = _proj(meta_b, m_prev, m_next, 0, N_META, w)

    pad = CHUNK - N_META
    att = _attn(q, k, v, jnp.pad(k_m, ((0, 0), (0, pad), (0, 0))), jnp.pad(v_m, ((0, 0), (0, pad), (0, 0))),
                w["lam4"], w["g_sub"])

    front = lambda a: jnp.pad(a, ((0, 0), (pad, 0), (0, 0)))
    is_input_gate = (jnp.arange(N_GATES) // ML_HEADS) % 2 == 0
    pad_gate = jnp.where(is_input_gate, NEG_INF, 0.0).astype(F32)
    gtm_p = jnp.concatenate([jnp.broadcast_to(pad_gate[None, None, :], (b, pad, N_GATES)), gtm_m], axis=1)
    gfm_p = jnp.concatenate([jnp.broadcast_to(pad_gate[None, :, None], (b, N_GATES, pad)), gfm_m], axis=2)
    s_zero = jnp.zeros((b, ML_HEADS, ML_DIM, 2 * LANES), F32)
    m_zero = jnp.zeros((b, SUBLANES, LANES), F32)
    _, _, s0, m0 = _mlstm(front(qm_m), front(km_m), front(vm_m), gtm_p, gfm_p, s_zero, m_zero)
    hf, hb, _, _ = _mlstm(qm, km, vm, g_tm, g_fm, s0, m0)

    h1, xn2, r2, e2, n1, e1 = _mix(x, att, hf, hb, om, w)
    return _peer(xn2, h1, r2, e2, n1, e1, w)


def kernel(x_prompt, x_sample, meta, g_mix, w_in, b_gates, conv_w, conv_b, g_qn, g_kn, lam_q1, lam_k1,
           lam_q2, lam_k2, g_sub, g_mh, w_out, g_ffn, w_pq, sub_k1, sub_k2, peer_u, peer_v):
    w = _prep_weights(g_mix, w_in, b_gates, conv_w, conv_b, g_qn, g_kn, lam_q1, lam_k1, lam_q2, lam_k2,
                      g_sub, g_mh, w_out, g_ffn, w_pq, sub_k1, sub_k2, peer_u, peer_v)
    return (_trunk(x_prompt, meta, w), _trunk(x_sample, meta, w))
```
